```python
import math
import jax, jax.numpy as jnp
from jax import lax
import numpy as np

D_MODEL = 1024
BATCH = 4
SEQ = 4096
DEPTH = 2

GRID_W = 64
CTX_LEN = 256
EPS = 1e-6
CONV_W = 3

W_GROUP = D_MODEL // 4
MIX_WIDTH = 4 * W_GROUP

S5_CH = 16
S5_GROUPS = W_GROUP // S5_CH
S5_STATE = 64

HY_ORDER = 2
HY_BANDS = 16
HY_EMB = 1 + 2 * HY_BANDS
HY_HIDDEN = 64
HY_TARGET = 1e-2
HY_FAST_PCT = 0.3
HY_SLOW_PCT = 1.5

RET_HEADS = 4
RET_DK = W_GROUP // RET_HEADS
RET_DV = W_GROUP // RET_HEADS
RET_CHUNK = 128
RET_ROPE_BASE = 10000.0
RET_DECAY_MIN_EXP = 5.0
RET_DECAY_MAX_EXP = 12.0

MLA_HEADS = 4
MLA_NOPE = 64
MLA_ROPE = 32
MLA_V = W_GROUP // MLA_HEADS
MLA_Q_RANK = 192
MLA_KV_RANK = 128
MLA_QK = MLA_NOPE + MLA_ROPE
Q_BLOCK = 128
ROPE_BASE = 10000.0

D_FF = 2816

OFF_HY = W_GROUP
OFF_RET = OFF_HY + 3 * W_GROUP
OFF_MLA = OFF_RET + 4 * W_GROUP
IN_WIDTH = OFF_MLA + MLA_Q_RANK + MLA_KV_RANK + MLA_ROPE

kernel_name = 'hybrid_parallel_heads_dit_block'


def rmsnorm(x, g):
    x32 = x.astype(jnp.float32)
    y = x32 * lax.rsqrt(jnp.mean(x32 * x32, axis=-1, keepdims=True) + EPS)
    return (y * g.astype(jnp.float32)).astype(x.dtype)


def modulate(h, shift, scale):
    return h * (1 + scale) + shift


def dwconv(x, w, b):
    ch = x.shape[-1]
    y = lax.conv_general_dilated(x, w[:, None, :].astype(x.dtype), window_strides=(1,),
                                 padding=[(CONV_W // 2, CONV_W // 2)],
                                 dimension_numbers=('NWC', 'WIO', 'NWC'), feature_group_count=ch)
    return y + b.astype(x.dtype)


def apply_rope(x, cos, sin):
    half = x.shape[-1] // 2
    x1, x2 = x[..., :half], x[..., half:]
    return jnp.concatenate([x1 * cos - x2 * sin, x1 * sin + x2 * cos], axis=-1)


def axial_rope_tables(n_tokens, rot_dim):
    rows = n_tokens // GRID_W
    row = jnp.repeat(jnp.arange(rows), GRID_W).astype(jnp.float32)
    col = jnp.tile(jnp.arange(GRID_W), rows).astype(jnp.float32)
    n_freq = rot_dim // 4
    inv = ROPE_BASE ** (-jnp.arange(n_freq, dtype=jnp.float32) / n_freq)
    ang = jnp.concatenate([row[:, None] * inv, col[:, None] * inv], axis=-1)
    return jnp.cos(ang), jnp.sin(ang)


def conv_ffn(h, w_up, conv_w, conv_b, w_down):
    u = dwconv(h @ w_up, conv_w, conv_b)
    a, v = jnp.split(u, 2, axis=-1)
    return (jax.nn.silu(a) * v) @ w_down


def _linear_combine(e1, e2):
    a1, b1 = e1
    a2, b2 = e2
    return a1 * a2, a2 * b1 + b2


def s5_scan(u, abar, bbar, h0):
    bu = jnp.einsum('blgc,gpc->blgp', u.astype(jnp.complex64), bbar)
    a = jnp.broadcast_to(abar, bu.shape)
    a_cum, h = lax.associative_scan(_linear_combine, (a, bu), axis=1)
    if h0 is not None:
        h = h + a_cum * h0[:, None]
    return h


def s5_mixer(u_lat, u_ctx, lam_re, lam_im, log_dt, b_re, b_im, c_re, c_im, d_skip, glu_w, glu_b, need_ctx):
    f32 = jnp.float32
    lam = lax.complex(lam_re.astype(f32), lam_im.astype(f32))
    lam_dt = lam * jnp.exp(log_dt.astype(f32))[..., None]
    abar = jnp.exp(lam_dt)
    bbar = ((abar - 1.0) / lam)[..., None] * lax.complex(b_re.astype(f32), b_im.astype(f32))
    cmat = lax.complex(c_re.astype(f32), c_im.astype(f32))

    def groups(u):
        return u.astype(f32).reshape(u.shape[0], u.shape[1], S5_GROUPS, S5_CH)

    def readout(u, h_f, h_b_rev):
        bsz, n = u.shape[0], u.shape[1]
        y = (jnp.einsum('blgp,gcp->blgc', h_f, cmat[0])
             + jnp.einsum('blgp,gcp->blgc', h_b_rev[:, ::-1], cmat[1])).real
        y = y.reshape(bsz, n, W_GROUP) + d_skip.astype(f32) * u.astype(f32)
        y = jax.nn.gelu(y)
        y = y * jax.nn.sigmoid(y @ glu_w.astype(f32) + glu_b.astype(f32))
        return y.astype(u.dtype)

    uc, ul = groups(u_ctx), groups(u_lat)
    hc_f = s5_scan(uc, abar[0], bbar[0], None)
    hc_b = s5_scan(uc[:, ::-1], abar[1], bbar[1], None)
    hl_f = s5_scan(ul, abar[0], bbar[0], hc_f[:, -1])
    hl_b = s5_scan(ul[:, ::-1], abar[1], bbar[1], hc_b[:, -1])
    y_lat = readout(u_lat, hl_f, hl_b)
    y_ctx = readout(u_ctx, hc_f, hc_b) if need_ctx else None
    return y_lat, y_ctx


def hyena_spectra(n_tokens, w1, b1, w2, b2, w3, freq, deltas):
    f32 = jnp.float32
    pos = jnp.arange(n_tokens, dtype=f32)
    t01 = pos / (n_tokens - 1)
    bands = jnp.linspace(1e-4, HY_BANDS - 1, HY_BANDS, dtype=f32)
    ang = (2.0 * math.pi / n_tokens) * pos[:, None] * bands[None, :]
    z = jnp.concatenate([t01[:, None], jnp.cos(ang), -jnp.sin(ang)], axis=-1)
    fr = freq.astype(f32)
    hid = jnp.sin(fr * (z @ w1.astype(f32) + b1.astype(f32)))
    hid = jnp.sin(fr * (hid @ w2.astype(f32) + b2.astype(f32)))
    k = (hid @ w3.astype(f32)) * jnp.exp(-t01[:, None] * jnp.abs(deltas.astype(f32)))
    k = k.reshape(n_tokens, HY_ORDER, 2, W_GROUP)
    k = k * lax.rsqrt(jnp.sum(k * k, axis=(0, 2), keepdims=True) + EPS)
    k_fwd, k_bwd = k[:, :, 0], k[:, :, 1]
    k_two = jnp.concatenate([k_fwd, jnp.zeros((1, HY_ORDER, W_GROUP), f32), k_bwd[:0:-1]], axis=0)
    return jnp.fft.rfft(k_two, axis=0)


def hyena_operator(p, conv_w, conv_b, spectra, bias):
    n = p.shape[1]
    u = dwconv(p, conv_w, conv_b).astype(jnp.float32)
    x1, x2, v = jnp.split(u, 3, axis=-1)
    b = bias.astype(jnp.float32)

    def long_conv(z, o):
        zf = jnp.fft.rfft(z, n=2 * n, axis=1)
        return jnp.fft.irfft(zf * spectra[None, :, o], n=2 * n, axis=1)[:, :n] + z * b[o]

    z = x1 * long_conv(v, 0)
    z = x2 * long_conv(z, 1)
    return z.astype(p.dtype)


def hyena_mixer(p_lat, p_ctx, conv_w, conv_b, w1, b1, w2, b2, w3, freq, deltas, bias, need_ctx):
    filt = (w1, b1, w2, b2, w3, freq, deltas)
    y_lat = hyena_operator(p_lat, conv_w, conv_b, hyena_spectra(p_lat.shape[1], *filt), bias)
    y_ctx = hyena_operator(p_ctx, conv_w, conv_b, hyena_spectra(p_ctx.shape[1], *filt), bias) if need_ctx else None
    return y_lat, y_ctx


def retention_chunkwise(q, k, v, log_gamma, s0):
    bsz, n, h, dk = q.shape
    dv = v.shape[-1]
    nc = n // RET_CHUNK
    qc = q.reshape(bsz, nc, RET_CHUNK, h, dk)
    kc = k.reshape(bsz, nc, RET_CHUNK, h, dk)
    vc = v.reshape(bsz, nc, RET_CHUNK, h, dv)
    idx = jnp.arange(RET_CHUNK, dtype=jnp.float32)
    lg = log_gamma[:, None]
    diff = idx[:, None] - idx[None, :]
    dmask = jnp.where(diff >= 0, jnp.exp(jnp.maximum(diff, 0.0)[None] * log_gamma[:, None, None]), 0.0)
    zeta = jnp.exp((RET_CHUNK - 1 - idx)[None] * lg)
    xi = jnp.exp((idx + 1)[None] * lg)
    g_chunk = jnp.exp(RET_CHUNK * log_gamma)[:, None, None]
    inner = jnp.einsum('bnihd,bnjhd->bnhij', qc, kc) * dmask
    o_inner = jnp.einsum('bnhij,bnjhe->bnihe', inner, vc)
    ds = jnp.einsum('bnjhd,hj,bnjhe->nbhde', kc, zeta, vc)

    def step(s, ds_n):
        return g_chunk * s + ds_n, s

    s_final, s_prev = lax.scan(step, s0, ds)
    o_cross = jnp.einsum('bnihd,hi,nbhde->bnihe', qc, xi, s_prev)
    return (o_inner + o_cross).reshape(bsz, n, h, dv), s_final


def retention_mixer(p_lat, p_ctx, decay_exp, gn_g, need_ctx):
    f32 = jnp.float32
    log_gamma = jnp.log1p(-jnp.exp2(-decay_exp.astype(f32)))

    def project(p, rotate):
        bsz, n, _ = p.shape
        q, k, v, g = jnp.split(p.astype(f32), 4, axis=-1)
        q = q.reshape(bsz, n, RET_HEADS, RET_DK)
        k = k.reshape(bsz, n, RET_HEADS, RET_DK) * (RET_DK ** -0.5)
        v = v.reshape(bsz, n, RET_HEADS, RET_DV)
        if rotate:
            theta = RET_ROPE_BASE ** (-jnp.linspace(0.0, 1.0, RET_DK // 2, dtype=f32))
            ang = jnp.arange(n, dtype=f32)[:, None] * theta
            cos, sin = jnp.cos(ang)[:, None], jnp.sin(ang)[:, None]
            q, k = apply_rope(q, cos, sin), apply_rope(k, cos, sin)
        return q, k, v, g

    def output(o, g, dtype):
        bsz, n = o.shape[0], o.shape[1]
        o = o * lax.rsqrt(jnp.mean(o * o, axis=-1, keepdims=True) + EPS)
        o = o.reshape(bsz, n, W_GROUP) * gn_g.astype(f32)
        return (jax.nn.silu(g) * o).astype(dtype)

    qc, kc, vc, gc = project(p_ctx, False)
    ql, kl, vl, gl = project(p_lat, True)
    s_zero = jnp.zeros((p_ctx.shape[0], RET_HEADS, RET_DK, RET_DV), f32)
    oc_f, sc_f = retention_chunkwise(qc, kc, vc, log_gamma[0], s_zero)
    oc_b, sc_b = retention_chunkwise(qc[:, ::-1], kc[:, ::-1], vc[:, ::-1], log_gamma[1], s_zero)
    ol_f, _ = retention_chunkwise(ql, kl, vl, log_gamma[0], sc_f)
    ol_b, _ = retention_chunkwise(ql[:, ::-1], kl[:, ::-1], vl[:, ::-1], log_gamma[1], sc_b)
    y_lat = output(ol_f + ol_b[:, ::-1], gl, p_lat.dtype)
    y_ctx = output(oc_f + oc_b[:, ::-1], gc, p_ctx.dtype) if need_ctx else None
    return y_lat, y_ctx


def softmax_attend(q, k, v):
    s = jnp.einsum('bqhd,bkhd->bhqk', q, k).astype(jnp.float32) * (MLA_QK ** -0.5)
    p = jax.nn.softmax(s, axis=-1).astype(v.dtype)
    return jnp.einsum('bhqk,bkhd->bqhd', p, v)


def mla_mixer(p_lat, p_ctx, q_norm_g, kv_norm_g, w_uq, w_ukv, need_ctx):
    def split(p):
        return (p[..., :MLA_Q_RANK], p[..., MLA_Q_RANK:MLA_Q_RANK + MLA_KV_RANK],
                p[..., MLA_Q_RANK + MLA_KV_RANK:])

    def queries(c_q, rope):
        bsz, n, _ = c_q.shape
        q = (rmsnorm(c_q, q_norm_g) @ w_uq).reshape(bsz, n, MLA_HEADS, MLA_QK)
        if rope is not None:
            q = jnp.concatenate([q[..., :MLA_NOPE], apply_rope(q[..., MLA_NOPE:], *rope)], axis=-1)
        return q

    def keys_values(c_kv, k_rope, rope):
        bsz, n, _ = c_kv.shape
        kv = (rmsnorm(c_kv, kv_norm_g) @ w_ukv).reshape(bsz, n, MLA_HEADS, MLA_NOPE + MLA_V)
        k_r = k_rope[:, :, None, :]
        if rope is not None:
            k_r = apply_rope(k_r, *rope)
        k = jnp.concatenate([kv[..., :MLA_NOPE],
                             jnp.broadcast_to(k_r, (bsz, n, MLA_HEADS, MLA_ROPE)).astype(kv.dtype)], axis=-1)
        return k, kv[..., MLA_NOPE:]

    bsz, n = p_lat.shape[0], p_lat.shape[1]
    cq_l, ckv_l, kr_l = split(p_lat)
    cq_c, ckv_c, kr_c = split(p_ctx)
    cos, sin = axial_rope_tables(n, MLA_ROPE)
    rope = (cos[:, None], sin[:, None])
    k_c, v_c = keys_values(ckv_c, kr_c, None)
    k_l, v_l = keys_values(ckv_l, kr_l, rope)
    q_l = queries(cq_l, rope).astype(k_l.dtype)
    k_all = jnp.concatenate([k_c, k_l], axis=1)
    v_all = jnp.concatenate([v_c, v_l], axis=1)
    q_blocks = q_l.reshape(bsz, n // Q_BLOCK, Q_BLOCK, MLA_HEADS, MLA_QK).transpose(1, 0, 2, 3, 4)
    o = lax.map(lambda qb: softmax_attend(qb, k_all, v_all), q_blocks)
    y_lat = o.transpose(1, 0, 2, 3, 4).reshape(bsz, n, MLA_HEADS * MLA_V).astype(p_lat.dtype)
    y_ctx = None
    if need_ctx:
        y_ctx = softmax_attend(queries(cq_c, None), k_c, v_c).reshape(
            bsz, p_ctx.shape[1], MLA_HEADS * MLA_V).astype(p_ctx.dtype)
    return y_lat, y_ctx


def token_mixing(h, hc, w_in, w_out, s5_p, hy_p, ret_p, mla_p, need_ctx):
    pl, pc = h @ w_in, hc @ w_in
    parts = ((0, OFF_HY, s5_mixer, s5_p), (OFF_HY, OFF_RET, hyena_mixer, hy_p),
             (OFF_RET, OFF_MLA, retention_mixer, ret_p), (OFF_MLA, IN_WIDTH, mla_mixer, mla_p))
    outs = [fn(pl[..., a:b], pc[..., a:b], *prm, need_ctx=need_ctx) for a, b, fn, prm in parts]
    y = jnp.concatenate([o[0] for o in outs], axis=-1) @ w_out
    yc = jnp.concatenate([o[1] for o in outs], axis=-1) @ w_out if need_ctx else None
    return y, yc


def setup_inputs(seed: int = 0) -> dict:
    key = jax.random.key(seed)
    ks = iter(jax.random.split(key, 64))
    f32 = jnp.float32

    def nrm(shape, scale=1.0):
        return scale * jax.random.normal(next(ks), shape, f32)

    def gain(shape):
        return 1.0 + nrm(shape, 0.05)

    L, G, P = DEPTH, S5_GROUPS, S5_STATE
    n_filt = 2 * HY_ORDER * W_GROUP
    decay_lo = abs(math.log(HY_TARGET)) / HY_SLOW_PCT
    decay_hi = abs(math.log(HY_TARGET)) / HY_FAST_PCT
    return {
        'x': nrm((BATCH, SEQ, D_MODEL)),
        'c': nrm((BATCH, D_MODEL)),
        'ctx': nrm((BATCH, CTX_LEN, D_MODEL)),
        'c_ctx': nrm((D_MODEL,)),
        'ada_w': nrm((L, D_MODEL, 6 * D_MODEL), D_MODEL ** -0.5),
        'ada_b': nrm((L, 6 * D_MODEL), 0.02),
        'norm_g': gain((L, 4, D_MODEL)),
        'w_in': nrm((L, D_MODEL, IN_WIDTH), D_MODEL ** -0.5),
        'w_out': nrm((L, MIX_WIDTH, D_MODEL), MIX_WIDTH ** -0.5),
        's5_lam_re': -0.5 + nrm((L, 2, G, P), 0.01),
        's5_lam_im': math.pi * jnp.arange(P, dtype=f32) + nrm((L, 2, G, P), 0.01),
        's5_log_dt': jax.random.uniform(next(ks), (L, 2, G), f32, math.log(1e-3), math.log(1e-1)),
        's5_b_re': nrm((L, 2, G, P, S5_CH), (2 * S5_CH) ** -0.5),
        's5_b_im': nrm((L, 2, G, P, S5_CH), (2 * S5_CH) ** -0.5),
        's5_c_re': nrm((L, 2, G, S5_CH, P), (2 * P) ** -0.5),
        's5_c_im': nrm((L, 2, G, S5_CH, P), (2 * P) ** -0.5),
        's5_d': nrm((L, W_GROUP)),
        's5_glu_w': nrm((L, W_GROUP, W_GROUP), W_GROUP ** -0.5),
        's5_glu_b': nrm((L, W_GROUP), 0.02),
        'hy_conv_w': nrm((L, CONV_W, 3 * W_GROUP), CONV_W ** -0.5),
        'hy_conv_b': nrm((L, 3 * W_GROUP), 0.02),
        'hy_w1': nrm((L, HY_EMB, HY_HIDDEN), HY_EMB ** -0.5),
        'hy_b1': nrm((L, HY_HIDDEN), 0.02),
        'hy_w2': nrm((L, HY_HIDDEN, HY_HIDDEN), HY_HIDDEN ** -0.5),
        'hy_b2': nrm((L, HY_HIDDEN), 0.02),
        'hy_w3': nrm((L, HY_HIDDEN, n_filt), HY_HIDDEN ** -0.5),
        'hy_freq': gain((L, HY_HIDDEN)),
        'hy_deltas': jnp.linspace(decay_lo, decay_hi, n_filt, dtype=f32) * gain((L, n_filt)),
        'hy_bias': nrm((L, HY_ORDER, W_GROUP)),
        'ret_decay_exp': jnp.linspace(RET_DECAY_MIN_EXP, RET_DECAY_MAX_EXP, RET_HEADS, dtype=f32) + nrm((L, 2, RET_HEADS), 0.1),
        'ret_gn_g': gain((L, W_GROUP)),
        'mla_q_norm_g': gain((L, MLA_Q_RANK)),
        'mla_kv_norm_g': gain((L, MLA_KV_RANK)),
        'mla_w_uq': nrm((L, MLA_Q_RANK, MLA_HEADS * MLA_QK), MLA_Q_RANK ** -0.5),
        'mla_w_ukv': nrm((L, MLA_KV_RANK, MLA_HEADS * (MLA_NOPE + MLA_V)), MLA_KV_RANK ** -0.5),
        'ffn_w_up': nrm((L, D_MODEL, 2 * D_FF), D_MODEL ** -0.5),
        'ffn_conv_w': nrm((L, CONV_W, 2 * D_FF), CONV_W ** -0.5),
        'ffn_conv_b': nrm((L, 2 * D_FF), 0.02),
        'ffn_w_down': nrm((L, D_FF, D_MODEL), D_FF ** -0.5),
    }


def reference(x, c, ctx, c_ctx, ada_w, ada_b, norm_g, w_in, w_out,
              s5_lam_re, s5_lam_im, s5_log_dt, s5_b_re, s5_b_im, s5_c_re, s5_c_im, s5_d, s5_glu_w, s5_glu_b,
              hy_conv_w, hy_conv_b, hy_w1, hy_b1, hy_w2, hy_b2, hy_w3, hy_freq, hy_deltas, hy_bias,
              ret_decay_exp, ret_gn_g, mla_q_norm_g, mla_kv_norm_g, mla_w_uq, mla_w_ukv,
              ffn_w_up, ffn_conv_w, ffn_conv_b, ffn_w_down):
    sc_lat = jax.nn.silu(c)
    sc_ctx = jax.nn.silu(c_ctx)
    for l in range(DEPTH):
        need_ctx = l < DEPTH - 1
        mod = sc_lat @ ada_w[l] + ada_b[l]
        sh1, sc1, g1, sh2, sc2, g2 = [m[:, None, :] for m in jnp.split(mod, 6, axis=-1)]
        modc = sc_ctx @ ada_w[l] + ada_b[l]
        csh1, csc1, cg1, csh2, csc2, cg2 = jnp.split(modc, 6, axis=-1)
        s5_p = (s5_lam_re[l], s5_lam_im[l], s5_log_dt[l], s5_b_re[l], s5_b_im[l],
                s5_c_re[l], s5_c_im[l], s5_d[l], s5_glu_w[l], s5_glu_b[l])
        hy_p = (hy_conv_w[l], hy_conv_b[l], hy_w1[l], hy_b1[l], hy_w2[l], hy_b2[l],
                hy_w3[l], hy_freq[l], hy_deltas[l], hy_bias[l])
        ret_p = (ret_decay_exp[l], ret_gn_g[l])
        mla_p = (mla_q_norm_g[l], mla_kv_norm_g[l], mla_w_uq[l], mla_w_ukv[l])
        ffn_p = (ffn_w_up[l], ffn_conv_w[l], ffn_conv_b[l], ffn_w_down[l])

        h = modulate(rmsnorm(x, norm_g[l, 0]), sh1, sc1)
        hc = modulate(rmsnorm(ctx, norm_g[l, 0]), csh1, csc1)
        y, yc = token_mixing(h, hc, w_in[l], w_out[l], s5_p, hy_p, ret_p, mla_p, need_ctx)
        x = x + g1 * rmsnorm(y, norm_g[l, 1])
        h = modulate(rmsnorm(x, norm_g[l, 2]), sh2, sc2)
        x = x + g2 * rmsnorm(conv_ffn(h, *ffn_p), norm_g[l, 3])
        if need_ctx:
            ctx = ctx + cg1 * rmsnorm(yc, norm_g[l, 1])
            hc = modulate(rmsnorm(ctx, norm_g[l, 2]), csh2, csc2)
            ctx = ctx + cg2 * rmsnorm(conv_ffn(hc, *ffn_p), norm_g[l, 3])
    return x
```

```python
import functools
import math

import numpy as np
import jax
import jax.numpy as jnp
from jax import lax
from jax.experimental import pallas as pl
from jax.experimental.pallas import tpu as pltpu

F32 = jnp.float32
BF16 = jnp.bfloat16
HIGHEST = lax.Precision.HIGHEST

EPS = 1e-6
D_MODEL = 1024
W_GROUP = 256
GRID_W = 64
S5_CH, S5_GROUPS, S5_STATE = 16, 16, 64
S5_COLS = S5_GROUPS * S5_STATE
S5_SUB = 16
S5_GRP = 128
HY_ORDER, HY_BANDS, HY_HIDDEN = 2, 16, 64
RET_HEADS, RET_DK = 4, 64
RET_ROPE_BASE = 10000.0
MLA_HEADS, MLA_NOPE, MLA_ROPE, MLA_V = 4, 64, 32, 64
MLA_Q_RANK, MLA_KV_RANK = 192, 128
MLA_QK = MLA_NOPE + MLA_ROPE
MLA_HEAD_PAD = 128
ROPE_BASE = 10000.0
D_FF = 2816
FF_CHUNK = 256
N_IN = 2560
FFT_N2 = 64

VMEM_LIMIT_BYTES = 56 * 1024 * 1024


def _call(kernel, *, name, grid, in_specs, out_specs, out_shape, scratch_shapes=()):
    return pl.pallas_call(
        kernel, name=name, grid=grid, in_specs=in_specs, out_specs=out_specs, out_shape=out_shape,
        scratch_shapes=scratch_shapes,
        compiler_params=pltpu.CompilerParams(dimension_semantics=("arbitrary",) * len(grid),
                                             vmem_limit_bytes=VMEM_LIMIT_BYTES))


def _dot(a, b):
    return jnp.dot(a, b, preferred_element_type=F32)


def _dot_nt(a, b):
    return lax.dot_general(a, b, (((1,), (1,)), ((), ())), preferred_element_type=F32)


def _dot_tn(a, b):
    return lax.dot_general(a, b, (((0,), (0,)), ((), ())), preferred_element_type=F32)


def _rms(x, g):
    return x * lax.rsqrt(jnp.mean(x * x, axis=-1, keepdims=True) + EPS) * g


def _silu(x):
    return x * jax.nn.sigmoid(x)


def _mod_kernel(c_ref, w_ref, b_ref, o_ref):
    s = _silu(c_ref[...])
    o_ref[0] = jnp.dot(s, w_ref[0], preferred_element_type=F32, precision=HIGHEST) + b_ref[0]


def _adaln(c, c_ctx, ada_w, ada_b):
    bsz, depth, n6 = c.shape[0], ada_w.shape[0], ada_w.shape[2]
    rows = 8
    assert bsz + 1 <= rows
    cc = jnp.concatenate([c, c_ctx[None], jnp.zeros((rows - bsz - 1, D_MODEL), F32)], axis=0)
    tn = 512
    out = _call(
        _mod_kernel, name="adaln", grid=(depth, n6 // tn),
        in_specs=[pl.BlockSpec((rows, D_MODEL), lambda l, j: (0, 0)),
                  pl.BlockSpec((1, D_MODEL, tn), lambda l, j: (l, 0, j)),
                  pl.BlockSpec((1, 1, tn), lambda l, j: (l, 0, j))],
        out_specs=pl.BlockSpec((1, rows, tn), lambda l, j: (l, 0, j)),
        out_shape=jax.ShapeDtypeStruct((depth, rows, n6), F32))(cc, ada_w, ada_b.reshape(depth, 1, n6))
    mod = out.reshape(depth, rows, 6, D_MODEL)
    return mod[:, :bsz], jnp.broadcast_to(mod[:, bsz:bsz + 1], (depth, bsz, 6, D_MODEL))


def _inproj_kernel(x_ref, mod_ref, g_ref, w_ref, s5_ref, hy_ref, ret_ref, mla_ref):
    h = _rms(x_ref[0], g_ref[...]) * (1.0 + mod_ref[0, 1:2, :]) + mod_ref[0, 0:1, :]
    hb = h.astype(BF16)
    s5_ref[0] = _dot(hb, w_ref[:, 0:256]).astype(BF16)
    hy_ref[0] = _dot(hb, w_ref[:, 256:1024]).astype(BF16)
    ret_ref[0] = _dot(hb, w_ref[:, 1024:2048]).astype(BF16)
    mla_ref[0] = _dot(hb, w_ref[:, 2048:2560]).astype(BF16)


def _inproj(x, mod, g, w):
    bsz, n, _ = x.shape
    tm = min(512, n)
    row = lambda width: pl.BlockSpec((1, tm, width), lambda b, i: (b, i, 0))
    shp = lambda width: jax.ShapeDtypeStruct((bsz, n, width), BF16)
    return _call(
        _inproj_kernel, name="inproj", grid=(bsz, n // tm),
        in_specs=[row(D_MODEL), pl.BlockSpec((1, 6, D_MODEL), lambda b, i: (b, 0, 0)),
                  pl.BlockSpec((1, D_MODEL), lambda b, i: (0, 0)),
                  pl.BlockSpec((D_MODEL, N_IN), lambda b, i: (0, 0))],
        out_specs=[row(256), row(768), row(1024), row(512)],
        out_shape=[shp(256), shp(768), shp(1024), shp(512)])(x, mod, g, w)


def _perm_w_in(w):
    z = lambda k: jnp.zeros((D_MODEL, k), w.dtype)
    return jnp.concatenate([w[:, :2048], w[:, 2048:2240], z(64), w[:, 2240:2368], w[:, 2368:2400], z(96)],
                           axis=1).astype(BF16)


def _gelu_tanh(x):
    return 0.5 * x * (1.0 + jnp.tanh(math.sqrt(2.0 / math.pi) * (x + 0.044715 * (x * x * x))))


def _s5_kernel(*refs, rev, tm, final):
    if final:
        (u_ref, bblk_ref, cblk_ref, tab_ref, tri_ref, h0_ref, yprev_ref, d_ref, gw_ref, gb_ref,
         y_ref, hfin_ref, hs_scr, carry_scr) = refs
    else:
        u_ref, bblk_ref, cblk_ref, tab_ref, tri_ref, h0_ref, y_ref, hfin_ref, hs_scr, carry_scr = refs
    nc = S5_COLS

    @pl.when(pl.program_id(1) == 0)
    def _():
        carry_scr[...] = h0_ref[0]

    u = u_ref[0]
    bu = _dot(u, bblk_ref[...])
    pre_r, pre_i, post_r, post_i = tab_ref[0], tab_ref[1], tab_ref[2], tab_ref[3]
    car_r, car_i = tab_ref[4, 0:S5_SUB, :], tab_ref[5, 0:S5_SUB, :]
    cr, ci = carry_scr[:, 0:nc], carry_scr[:, nc:]
    groups = range(tm // S5_GRP)
    subs = range(S5_GRP // S5_SUB)
    for g in (reversed(groups) if rev else groups):
        r0 = g * S5_GRP
        br, bi = bu[r0:r0 + S5_GRP, 0:nc], bu[r0:r0 + S5_GRP, nc:]
        gr = pre_r * br - pre_i * bi
        gi = pre_r * bi + pre_i * br
        cs = _dot(tri_ref[...], jnp.concatenate([gr, gi], axis=1).astype(BF16))
        csr, csi = cs[:, 0:nc], cs[:, nc:]
        wr = post_r * csr - post_i * csi
        wi = post_r * csi + post_i * csr
        for s in (reversed(subs) if rev else subs):
            a0 = s * S5_SUB
            hr = wr[a0:a0 + S5_SUB] + car_r * cr - car_i * ci
            hi = wi[a0:a0 + S5_SUB] + car_r * ci + car_i * cr
            hs_scr[r0 + a0:r0 + a0 + S5_SUB, 0:nc] = hr.astype(BF16)
            hs_scr[r0 + a0:r0 + a0 + S5_SUB, nc:] = hi.astype(BF16)
            edge = 0 if rev else S5_SUB - 1
            cr, ci = hr[edge:edge + 1], hi[edge:edge + 1]
    carry = jnp.concatenate([cr, ci], axis=1)
    carry_scr[...] = carry
    hfin_ref[0] = carry
    y = _dot(hs_scr[...], cblk_ref[...])
    if final:
        yt = _gelu_tanh(yprev_ref[0] + y + d_ref[...] * u.astype(F32))
        z = _dot(yt.astype(BF16), gw_ref[...]) + gb_ref[...]
        y_ref[0] = (yt * jax.nn.sigmoid(z)).astype(y_ref.dtype)
    else:
        y_ref[0] = y


def _s5_dir(u, bblk, cblk, tab, tri, h0, rev, final=None):
    bsz, n, _ = u.shape
    tm = min(256, n)
    nb = n // tm
    blk = (lambda i: nb - 1 - i) if rev else (lambda i: i)
    row = pl.BlockSpec((1, tm, W_GROUP), lambda b, i: (b, blk(i), 0))
    full = lambda a: pl.BlockSpec(a.shape, lambda b, i: (0,) * a.ndim)
    state = pl.BlockSpec((1, 1, 2 * S5_COLS), lambda b, i: (b, 0, 0))
    ins = [u, bblk, cblk, tab, tri, h0]
    in_specs = [row, full(bblk), full(cblk), full(tab), full(tri), state]
    if final is not None:
        ins += list(final)
        in_specs += [row] + [full(a) for a in final[1:]]
    return _call(
        functools.partial(_s5_kernel, rev=rev, tm=tm, final=final is not None),
        name="s5_bwd" if rev else "s5_fwd", grid=(bsz, nb), in_specs=in_specs,
        out_specs=[row, state],
        out_shape=[jax.ShapeDtypeStruct((bsz, n, W_GROUP), BF16 if final is not None else F32),
                   jax.ShapeDtypeStruct((bsz, 1, 2 * S5_COLS), F32)],
        scratch_shapes=[pltpu.VMEM((tm, 2 * S5_COLS), BF16), pltpu.VMEM((1, 2 * S5_COLS), F32)])(*ins)


def _s5_weights(lam_re, lam_im, log_dt, b_re, b_im, c_re, c_im):
    g, p, ch = S5_GROUPS, S5_STATE, S5_CH
    dt = jnp.exp(log_dt)[..., None]
    re_dt = (lam_re * dt).reshape(2, 1, g * p)
    im_dt = (lam_im * dt).reshape(2, 1, g * p)
    lam = lax.complex(lam_re, lam_im)
    abar = jnp.exp(lax.complex(lam_re * dt, lam_im * dt))
    bbar = ((abar - 1.0) / lam)[..., None] * lax.complex(b_re, b_im)
    eye = jnp.eye(g, dtype=F32)
    place_b = lambda a: jnp.einsum('dgpc,gh->dgchp', a, eye).reshape(2, g * ch, g * p)
    bblk = jnp.concatenate([place_b(jnp.real(bbar)), place_b(jnp.imag(bbar))], axis=2).astype(BF16)
    place_c = lambda a: jnp.einsum('dgcp,gh->dgphc', a, eye).reshape(2, g * p, g * ch)
    cblk = jnp.concatenate([place_c(c_re), place_c(-c_im)], axis=1).astype(BF16)

    def powers(k, d):
        kk = k[:, None]
        mag = jnp.exp(kk * re_dt[d])
        return [mag * jnp.cos(kk * im_dt[d]), mag * jnp.sin(kk * im_dt[d])]

    i_sub = jnp.tile(jnp.arange(S5_SUB, dtype=F32), S5_GRP // S5_SUB)
    tab_f = jnp.stack(powers(-i_sub, 0) + powers(i_sub, 0) + powers(i_sub + 1.0, 0))
    tab_b = jnp.stack(powers(i_sub, 1) + powers(-i_sub, 1) + powers(S5_SUB - i_sub, 1))
    ii = np.arange(S5_GRP)
    same = (ii[:, None] // S5_SUB) == (ii[None, :] // S5_SUB)
    tri_f = jnp.asarray(same & (ii[None, :] <= ii[:, None]), F32).astype(BF16)
    tri_b = jnp.asarray(same & (ii[None, :] >= ii[:, None]), F32).astype(BF16)
    return (bblk[0], cblk[0], tab_f, tri_f), (bblk[1], cblk[1], tab_b, tri_b)


def _s5_mixer(u_lat, u_ctx, prm, need_ctx):
    (lam_re, lam_im, log_dt, b_re, b_im, c_re, c_im, d_skip, glu_w, glu_b) = prm
    wf, wb = _s5_weights(lam_re, lam_im, log_dt, b_re, b_im, c_re, c_im)
    bsz = u_lat.shape[0]
    zero = jnp.zeros((bsz, 1, 2 * S5_COLS), F32)
    fin = lambda yprev: (yprev, d_skip.reshape(1, W_GROUP), glu_w.astype(BF16), glu_b.reshape(1, W_GROUP))
    yc_f, hc_f = _s5_dir(u_ctx, *wf, zero, rev=False)
    yc, hc_b = _s5_dir(u_ctx, *wb, zero, rev=True, final=fin(yc_f))
    yl_f, _ = _s5_dir(u_lat, *wf, hc_f, rev=False)
    yl, _ = _s5_dir(u_lat, *wb, hc_b, rev=True, final=fin(yl_f))
    return yl, (yc if need_ctx else None)


def _rot_half(x, half, period):
    lane = lax.broadcasted_iota(jnp.int32, x.shape, 1)
    width = x.shape[1]
    return jnp.where((lane % period) < half, pltpu.roll(x, width - half, axis=1), pltpu.roll(x, half, axis=1))


def _ret_kernel(*refs, rev, tm, rotate, final):
    refs = list(refs)
    q_ref, k_ref, v_ref = refs[:3]
    del refs[:3]
    if rotate:
        cos_ref, sin_ref = refs[:2]
        del refs[:2]
    lgt_ref, lgc_ref, s0_ref = refs[:3]
    del refs[:3]
    if final:
        of_ref, g_ref, gn_ref, avg_ref = refs[:4]
        del refs[:4]
    o_ref, sfin_ref, d_scr, xz_scr, s_scr = refs
    hd = RET_HEADS * RET_DK

    @pl.when((pl.program_id(0) == 0) & (pl.program_id(1) == 0))
    def _():
        ri = lax.broadcasted_iota(jnp.int32, (tm, tm), 0)
        ci = lax.broadcasted_iota(jnp.int32, (tm, tm), 1)
        diff = (ci - ri) if rev else (ri - ci)
        dpos = jnp.maximum(diff, 0).astype(F32)
        for h in range(RET_HEADS):
            d_scr[h] = jnp.where(diff >= 0, jnp.exp(dpos * lgt_ref[h]), 0.0)
        pos = lax.broadcasted_iota(jnp.int32, (tm, hd), 0).astype(F32)
        lgc = lgc_ref[...]
        xz_scr[0] = jnp.exp(((tm - pos) if rev else (pos + 1.0)) * lgc)
        xz_scr[1] = jnp.exp((pos if rev else (tm - 1.0 - pos)) * lgc)

    @pl.when(pl.program_id(1) == 0)
    def _():
        s_scr[...] = s0_ref[0]

    q = q_ref[0].astype(F32)
    k = k_ref[0].astype(F32) * (RET_DK ** -0.5)
    if rotate:
        cos, sin = cos_ref[...], sin_ref[...]
        q = q * cos + _rot_half(q, RET_DK // 2, RET_DK) * sin
        k = k * cos + _rot_half(k, RET_DK // 2, RET_DK) * sin
    qb, kb, v = q.astype(BF16), k.astype(BF16), v_ref[0]
    head = lax.broadcasted_iota(jnp.int32, (tm, hd), 1) // RET_DK
    o = _dot((q * xz_scr[0]).astype(BF16), s_scr[...].astype(BF16))
    for h in range(RET_HEADS):
        m = head == h
        s = _dot_nt(jnp.where(m, qb, jnp.zeros_like(qb)), kb)
        oh = _dot((s * d_scr[h]).astype(BF16), v)
        o = o + jnp.where(m, oh, 0.0)
    ds = _dot_tn(kb, (v.astype(F32) * xz_scr[1]).astype(BF16))
    rh = lax.broadcasted_iota(jnp.int32, (hd, hd), 0) // RET_DK
    ch = lax.broadcasted_iota(jnp.int32, (hd, hd), 1) // RET_DK
    s_new = s_scr[...] * jnp.exp(tm * lgc_ref[...]) + jnp.where(rh == ch, ds, 0.0)
    s_scr[...] = s_new
    sfin_ref[0] = s_new
    if final:
        o = o + of_ref[0]
        o2 = o * o
        hi = o2.astype(BF16)
        lo = (o2 - hi.astype(F32)).astype(BF16)
        ms = _dot(hi, avg_ref[...]) + _dot(lo, avg_ref[...])
        g = g_ref[0].astype(F32)
        o_ref[0] = (_silu(g) * (o * lax.rsqrt(ms + EPS) * gn_ref[...])).astype(o_ref.dtype)
    else:
        o_ref[0] = o


def _ret_dir(p, lg, s0, rev, rope=None, final=None):
    bsz, n, _ = p.shape
    tm = min(256, n)
    nb = n // tm
    hd = RET_HEADS * RET_DK
    blk = (lambda i: nb - 1 - i) if rev else (lambda i: i)
    col = lambda j: pl.BlockSpec((1, tm, hd), lambda b, i: (b, blk(i), j))
    full = lambda a: pl.BlockSpec(a.shape, lambda b, i: (0,) * a.ndim)
    state = pl.BlockSpec((1, hd, hd), lambda b, i: (b, 0, 0))
    lgt = jnp.broadcast_to(lg[:, None, None], (RET_HEADS, 1, tm))
    lgc = jnp.repeat(lg, RET_DK).reshape(1, hd)
    ins, in_specs = [p, p, p], [col(0), col(1), col(2)]
    if rope is not None:
        ins += list(rope)
        in_specs += [pl.BlockSpec((tm, hd), lambda b, i: (blk(i), 0))] * 2
    ins += [lgt, lgc, s0]
    in_specs += [full(lgt), full(lgc), state]
    if final is not None:
        of, gn = final
        avg = jnp.asarray(np.kron(np.eye(RET_HEADS), np.full((RET_DK, RET_DK), 1.0 / RET_DK)), F32).astype(BF16)
        ins += [of, p, gn.reshape(1, hd), avg]
        in_specs += [pl.BlockSpec((1, tm, hd), lambda b, i: (b, blk(i), 0)), col(3), full(gn.reshape(1, hd)), full(avg)]
    return _call(
        functools.partial(_ret_kernel, rev=rev, tm=tm, rotate=rope is not None, final=final is not None),
        name="ret_bwd" if rev else "ret_fwd", grid=(bsz, nb), in_specs=in_specs,
        out_specs=[pl.BlockSpec((1, tm, hd), lambda b, i: (b, blk(i), 0)), state],
        out_shape=[jax.ShapeDtypeStruct((bsz, n, hd), BF16 if final is not None else F32),
                   jax.ShapeDtypeStruct((bsz, hd, hd), F32)],
        scratch_shapes=[pltpu.VMEM((RET_HEADS, tm, tm), F32), pltpu.VMEM((2, tm, hd), F32),
                        pltpu.VMEM((hd, hd), F32)])(*ins)


def _ret_mixer(p_lat, p_ctx, prm, need_ctx):
    decay_exp, gn_g = prm
    lg = jnp.log1p(-jnp.exp2(-decay_exp))
    bsz, n, _ = p_lat.shape
    hd = RET_HEADS * RET_DK
    theta = RET_ROPE_BASE ** (-jnp.linspace(0.0, 1.0, RET_DK // 2, dtype=F32))
    ang = jnp.arange(n, dtype=F32)[:, None] * theta
    cos = jnp.tile(jnp.cos(ang), (1, 2 * RET_HEADS))
    sin = jnp.tile(jnp.concatenate([-jnp.sin(ang), jnp.sin(ang)], axis=1), (1, RET_HEADS))
    zero = jnp.zeros((bsz, hd, hd), F32)
    oc_f, sc_f = _ret_dir(p_ctx, lg[0], zero, rev=False)
    yc, sc_b = _ret_dir(p_ctx, lg[1], zero, rev=True, final=(oc_f, gn_g))
    ol_f, _ = _ret_dir(p_lat, lg[0], sc_f, rev=False, rope=(cos, sin))
    yl, _ = _ret_dir(p_lat, lg[1], sc_b, rev=True, rope=(cos, sin), final=(ol_f, gn_g))
    return yl, (yc if need_ctx else None)


def _mla_prep_kernel(*refs, rotate):
    if rotate:
        p_ref, gq_ref, gkv_ref, wq_ref, wkv_ref, e_ref, cos_ref, sin_ref, q_ref, k_ref, v_ref = refs
    else:
        p_ref, gq_ref, gkv_ref, wq_ref, wkv_ref, e_ref, q_ref, k_ref, v_ref = refs
    p = p_ref[0]
    cq = p[:, 0:256].astype(F32)
    cqn = cq * lax.rsqrt(jnp.sum(cq * cq, axis=-1, keepdims=True) * (1.0 / MLA_Q_RANK) + EPS) * gq_ref[...]
    q = _dot(cqn.astype(BF16), wq_ref[...])
    ckv = p[:, 256:384].astype(F32)
    kv = _dot(_rms(ckv, gkv_ref[...]).astype(BF16), wkv_ref[...])
    width = MLA_HEADS * MLA_HEAD_PAD
    k = kv[:, 0:width] + _dot(p[:, 384:512], e_ref[...])
    if rotate:
        cos, sin = cos_ref[...], sin_ref[...]
        lane = lax.broadcasted_iota(jnp.int32, q.shape, 1) % MLA_HEAD_PAD
        first = lane < MLA_NOPE + MLA_ROPE // 2
        rot = lambda x: jnp.where(first, pltpu.roll(x, width - MLA_ROPE // 2, axis=1),
                                  pltpu.roll(x, MLA_ROPE // 2, axis=1))
        q = q * cos + rot(q) * sin
        k = k * cos + rot(k) * sin
    q_ref[0] = (q * (MLA_QK ** -0.5)).astype(BF16)
    k_ref[0] = k.astype(BF16)
    v_ref[0] = kv[:, width:].astype(BF16)


def _mla_prep(p, wts, rope=None):
    bsz, n, _ = p.shape
    tm = min(512, n)
    width = MLA_HEADS * MLA_HEAD_PAD
    full = lambda a: pl.BlockSpec(a.shape, lambda b, i: (0,) * a.ndim)
    row = lambda w: pl.BlockSpec((1, tm, w), lambda b, i: (b, i, 0))
    ins = [p] + list(wts)
    in_specs = [row(512)] + [full(a) for a in wts]
    if rope is not None:
        ins += list(rope)
        in_specs += [pl.BlockSpec((tm, width), lambda b, i: (i, 0))] * 2
    return _call(
        functools.partial(_mla_prep_kernel, rotate=rope is not None), name="mla_prep", grid=(bsz, n // tm),
        in_specs=in_specs, out_specs=[row(width), row(width), row(MLA_HEADS * MLA_V)],
        out_shape=[jax.ShapeDtypeStruct((bsz, n, width), BF16), jax.ShapeDtypeStruct((bsz, n, width), BF16),
                   jax.ShapeDtypeStruct((bsz, n, MLA_HEADS * MLA_V), BF16)])(*ins)


def _attn_kernel(q_ref, k_ref, v_ref, o_ref):
    q = q_ref[0]
    v = v_ref[0]
    head = lax.broadcasted_iota(jnp.int32, (q.shape[0], MLA_HEADS * MLA_V), 1) // MLA_V
    o = jnp.zeros((q.shape[0], MLA_HEADS * MLA_V), F32)
    for h in range(MLA_HEADS):
        c0 = h * MLA_HEAD_PAD
        s = _dot_nt(q[:, c0:c0 + MLA_HEAD_PAD], k_ref[0, :, c0:c0 + MLA_HEAD_PAD])
        e = jnp.exp(s - jnp.max(s, axis=-1, keepdims=True))
        oh = _dot(e.astype(BF16), v) / jnp.sum(e, axis=-1, keepdims=True)
        o = o + jnp.where(head == h, oh, 0.0)
    o_ref[0] = o.astype(o_ref.dtype)


def _attention(q, k, v):
    bsz, n, width = q.shape
    nk = k.shape[1]
    tq = min(256, n)
    return _call(
        _attn_kernel, name="mla_attn", grid=(bsz, n // tq),
        in_specs=[pl.BlockSpec((1, tq, width), lambda b, i: (b, i, 0)),
                  pl.BlockSpec((1, nk, width), lambda b, i: (b, 0, 0)),
                  pl.BlockSpec((1, nk, MLA_HEADS * MLA_V), lambda b, i: (b, 0, 0))],
        out_specs=pl.BlockSpec((1, tq, MLA_HEADS * MLA_V), lambda b, i: (b, i, 0)),
        out_shape=jax.ShapeDtypeStruct((bsz, n, MLA_HEADS * MLA_V), BF16))(q, k, v)


def _mla_mixer(p_lat, p_ctx, prm, need_ctx):
    q_norm_g, kv_norm_g, w_uq, w_ukv = prm
    n = p_lat.shape[1]
    width = MLA_HEADS * MLA_HEAD_PAD
    wq = jnp.pad(w_uq.reshape(MLA_Q_RANK, MLA_HEADS, MLA_QK),
                 ((0, 256 - MLA_Q_RANK), (0, 0), (0, MLA_HEAD_PAD - MLA_QK))).reshape(256, width).astype(BF16)
    wkv = w_ukv.reshape(MLA_KV_RANK, MLA_HEADS, MLA_NOPE + MLA_V)
    wk = jnp.pad(wkv[:, :, :MLA_NOPE], ((0, 0), (0, 0), (0, MLA_HEAD_PAD - MLA_NOPE))).reshape(MLA_KV_RANK, width)
    wv = wkv[:, :, MLA_NOPE:].reshape(MLA_KV_RANK, MLA_HEADS * MLA_V)
    wkv = jnp.concatenate([wk, wv], axis=1).astype(BF16)
    place = np.zeros((128, width), np.float32)
    for h in range(MLA_HEADS):
        for j in range(MLA_ROPE):
            place[j, h * MLA_HEAD_PAD + MLA_NOPE + j] = 1.0
    place = jnp.asarray(place).astype(BF16)
    gq = jnp.pad(q_norm_g, (0, 256 - MLA_Q_RANK)).reshape(1, 256)
    wts = (gq, kv_norm_g.reshape(1, MLA_KV_RANK), wq, wkv, place)
    pos = jnp.arange(n)
    row, colp = (pos // GRID_W).astype(F32), (pos % GRID_W).astype(F32)
    n_freq = MLA_ROPE // 4
    inv = ROPE_BASE ** (-jnp.arange(n_freq, dtype=F32) / n_freq)
    ang = jnp.concatenate([row[:, None] * inv, colp[:, None] * inv], axis=-1)
    ones, zeros = jnp.ones((n, MLA_NOPE), F32), jnp.zeros((n, MLA_NOPE), F32)
    tail = MLA_HEAD_PAD - MLA_QK
    cos = jnp.concatenate([ones, jnp.cos(ang), jnp.cos(ang), jnp.ones((n, tail), F32)], axis=1)
    sin = jnp.concatenate([zeros, -jnp.sin(ang), jnp.sin(ang), jnp.zeros((n, tail), F32)], axis=1)
    rope = (jnp.tile(cos, (1, MLA_HEADS)), jnp.tile(sin, (1, MLA_HEADS)))
    q_c, k_c, v_c = _mla_prep(p_ctx, wts)
    q_l, k_l, v_l = _mla_prep(p_lat, wts, rope)
    y_lat = _attention(q_l, jnp.concatenate([k_c, k_l], axis=1), jnp.concatenate([v_c, v_l], axis=1))
    y_ctx = _attention(q_c, k_c, v_c) if need_ctx else None
    return y_lat, y_ctx


def _hy_filter_kernel(z_ref, w1_ref, b1_ref, w2_ref, b2_ref, w3_ref, fr_ref, dl_ref, k_ref):
    z = z_ref[...]
    fr = fr_ref[...]
    dotp = lambda a, b: jnp.dot(a, b, preferred_element_type=F32, precision=HIGHEST)
    hid = jnp.sin(fr * (dotp(z, w1_ref[...]) + b1_ref[...]))
    hid = jnp.sin(fr * (dotp(hid, w2_ref[...]) + b2_ref[...]))
    k = dotp(hid, w3_ref[...]) * jnp.exp(-z[:, 0:1] * jnp.abs(dl_ref[...]))
    ss = jnp.sum(k * k, axis=0, keepdims=True)
    r = lax.rsqrt(ss[:, 0:W_GROUP] + ss[:, W_GROUP:] + EPS)
    k_ref[...] = (k * jnp.concatenate([r, r], axis=1)).astype(k_ref.dtype)


def _hy_filter(n, prm):
    w1, b1, w2, b2, w3, freq, deltas = prm
    pos = jnp.arange(n, dtype=F32)
    t01 = pos / (n - 1)
    bands = jnp.linspace(1e-4, HY_BANDS - 1, HY_BANDS, dtype=F32)
    ang = (2.0 * math.pi / n) * pos[:, None] * bands[None, :]
    z = jnp.concatenate([t01[:, None], jnp.cos(ang), -jnp.sin(ang)], axis=-1)
    emb = z.shape[1]
    z = jnp.pad(z, ((0, 0), (0, 128 - emb)))
    w1p = jnp.pad(w1, ((0, 128 - emb), (0, 0)))
    ins = [z, w1p, b1.reshape(1, -1), w2, b2.reshape(1, -1), w3, freq.reshape(1, -1), deltas.reshape(1, -1)]
    full = lambda a: pl.BlockSpec(a.shape, lambda o: (0,) * a.ndim)
    ncol = w3.shape[1]
    cw = 2 * W_GROUP
    in_specs = [full(a) for a in ins]
    in_specs[5] = pl.BlockSpec((HY_HIDDEN, cw), lambda o: (0, o))
    in_specs[7] = pl.BlockSpec((1, cw), lambda o: (0, o))
    return _call(_hy_filter_kernel, name="hy_filter", grid=(ncol // cw,), in_specs=in_specs,
                 out_specs=pl.BlockSpec((n, cw), lambda o: (0, o)),
                 out_shape=jax.ShapeDtypeStruct((n, ncol), BF16))(*ins)


def _dwconv_rows(x, w_ref, b_ref):
    n = x.shape[0]
    r = lax.broadcasted_iota(jnp.int32, x.shape, 0)
    prev = jnp.where(r == 0, 0.0, pltpu.roll(x, 1, axis=0))
    nxt = jnp.where(r == n - 1, 0.0, pltpu.roll(x, n - 1, axis=0))
    return prev * w_ref[0:1, :] + x * w_ref[1:2, :] + nxt * w_ref[2:3, :] + b_ref[...]


def _hy_pre_kernel(p_ref, w_ref, b_ref, o_ref):
    o_ref[0] = _dwconv_rows(p_ref[0].astype(F32), w_ref, b_ref).astype(o_ref.dtype)


def _hy_pre(p, conv_w, conv_b):
    bsz, n, ch = p.shape
    cb = 256
    return _call(
        _hy_pre_kernel, name="hy_dwconv", grid=(bsz, ch // cb),
        in_specs=[pl.BlockSpec((1, n, cb), lambda b, j: (b, 0, j)), pl.BlockSpec((3, cb), lambda b, j: (0, j)),
                  pl.BlockSpec((1, cb), lambda b, j: (0, j))],
        out_specs=pl.BlockSpec((1, n, cb), lambda b, j: (b, 0, j)),
        out_shape=jax.ShapeDtypeStruct((bsz, n, ch), BF16))(p, conv_w, conv_b.reshape(1, ch))


def _fft_consts(n_tok):
    n = 2 * n_tok
    n2n = FFT_N2
    n1n = n // n2n
    k1 = np.arange(n1n)[None, :, None]
    n1 = np.arange(n1n // 2)[None, None, :]
    n2 = np.arange(n2n)[:, None, None]
    th = 2.0 * np.pi * ((k1 * (n2n * n1 + n2)) % n) / n
    fa = np.concatenate([np.cos(th), -np.sin(th)], axis=1)
    fat = np.transpose(fa, (0, 2, 1)) / n
    a = np.arange(n2n)
    t2 = 2.0 * np.pi * ((a[:, None] * a[None, :]) % n2n) / n2n
    fr, fi = np.cos(t2), -np.sin(t2)
    f2 = np.block([[fr, -fi], [fi, fr]])
    f2i = np.block([[fr, fi], [-fi, fr]])
    as_bf16 = lambda m: jnp.asarray(m, F32).astype(BF16)
    return as_bf16(fa), as_bf16(fat), as_bf16(f2), as_bf16(f2i)


def _fft_a_kernel(x_ref, fa_ref, o_ref, *, nb, cw):
    for j in range(nb):
        o_ref[0, :, j * cw:(j + 1) * cw] = _dot(fa_ref[j], x_ref[0, :, j * cw:(j + 1) * cw]).astype(o_ref.dtype)


def _fft_a(x, fa, ctot):
    bsz, h1, _ = x.shape
    cw, nb = 256, 8
    ncb = ctot // cw
    if ncb == 1:
        blkw = nb * cw
        grid = (bsz, FFT_N2 // nb)
        xs = pl.BlockSpec((1, h1, blkw), lambda b, j: (b, 0, j))
        fs = pl.BlockSpec((nb, 4 * h1, h1), lambda b, j: (j, 0, 0))
        os = pl.BlockSpec((1, 4 * h1, blkw), lambda b, j: (b, 0, j))
        kern = functools.partial(_fft_a_kernel, nb=nb, cw=cw)
    else:
        grid = (bsz, FFT_N2 * ncb)
        xs = pl.BlockSpec((1, h1, cw), lambda b, j: (b, 0, j))
        fs = pl.BlockSpec((1, 4 * h1, h1), lambda b, j: (j // ncb, 0, 0))
        os = pl.BlockSpec((1, 4 * h1, cw), lambda b, j: (b, 0, j))
        kern = functools.partial(_fft_a_kernel, nb=1, cw=cw)
    return _call(kern, name="hy_fft_a", grid=grid, in_specs=[xs, fs], out_specs=os,
                 out_shape=jax.ShapeDtypeStruct((bsz, 4 * h1, FFT_N2 * ctot), BF16))(x, fa)


def _fft_spec_kernel(a_ref, k0_ref, f2_ref, h_ref, *, kb):
    c = W_GROUP
    for j in range(kb):
        for o in range(HY_ORDER):
            af = jnp.concatenate([a_ref[0, 0, j, :, 2 * o * c:(2 * o + 1) * c],
                                  a_ref[0, 1, j, :, 2 * o * c:(2 * o + 1) * c]], axis=0)
            ab = jnp.concatenate([a_ref[0, 0, j, :, (2 * o + 1) * c:(2 * o + 2) * c],
                                  a_ref[0, 1, j, :, (2 * o + 1) * c:(2 * o + 2) * c]], axis=0)
            xf, xb = _dot(f2_ref[...], af), _dot(f2_ref[...], ab)
            k0 = k0_ref[:, (2 * o + 1) * c:(2 * o + 2) * c].astype(F32)
            h_ref[o, j, 0:FFT_N2, :] = xf[0:FFT_N2] + xb[0:FFT_N2] - k0
            h_ref[o, j, FFT_N2:, :] = xf[FFT_N2:] - xb[FFT_N2:]


def _fft_spec(a5, k0, f2):
    n1n = a5.shape[2]
    kb = 8
    ctot = a5.shape[4]
    return _call(
        functools.partial(_fft_spec_kernel, kb=kb), name="hy_fft_spec", grid=(n1n // kb,),
        in_specs=[pl.BlockSpec((1, 2, kb, FFT_N2, ctot), lambda i: (0, 0, i, 0, 0)),
                  pl.BlockSpec((1, ctot), lambda i: (0, 0)), pl.BlockSpec(f2.shape, lambda i: (0, 0))],
        out_specs=pl.BlockSpec((HY_ORDER, kb, 2 * FFT_N2, W_GROUP), lambda i: (0, i, 0, 0)),
        out_shape=jax.ShapeDtypeStruct((HY_ORDER, n1n, 2 * FFT_N2, W_GROUP), F32))(a5, k0, f2)


def _fft_c_kernel(a_ref, h_ref, f2_ref, f2i_ref, o_ref, *, kb):
    for j in range(kb):
        a = jnp.concatenate([a_ref[0, 0, j], a_ref[0, 1, j]], axis=0)
        x = _dot(f2_ref[...], a)
        xr, xi = x[0:FFT_N2], x[FFT_N2:]
        hr, hi = h_ref[0, j, 0:FFT_N2, :], h_ref[0, j, FFT_N2:, :]
        y = jnp.concatenate([xr * hr - xi * hi, xr * hi + xi * hr], axis=0).astype(BF16)
        z = _dot(f2i_ref[...], y)
        o_ref[0, 0, j] = z[0:FFT_N2].astype(o_ref.dtype)
        o_ref[0, 1, j] = z[FFT_N2:].astype(o_ref.dtype)


def _fft_c(a5, spec, order, f2, f2i):
    bsz, _, n1n, _, c = a5.shape
    kb = 8
    return _call(
        functools.partial(_fft_c_kernel, kb=kb), name="hy_fft_c", grid=(bsz, n1n // kb),
        in_specs=[pl.BlockSpec((1, 2, kb, FFT_N2, c), lambda b, i: (b, 0, i, 0, 0)),
                  pl.BlockSpec((1, kb, 2 * FFT_N2, c), lambda b, i: (order, i, 0, 0)),
                  pl.BlockSpec(f2.shape, lambda b, i: (0, 0)), pl.BlockSpec(f2i.shape, lambda b, i: (0, 0))],
        out_specs=pl.BlockSpec((1, 2, kb, FFT_N2, c), lambda b, i: (b, 0, i, 0, 0)),
        out_shape=jax.ShapeDtypeStruct(a5.shape, BF16))(a5, spec, f2, f2i)


def _fft_ainv_kernel(z_ref, fat_ref, x_ref, v_ref, b_ref, o_ref, *, nb, cw):
    for j in range(nb):
        sl = slice(j * cw, (j + 1) * cw)
        y = _dot(fat_ref[j], z_ref[0, :, sl])
        v = v_ref[0, :, sl].astype(F32)
        o_ref[0, :, sl] = (x_ref[0, :, sl].astype(F32) * (y + v * b_ref[...])).astype(o_ref.dtype)


def _fft_ainv(z, fat, gate, v, bias):
    bsz, r4, _ = z.shape
    h1 = r4 // 4
    cw, nb = 256, 8
    dat = pl.BlockSpec((1, h1, nb * cw), lambda b, j: (b, 0, j))
    return _call(
        functools.partial(_fft_ainv_kernel, nb=nb, cw=cw), name="hy_fft_ainv", grid=(bsz, FFT_N2 // nb),
        in_specs=[pl.BlockSpec((1, r4, nb * cw), lambda b, j: (b, 0, j)),
                  pl.BlockSpec((nb, h1, r4), lambda b, j: (j, 0, 0)), dat, dat,
                  pl.BlockSpec((1, cw), lambda b, j: (0, 0))],
        out_specs=dat, out_shape=jax.ShapeDtypeStruct((bsz, h1, FFT_N2 * cw), BF16))(z, fat, gate, v, bias)


def _hyena_lat(p, prm):
    conv_w, conv_b, w1, b1, w2, b2, w3, freq, deltas, bias = prm
    bsz, n, _ = p.shape
    c = W_GROUP
    h1 = n // FFT_N2
    fa, fat, f2, f2i = _fft_consts(n)
    kf = _hy_filter(n, (w1, b1, w2, b2, w3, freq, deltas))
    ak = _fft_a(kf.reshape(1, h1, FFT_N2 * 4 * c), fa, 4 * c)
    spec = _fft_spec(ak.reshape(1, 2, 2 * h1, FFT_N2, 4 * c), kf[0:1], f2)
    u = _hy_pre(p, conv_w, conv_b)
    perm = lambda a: a.reshape(bsz, h1, FFT_N2 * c)
    x1, x2, v = perm(u[..., 0:c]), perm(u[..., c:2 * c]), perm(u[..., 2 * c:])
    z = v
    for o, gate in enumerate((x1, x2)):
        a = _fft_a(z, fa, c).reshape(bsz, 2, 2 * h1, FFT_N2, c)
        zc = _fft_c(a, spec, o, f2, f2i).reshape(bsz, 4 * h1, FFT_N2 * c)
        z = _fft_ainv(zc, fat, gate, z, bias[o].reshape(1, c))
    return z.reshape(bsz, n, c)


def _hy_ctx_kernel(p_ref, cw_ref, cb_ref, k_ref, fd_ref, fdi_ref, bias_ref, o_ref):
    c = W_GROUP
    u = _dwconv_rows(p_ref[0].astype(F32), cw_ref, cb_ref)
    nf = fd_ref.shape[0] // 2
    xk = _dot(fd_ref[...], k_ref[...])
    z = u[:, 2 * c:]
    for o, gate in enumerate((u[:, 0:c], u[:, c:2 * c])):
        kf, kb = xk[:, 2 * o * c:(2 * o + 1) * c], xk[:, (2 * o + 1) * c:(2 * o + 2) * c]
        k0 = k_ref[0:1, (2 * o + 1) * c:(2 * o + 2) * c].astype(F32)
        hr = kf[0:nf] + kb[0:nf] - k0
        hi = kf[nf:] - kb[nf:]
        x = _dot(fd_ref[...], z.astype(BF16))
        xr, xi = x[0:nf], x[nf:]
        y = jnp.concatenate([xr * hr - xi * hi, xr * hi + xi * hr], axis=0).astype(BF16)
        z = gate * (_dot(fdi_ref[...], y) + z * bias_ref[o:o + 1, :])
    o_ref[0] = z.astype(o_ref.dtype)


def _hyena_ctx(p, prm):
    conv_w, conv_b, w1, b1, w2, b2, w3, freq, deltas, bias = prm
    bsz, n, ch = p.shape
    kf = _hy_filter(n, (w1, b1, w2, b2, w3, freq, deltas))
    nn = 2 * n
    th = 2.0 * np.pi * ((np.arange(nn)[:, None] * np.arange(n)[None, :]) % nn) / nn
    fd = np.concatenate([np.cos(th), -np.sin(th)], axis=0)
    fdj = jnp.asarray(fd, F32).astype(BF16)
    fdi = jnp.asarray(fd.T / nn, F32).astype(BF16)
    ins = [p, conv_w, conv_b.reshape(1, ch), kf, fdj, fdi, bias]
    full = lambda a: pl.BlockSpec(a.shape, lambda b: (0,) * a.ndim)
    return _call(
        _hy_ctx_kernel, name="hy_ctx", grid=(bsz,),
        in_specs=[pl.BlockSpec((1, n, ch), lambda b: (b, 0, 0))] + [full(a) for a in ins[1:]],
        out_specs=pl.BlockSpec((1, n, W_GROUP), lambda b: (b, 0, 0)),
        out_shape=jax.ShapeDtypeStruct((bsz, n, W_GROUP), BF16))(*ins)


def _outproj_kernel(y0_ref, y1_ref, y2_ref, y3_ref, w_ref, x_ref, mod_ref, g_ref, o_ref):
    c = W_GROUP
    y = (_dot(y0_ref[0], w_ref[0:c, :]) + _dot(y1_ref[0], w_ref[c:2 * c, :])
         + _dot(y2_ref[0], w_ref[2 * c:3 * c, :]) + _dot(y3_ref[0], w_ref[3 * c:, :]))
    o_ref[0] = x_ref[0] + mod_ref[0, 2:3, :] * _rms(y, g_ref[...])


def _outproj(ys, w, x, mod, g):
    bsz, n, _ = x.shape
    tm = min(512, n)
    row = lambda width: pl.BlockSpec((1, tm, width), lambda b, i: (b, i, 0))
    return _call(
        _outproj_kernel, name="outproj", grid=(bsz, n // tm),
        in_specs=[row(W_GROUP)] * 4 + [pl.BlockSpec(w.shape, lambda b, i: (0, 0)), row(D_MODEL),
                                       pl.BlockSpec((1, 6, D_MODEL), lambda b, i: (b, 0, 0)),
                                       pl.BlockSpec((1, D_MODEL), lambda b, i: (0, 0))],
        out_specs=row(D_MODEL), out_shape=jax.ShapeDtypeStruct(x.shape, F32))(*ys, w, x, mod, g)


def _ffn_kernel(xp_ref, x_ref, xn_ref, mod_ref, g2_ref, g3_ref, wup_ref, cw_ref, cb_ref, wdn_ref, o_ref,
                h_scr, u_scr, acc_scr, *, tm, nchunk):
    i, nb = pl.program_id(1), pl.num_programs(1)
    shift, scale = mod_ref[0, 3:4, :], mod_ref[0, 4:5, :]
    pre = lambda x: _rms(x, g2_ref[...]) * (1.0 + scale) + shift
    h_scr[0:8, :] = (pre(xp_ref[0]) * (i > 0).astype(F32)).astype(BF16)
    h_scr[8:8 + tm, :] = pre(x_ref[0]).astype(BF16)
    h_scr[8 + tm:, :] = (pre(xn_ref[0]) * (i < nb - 1).astype(F32)).astype(BF16)
    acc_scr[...] = jnp.zeros_like(acc_scr)

    def body(c, carry):
        u_scr[...] = _dot(h_scr[...], wup_ref[c])
        cw = cw_ref[c]
        u = (u_scr[7:7 + tm, :] * cw[0:1, :] + u_scr[8:8 + tm, :] * cw[1:2, :]
             + u_scr[9:9 + tm, :] * cw[2:3, :] + cb_ref[c])
        act = (_silu(u[:, 0:FF_CHUNK]) * u[:, FF_CHUNK:]).astype(BF16)
        acc_scr[...] += _dot(act, wdn_ref[c])
        return carry

    lax.fori_loop(0, nchunk, body, 0)
    o_ref[0] = x_ref[0] + mod_ref[0, 5:6, :] * _rms(acc_scr[...], g3_ref[...])


def _ffn(x, mod, g2, g3, w_up, conv_w, conv_b, w_down):
    bsz, n, _ = x.shape
    tm = min(256, n)
    nb = n // tm
    nchunk = D_FF // FF_CHUNK
    inter = lambda a: jnp.concatenate([a[..., :D_FF].reshape(a.shape[:-1] + (nchunk, FF_CHUNK)),
                                       a[..., D_FF:].reshape(a.shape[:-1] + (nchunk, FF_CHUNK))], axis=-1)
    wup = jnp.transpose(inter(w_up), (1, 0, 2)).astype(BF16)
    cw = jnp.transpose(inter(conv_w), (1, 0, 2))
    cb = inter(conv_b).reshape(nchunk, 1, 2 * FF_CHUNK)
    wdn = w_down.reshape(nchunk, FF_CHUNK, D_MODEL).astype(BF16)
    r8 = tm // 8
    last8 = n // 8 - 1
    full = lambda a: pl.BlockSpec(a.shape, lambda b, i: (0,) * a.ndim)
    vec = pl.BlockSpec((1, D_MODEL), lambda b, i: (0, 0))
    return _call(
        functools.partial(_ffn_kernel, tm=tm, nchunk=nchunk), name="convffn", grid=(bsz, nb),
        in_specs=[pl.BlockSpec((1, 8, D_MODEL), lambda b, i: (b, jnp.maximum(i * r8 - 1, 0), 0)),
                  pl.BlockSpec((1, tm, D_MODEL), lambda b, i: (b, i, 0)),
                  pl.BlockSpec((1, 8, D_MODEL), lambda b, i: (b, jnp.minimum((i + 1) * r8, last8), 0)),
                  pl.BlockSpec((1, 6, D_MODEL), lambda b, i: (b, 0, 0)), vec, vec,
                  full(wup), full(cw), full(cb), full(wdn)],
        out_specs=pl.BlockSpec((1, tm, D_MODEL), lambda b, i: (b, i, 0)),
        out_shape=jax.ShapeDtypeStruct(x.shape, F32),
        scratch_shapes=[pltpu.VMEM((tm + 16, D_MODEL), BF16), pltpu.VMEM((tm + 16, 2 * FF_CHUNK), F32),
                        pltpu.VMEM((tm, D_MODEL), F32)])(x, x, x, mod, g2, g3, wup, cw, cb, wdn)


def kernel(x, c, ctx, c_ctx, ada_w, ada_b, norm_g, w_in, w_out, s5_lam_re, s5_lam_im, s5_log_dt, s5_b_re, s5_b_im, s5_c_re, s5_c_im, s5_d, s5_glu_w, s5_glu_b, hy_conv_w, hy_conv_b, hy_w1, hy_b1, hy_w2, hy_b2, hy_w3, hy_freq, hy_deltas, hy_bias, ret_decay_exp, ret_gn_g, mla_q_norm_g, mla_kv_norm_g, mla_w_uq, mla_w_ukv, ffn_w_up, ffn_conv_w, ffn_conv_b, ffn_w_down):
    depth = ada_w.shape[0]
    mod_lat, mod_ctx = _adaln(c, c_ctx, ada_w, ada_b)
    for l in range(depth):
        need_ctx = l < depth - 1
        ml, mc = mod_lat[l], mod_ctx[l]
        g = lambda j: norm_g[l, j].reshape(1, D_MODEL)
        w_in_l = _perm_w_in(w_in[l])
        pl_s5, pl_hy, pl_ret, pl_mla = _inproj(x, ml, g(0), w_in_l)
        pc_s5, pc_hy, pc_ret, pc_mla = _inproj(ctx, mc, g(0), w_in_l)
        s5_p = (s5_lam_re[l], s5_lam_im[l], s5_log_dt[l], s5_b_re[l], s5_b_im[l], s5_c_re[l], s5_c_im[l],
                s5_d[l], s5_glu_w[l], s5_glu_b[l])
        hy_p = (hy_conv_w[l], hy_conv_b[l], hy_w1[l], hy_b1[l], hy_w2[l], hy_b2[l], hy_w3[l], hy_freq[l],
                hy_deltas[l], hy_bias[l])
        y_s5, yc_s5 = _s5_mixer(pl_s5, pc_s5, s5_p, need_ctx)
        y_hy = _hyena_lat(pl_hy, hy_p)
        y_ret, yc_ret = _ret_mixer(pl_ret, pc_ret, (ret_decay_exp[l], ret_gn_g[l]), need_ctx)
        y_mla, yc_mla = _mla_mixer(pl_mla, pc_mla, (mla_q_norm_g[l], mla_kv_norm_g[l], mla_w_uq[l], mla_w_ukv[l]),
                                   need_ctx)
        w_out_l = w_out[l].astype(BF16)
        ffn_p = (ffn_w_up[l], ffn_conv_w[l], ffn_conv_b[l], ffn_w_down[l])
        x = _outproj((y_s5, y_hy, y_ret, y_mla), w_out_l, x, ml, g(1))
        x = _ffn(x, ml, g(2), g(3), *ffn_p)
        if need_ctx:
            yc_hy = _hyena_ctx(pc_hy, hy_p)
            ctx = _outproj((yc_s5, yc_hy, yc_ret, yc_mla), w_out_l, ctx, mc, g(1))
            ctx = _ffn(ctx, mc, g(2), g(3), *ffn_p)
    return x
```

```python
import functools
import math

import numpy as np
import jax
import jax.numpy as jnp
from jax import lax
from jax.experimental import pallas as pl
from jax.experimental.pallas import tpu as pltpu

F32 = jnp.float32
BF16 = jnp.bfloat16
HIGHEST = lax.Precision.HIGHEST

EPS = 1e-6
D_MODEL = 1024
W_GROUP = 256
GRID_W = 64
S5_CH, S5_GROUPS, S5_STATE = 16, 16, 64
S5_COLS = S5_GROUPS * S5_STATE
S5_SUB = 16
S5_GRP = 128
HY_ORDER, HY_BANDS, HY_HIDDEN = 2, 16, 64
RET_HEADS, RET_DK = 4, 64
RET_ROPE_BASE = 10000.0
MLA_HEADS, MLA_NOPE, MLA_ROPE, MLA_V = 4, 64, 32, 64
MLA_Q_RANK, MLA_KV_RANK = 192, 128
MLA_QK = MLA_NOPE + MLA_ROPE
MLA_HEAD_PAD = 128
ROPE_BASE = 10000.0
D_FF = 2816
FF_CHUNK = 256
N_IN = 2560
FFT_N2 = 64

VMEM_LIMIT_BYTES = 56 * 1024 * 1024


def _call(kernel, *, name, grid, in_specs, out_specs, out_shape, scratch_shapes=()):
    return pl.pallas_call(
        kernel, name=name, grid=grid, in_specs=in_specs, out_specs=out_specs, out_shape=out_shape,
        scratch_shapes=scratch_shapes,
        compiler_params=pltpu.CompilerParams(dimension_semantics=("arbitrary",) * len(grid),
                                             vmem_limit_bytes=VMEM_LIMIT_BYTES))


def _dot(a, b):
    return jnp.dot(a, b, preferred_element_type=F32)


def _dot_nt(a, b):
    return lax.dot_general(a, b, (((1,), (1,)), ((), ())), preferred_element_type=F32)


def _dot_tn(a, b):
    return lax.dot_general(a, b, (((0,), (0,)), ((), ())), preferred_element_type=F32)


def _rms(x, g):
    return x * lax.rsqrt(jnp.mean(x * x, axis=-1, keepdims=True) + EPS) * g


def _silu(x):
    return x * jax.nn.sigmoid(x)


def _mod_kernel(c_ref, w_ref, b_ref, o_ref):
    s = _silu(c_ref[...])
    o_ref[0] = jnp.dot(s, w_ref[0], preferred_element_type=F32, precision=HIGHEST) + b_ref[0]


def _adaln(c, c_ctx, ada_w, ada_b):
    bsz, depth, n6 = c.shape[0], ada_w.shape[0], ada_w.shape[2]
    rows = 8
    assert bsz + 1 <= rows
    cc = jnp.concatenate([c, c_ctx[None], jnp.zeros((rows - bsz - 1, D_MODEL), F32)], axis=0)
    tn = 512
    out = _call(
        _mod_kernel, name="adaln", grid=(depth, n6 // tn),
        in_specs=[pl.BlockSpec((rows, D_MODEL), lambda l, j: (0, 0)),
                  pl.BlockSpec((1, D_MODEL, tn), lambda l, j: (l, 0, j)),
                  pl.BlockSpec((1, 1, tn), lambda l, j: (l, 0, j))],
        out_specs=pl.BlockSpec((1, rows, tn), lambda l, j: (l, 0, j)),
        out_shape=jax.ShapeDtypeStruct((depth, rows, n6), F32))(cc, ada_w, ada_b.reshape(depth, 1, n6))
    mod = out.reshape(depth, rows, 6, D_MODEL)
    return mod[:, :bsz], jnp.broadcast_to(mod[:, bsz:bsz + 1], (depth, bsz, 6, D_MODEL))


def _inproj_kernel(x_ref, mod_ref, g_ref, w_ref, s5_ref, hy_ref, ret_ref, mla_ref):
    h = _rms(x_ref[0], g_ref[...]) * (1.0 + mod_ref[0, 1:2, :]) + mod_ref[0, 0:1, :]
    hb = h.astype(BF16)
    s5_ref[0] = _dot(hb, w_ref[:, 0:256]).astype(BF16)
    hy_ref[0] = _dot(hb, w_ref[:, 256:1024]).astype(BF16)
    ret_ref[0] = _dot(hb, w_ref[:, 1024:2048]).astype(BF16)
    mla_ref[0] = _dot(hb, w_ref[:, 2048:2560]).astype(BF16)


def _inproj(x, mod, g, w):
    bsz, n, _ = x.shape
    tm = min(512, n)
    row = lambda width: pl.BlockSpec((1, tm, width), lambda b, i: (b, i, 0))
    shp = lambda width: jax.ShapeDtypeStruct((bsz, n, width), BF16)
    return _call(
        _inproj_kernel, name="inproj", grid=(bsz, n // tm),
        in_specs=[row(D_MODEL), pl.BlockSpec((1, 6, D_MODEL), lambda b, i: (b, 0, 0)),
                  pl.BlockSpec((1, D_MODEL), lambda b, i: (0, 0)),
                  pl.BlockSpec((D_MODEL, N_IN), lambda b, i: (0, 0))],
        out_specs=[row(256), row(768), row(1024), row(512)],
        out_shape=[shp(256), shp(768), shp(1024), shp(512)])(x, mod, g, w)


def _perm_w_in(w):
    z = lambda k: jnp.zeros((D_MODEL, k), w.dtype)
    return jnp.concatenate([w[:, :2048], w[:, 2048:2240], z(64), w[:, 2240:2368], w[:, 2368:2400], z(96)],
                           axis=1).astype(BF16)


def _gelu_tanh(x):
    return 0.5 * x * (1.0 + jnp.tanh(math.sqrt(2.0 / math.pi) * (x + 0.044715 * (x * x * x))))


def _s5_kernel(*refs, rev, tm, final):
    if final:
        (u_ref, bblk_ref, cblk_ref, tab_ref, tri_ref, h0_ref, yprev_ref, d_ref, gw_ref, gb_ref,
         y_ref, hfin_ref, hs_scr, carry_scr) = refs
    else:
        u_ref, bblk_ref, cblk_ref, tab_ref, tri_ref, h0_ref, y_ref, hfin_ref, hs_scr, carry_scr = refs
    nc = S5_COLS

    @pl.when(pl.program_id(1) == 0)
    def _():
        carry_scr[...] = h0_ref[0]

    u = u_ref[0]
    bu = _dot(u, bblk_ref[...])
    pre_r, pre_i, post_r, post_i = tab_ref[0], tab_ref[1], tab_ref[2], tab_ref[3]
    car_r, car_i = tab_ref[4, 0:S5_SUB, :], tab_ref[5, 0:S5_SUB, :]
    cr, ci = carry_scr[:, 0:nc], carry_scr[:, nc:]
    groups = range(tm // S5_GRP)
    subs = range(S5_GRP // S5_SUB)
    for g in (reversed(groups) if rev else groups):
        r0 = g * S5_GRP
        br, bi = bu[r0:r0 + S5_GRP, 0:nc], bu[r0:r0 + S5_GRP, nc:]
        gr = pre_r * br - pre_i * bi
        gi = pre_r * bi + pre_i * br
        cs = _dot(tri_ref[...], jnp.concatenate([gr, gi], axis=1).astype(BF16))
        csr, csi = cs[:, 0:nc], cs[:, nc:]
        wr = post_r * csr - post_i * csi
        wi = post_r * csi + post_i * csr
        for s in (reversed(subs) if rev else subs):
            a0 = s * S5_SUB
            hr = wr[a0:a0 + S5_SUB] + car_r * cr - car_i * ci
            hi = wi[a0:a0 + S5_SUB] + car_r * ci + car_i * cr
            hs_scr[r0 + a0:r0 + a0 + S5_SUB, 0:nc] = hr.astype(BF16)
            hs_scr[r0 + a0:r0 + a0 + S5_SUB, nc:] = hi.astype(BF16)
            edge = 0 if rev else S5_SUB - 1
            cr, ci = hr[edge:edge + 1], hi[edge:edge + 1]
    carry = jnp.concatenate([cr, ci], axis=1)
    carry_scr[...] = carry
    hfin_ref[0] = carry
    y = _dot(hs_scr[...], cblk_ref[...])
    if final:
        yt = _gelu_tanh(yprev_ref[0] + y + d_ref[...] * u.astype(F32))
        z = _dot(yt.astype(BF16), gw_ref[...]) + gb_ref[...]
        y_ref[0] = (yt * jax.nn.sigmoid(z)).astype(y_ref.dtype)
    else:
        y_ref[0] = y


def _s5_dir(u, bblk, cblk, tab, tri, h0, rev, final=None):
    bsz, n, _ = u.shape
    tm = min(256, n)
    nb = n // tm
    blk = (lambda i: nb - 1 - i) if rev else (lambda i: i)
    row = pl.BlockSpec((1, tm, W_GROUP), lambda b, i: (b, blk(i), 0))
    full = lambda a: pl.BlockSpec(a.shape, lambda b, i: (0,) * a.ndim)
    state = pl.BlockSpec((1, 1, 2 * S5_COLS), lambda b, i: (b, 0, 0))
    ins = [u, bblk, cblk, tab, tri, h0]
    in_specs = [row, full(bblk), full(cblk), full(tab), full(tri), state]
    if final is not None:
        ins += list(final)
        in_specs += [row] + [full(a) for a in final[1:]]
    return _call(
        functools.partial(_s5_kernel, rev=rev, tm=tm, final=final is not None),
        name="s5_bwd" if rev else "s5_fwd", grid=(bsz, nb), in_specs=in_specs,
        out_specs=[row, state],
        out_shape=[jax.ShapeDtypeStruct((bsz, n, W_GROUP), BF16 if final is not None else F32),
                   jax.ShapeDtypeStruct((bsz, 1, 2 * S5_COLS), F32)],
        scratch_shapes=[pltpu.VMEM((tm, 2 * S5_COLS), BF16), pltpu.VMEM((1, 2 * S5_COLS), F32)])(*ins)


def _s5_weights(lam_re, lam_im, log_dt, b_re, b_im, c_re, c_im):
    g, p, ch = S5_GROUPS, S5_STATE, S5_CH
    dt = jnp.exp(log_dt)[..., None]
    re_dt = (lam_re * dt).reshape(2, 1, g * p)
    im_dt = (lam_im * dt).reshape(2, 1, g * p)
    lam = lax.complex(lam_re, lam_im)
    abar = jnp.exp(lax.complex(lam_re * dt, lam_im * dt))
    bbar = ((abar - 1.0) / lam)[..., None] * lax.complex(b_re, b_im)
    eye = jnp.eye(g, dtype=F32)
    place_b = lambda a: jnp.einsum('dgpc,gh->dgchp', a, eye).reshape(2, g * ch, g * p)
    bblk = jnp.concatenate([place_b(jnp.real(bbar)), place_b(jnp.imag(bbar))], axis=2).astype(BF16)
    place_c = lambda a: jnp.einsum('dgcp,gh->dgphc', a, eye).reshape(2, g * p, g * ch)
    cblk = jnp.concatenate([place_c(c_re), place_c(-c_im)], axis=1).astype(BF16)

    def powers(k, d):
        kk = k[:, None]
        mag = jnp.exp(kk * re_dt[d])
        return [mag * jnp.cos(kk * im_dt[d]), mag * jnp.sin(kk * im_dt[d])]

    i_sub = jnp.tile(jnp.arange(S5_SUB, dtype=F32), S5_GRP // S5_SUB)
    tab_f = jnp.stack(powers(-i_sub, 0) + powers(i_sub, 0) + powers(i_sub + 1.0, 0))
    tab_b = jnp.stack(powers(i_sub, 1) + powers(-i_sub, 1) + powers(S5_SUB - i_sub, 1))
    ii = np.arange(S5_GRP)
    same = (ii[:, None] // S5_SUB) == (ii[None, :] // S5_SUB)
    tri_f = jnp.asarray(same & (ii[None, :] <= ii[:, None]), F32).astype(BF16)
    tri_b = jnp.asarray(same & (ii[None, :] >= ii[:, None]), F32).astype(BF16)
    return (bblk[0], cblk[0], tab_f, tri_f), (bblk[1], cblk[1], tab_b, tri_b)


def _s5_mixer(u_lat, u_ctx, prm, need_ctx):
    (lam_re, lam_im, log_dt, b_re, b_im, c_re, c_im, d_skip, glu_w, glu_b) = prm
    wf, wb = _s5_weights(lam_re, lam_im, log_dt, b_re, b_im, c_re, c_im)
    bsz = u_lat.shape[0]
    zero = jnp.zeros((bsz, 1, 2 * S5_COLS), F32)
    fin = lambda yprev: (yprev, d_skip.reshape(1, W_GROUP), glu_w.astype(BF16), glu_b.reshape(1, W_GROUP))
    yc_f, hc_f = _s5_dir(u_ctx, *wf, zero, rev=False)
    yc, hc_b = _s5_dir(u_ctx, *wb, zero, rev=True, final=fin(yc_f))
    yl_f, _ = _s5_dir(u_lat, *wf, hc_f, rev=False)
    yl, _ = _s5_dir(u_lat, *wb, hc_b, rev=True, final=fin(yl_f))
    return yl, (yc if need_ctx else None)


def _rot_half(x, half, period):
    lane = lax.broadcasted_iota(jnp.int32, x.shape, 1)
    width = x.shape[1]
    return jnp.where((lane % period) < half, pltpu.roll(x, width - half, axis=1), pltpu.roll(x, half, axis=1))


def _ret_kernel(*refs, rev, tm, rotate, final):
    refs = list(refs)
    q_ref, k_ref, v_ref = refs[:3]
    del refs[:3]
    if rotate:
        cos_ref, sin_ref = refs[:2]
        del refs[:2]
    lgt_ref, lgc_ref, s0_ref = refs[:3]
    del refs[:3]
    if final:
        of_ref, g_ref, gn_ref, avg_ref = refs[:4]
        del refs[:4]
    o_ref, sfin_ref, d_scr, xz_scr, s_scr = refs
    hd = RET_HEADS * RET_DK

    @pl.when((pl.program_id(0) == 0) & (pl.program_id(1) == 0))
    def _():
        ri = lax.broadcasted_iota(jnp.int32, (tm, tm), 0)
        ci = lax.broadcasted_iota(jnp.int32, (tm, tm), 1)
        diff = (ci - ri) if rev else (ri - ci)
        dpos = jnp.maximum(diff, 0).astype(F32)
        for h in range(RET_HEADS):
            d_scr[h] = jnp.where(diff >= 0, jnp.exp(dpos * lgt_ref[h]), 0.0)
        pos = lax.broadcasted_iota(jnp.int32, (tm, hd), 0).astype(F32)
        lgc = lgc_ref[...]
        xz_scr[0] = jnp.exp(((tm - pos) if rev else (pos + 1.0)) * lgc)
        xz_scr[1] = jnp.exp((pos if rev else (tm - 1.0 - pos)) * lgc)

    @pl.when(pl.program_id(1) == 0)
    def _():
        s_scr[...] = s0_ref[0]

    q = q_ref[0].astype(F32)
    k = k_ref[0].astype(F32) * (RET_DK ** -0.5)
    if rotate:
        cos, sin = cos_ref[...], sin_ref[...]
        q = q * cos + _rot_half(q, RET_DK // 2, RET_DK) * sin
        k = k * cos + _rot_half(k, RET_DK // 2, RET_DK) * sin
    qb, kb, v = q.astype(BF16), k.astype(BF16), v_ref[0]
    head = lax.broadcasted_iota(jnp.int32, (tm, hd), 1) // RET_DK
    o = _dot((q * xz_scr[0]).astype(BF16), s_scr[...].astype(BF16))
    for h in range(RET_HEADS):
        m = head == h
        s = _dot_nt(jnp.where(m, qb, jnp.zeros_like(qb)), kb)
        oh = _dot((s * d_scr[h]).astype(BF16), v)
        o = o + jnp.where(m, oh, 0.0)
    ds = _dot_tn(kb, (v.astype(F32) * xz_scr[1]).astype(BF16))
    rh = lax.broadcasted_iota(jnp.int32, (hd, hd), 0) // RET_DK
    ch = lax.broadcasted_iota(jnp.int32, (hd, hd), 1) // RET_DK
    s_new = s_scr[...] * jnp.exp(tm * lgc_ref[...]) + jnp.where(rh == ch, ds, 0.0)
    s_scr[...] = s_new
    sfin_ref[0] = s_new
    if final:
        o = o + of_ref[0]
        o2 = o * o
        hi = o2.astype(BF16)
        lo = (o2 - hi.astype(F32)).astype(BF16)
        ms = _dot(hi, avg_ref[...]) + _dot(lo, avg_ref[...])
        g = g_ref[0].astype(F32)
        o_ref[0] = (_silu(g) * (o * lax.rsqrt(ms + EPS) * gn_ref[...])).astype(o_ref.dtype)
    else:
        o_ref[0] = o


def _ret_dir(p, lg, s0, rev, rope=None, final=None):
    bsz, n, _ = p.shape
    tm = min(256, n)
    nb = n // tm
    hd = RET_HEADS * RET_DK
    blk = (lambda i: nb - 1 - i) if rev else (lambda i: i)
    col = lambda j: pl.BlockSpec((1, tm, hd), lambda b, i: (b, blk(i), j))
    full = lambda a: pl.BlockSpec(a.shape, lambda b, i: (0,) * a.ndim)
    state = pl.BlockSpec((1, hd, hd), lambda b, i: (b, 0, 0))
    lgt = jnp.broadcast_to(lg[:, None, None], (RET_HEADS, 1, tm))
    lgc = jnp.repeat(lg, RET_DK).reshape(1, hd)
    ins, in_specs = [p, p, p], [col(0), col(1), col(2)]
    if rope is not None:
        ins += list(rope)
        in_specs += [pl.BlockSpec((tm, hd), lambda b, i: (blk(i), 0))] * 2
    ins += [lgt, lgc, s0]
    in_specs += [full(lgt), full(lgc), state]
    if final is not None:
        of, gn = final
        avg = jnp.asarray(np.kron(np.eye(RET_HEADS), np.full((RET_DK, RET_DK), 1.0 / RET_DK)), F32).astype(BF16)
        ins += [of, p, gn.reshape(1, hd), avg]
        in_specs += [pl.BlockSpec((1, tm, hd), lambda b, i: (b, blk(i), 0)), col(3), full(gn.reshape(1, hd)), full(avg)]
    return _call(
        functools.partial(_ret_kernel, rev=rev, tm=tm, rotate=rope is not None, final=final is not None),
        name="ret_bwd" if rev else "ret_fwd", grid=(bsz, nb), in_specs=in_specs,
        out_specs=[pl.BlockSpec((1, tm, hd), lambda b, i: (b, blk(i), 0)), state],
        out_shape=[jax.ShapeDtypeStruct((bsz, n, hd), BF16 if final is not None else F32),
                   jax.ShapeDtypeStruct((bsz, hd, hd), F32)],
        scratch_shapes=[pltpu.VMEM((RET_HEADS, tm, tm), F32), pltpu.VMEM((2, tm, hd), F32),
                        pltpu.VMEM((hd, hd), F32)])(*ins)


def _ret_mixer(p_lat, p_ctx, prm, need_ctx):
    decay_exp, gn_g = prm
    lg = jnp.log1p(-jnp.exp2(-decay_exp))
    bsz, n, _ = p_lat.shape
    hd = RET_HEADS * RET_DK
    theta = RET_ROPE_BASE ** (-jnp.linspace(0.0, 1.0, RET_DK // 2, dtype=F32))
    ang = jnp.arange(n, dtype=F32)[:, None] * theta
    cos = jnp.tile(jnp.cos(ang), (1, 2 * RET_HEADS))
    sin = jnp.tile(jnp.concatenate([-jnp.sin(ang), jnp.sin(ang)], axis=1), (1, RET_HEADS))
    zero = jnp.zeros((bsz, hd, hd), F32)
    oc_f, sc_f = _ret_dir(p_ctx, lg[0], zero, rev=False)
    yc, sc_b = _ret_dir(p_ctx, lg[1], zero, rev=True, final=(oc_f, gn_g))
    ol_f, _ = _ret_dir(p_lat, lg[0], sc_f, rev=False, rope=(cos, sin))
    yl, _ = _ret_dir(p_lat, lg[1], sc_b, rev=True, rope=(cos, sin), final=(ol_f, gn_g))
    return yl, (yc if need_ctx else None)


def _mla_prep_kernel(*refs, rotate):
    if rotate:
        p_ref, gq_ref, gkv_ref, wq_ref, wkv_ref, e_ref, cos_ref, sin_ref, q_ref, k_ref, v_ref = refs
    else:
        p_ref, gq_ref, gkv_ref, wq_ref, wkv_ref, e_ref, q_ref, k_ref, v_ref = refs
    p = p_ref[0]
    cq = p[:, 0:256].astype(F32)
    cqn = cq * lax.rsqrt(jnp.sum(cq * cq, axis=-1, keepdims=True) * (1.0 / MLA_Q_RANK) + EPS) * gq_ref[...]
    q = _dot(cqn.astype(BF16), wq_ref[...])
    ckv = p[:, 256:384].astype(F32)
    kv = _dot(_rms(ckv, gkv_ref[...]).astype(BF16), wkv_ref[...])
    width = MLA_HEADS * MLA_HEAD_PAD
    k = kv[:, 0:width] + _dot(p[:, 384:512], e_ref[...])
    if rotate:
        cos, sin = cos_ref[...], sin_ref[...]
        lane = lax.broadcasted_iota(jnp.int32, q.shape, 1) % MLA_HEAD_PAD
        first = lane < MLA_NOPE + MLA_ROPE // 2
        rot = lambda x: jnp.where(first, pltpu.roll(x, width - MLA_ROPE // 2, axis=1),
                                  pltpu.roll(x, MLA_ROPE // 2, axis=1))
        q = q * cos + rot(q) * sin
        k = k * cos + rot(k) * sin
    q_ref[0] = (q * (MLA_QK ** -0.5 * math.log2(math.e))).astype(BF16)
    k_ref[0] = k.astype(BF16)
    v_ref[0] = kv[:, width:].astype(BF16)


def _mla_prep(p, wts, rope=None):
    bsz, n, _ = p.shape
    tm = min(512, n)
    width = MLA_HEADS * MLA_HEAD_PAD
    full = lambda a: pl.BlockSpec(a.shape, lambda b, i: (0,) * a.ndim)
    row = lambda w: pl.BlockSpec((1, tm, w), lambda b, i: (b, i, 0))
    ins = [p] + list(wts)
    in_specs = [row(512)] + [full(a) for a in wts]
    if rope is not None:
        ins += list(rope)
        in_specs += [pl.BlockSpec((tm, width), lambda b, i: (i, 0))] * 2
    return _call(
        functools.partial(_mla_prep_kernel, rotate=rope is not None), name="mla_prep", grid=(bsz, n // tm),
        in_specs=in_specs, out_specs=[row(width), row(width), row(MLA_HEADS * MLA_V)],
        out_shape=[jax.ShapeDtypeStruct((bsz, n, width), BF16), jax.ShapeDtypeStruct((bsz, n, width), BF16),
                   jax.ShapeDtypeStruct((bsz, n, MLA_HEADS * MLA_V), BF16)])(*ins)


def _kv_blocks(nk, target=1152):
    assert nk % 128 == 0
    n128 = nk // 128
    nblk = max(1, -(-nk // target))
    sizes = [(n128 // nblk + (1 if j < n128 % nblk else 0)) * 128 for j in range(nblk)]
    starts = [sum(sizes[:j]) for j in range(nblk)]
    return list(zip(starts, sizes))


def _attn_kernel(q_ref, k_ref, v_ref, o_ref, *, blocks):
    q = q_ref[0]
    tq = q.shape[0]
    head = lax.broadcasted_iota(jnp.int32, (tq, MLA_HEADS * MLA_V), 1) // MLA_V
    units = [(j, h) for j in range(len(blocks)) for h in range(MLA_HEADS)]

    def scores(j, h):
        k0, kn = blocks[j]
        c0 = h * MLA_HEAD_PAD
        return _dot_nt(q[:, c0:c0 + MLA_HEAD_PAD], k_ref[0, k0:k0 + kn, c0:c0 + MLA_HEAD_PAD])

    m = [jnp.full((tq, 1), -jnp.inf, F32)] * MLA_HEADS
    l = [jnp.zeros((tq, 1), F32)] * MLA_HEADS
    acc = jnp.zeros((tq, MLA_HEADS * MLA_V), F32)
    s_next = scores(*units[0])
    for idx, (j, h) in enumerate(units):
        s = s_next
        if idx + 1 < len(units):
            s_next = scores(*units[idx + 1])
        k0, kn = blocks[j]
        m_new = jnp.maximum(m[h], jnp.max(s, axis=-1, keepdims=True))
        e = jnp.exp2(s - m_new)
        alpha = jnp.exp2(m[h] - m_new)
        l[h] = alpha * l[h] + jnp.sum(e, axis=-1, keepdims=True)
        m[h] = m_new
        pv = _dot(e.astype(BF16), v_ref[0, k0:k0 + kn, :])
        acc = jnp.where(head == h, alpha * acc + pv, acc)
    inv = jnp.zeros((tq, MLA_HEADS * MLA_V), F32)
    for h in range(MLA_HEADS):
        inv = jnp.where(head == h, 1.0 / l[h], inv)
    o_ref[0] = (acc * inv).astype(o_ref.dtype)


def _attention(q, k, v):
    bsz, n, width = q.shape
    nk = k.shape[1]
    tq = min(256, n)
    return _call(
        functools.partial(_attn_kernel, blocks=_kv_blocks(nk)), name="mla_attn", grid=(bsz, n // tq),
        in_specs=[pl.BlockSpec((1, tq, width), lambda b, i: (b, i, 0)),
                  pl.BlockSpec((1, nk, width), lambda b, i: (b, 0, 0)),
                  pl.BlockSpec((1, nk, MLA_HEADS * MLA_V), lambda b, i: (b, 0, 0))],
        out_specs=pl.BlockSpec((1, tq, MLA_HEADS * MLA_V), lambda b, i: (b, i, 0)),
        out_shape=jax.ShapeDtypeStruct((bsz, n, MLA_HEADS * MLA_V), BF16))(q, k, v)


def _mla_mixer(p_lat, p_ctx, prm, need_ctx):
    q_norm_g, kv_norm_g, w_uq, w_ukv = prm
    n = p_lat.shape[1]
    width = MLA_HEADS * MLA_HEAD_PAD
    wq = jnp.pad(w_uq.reshape(MLA_Q_RANK, MLA_HEADS, MLA_QK),
                 ((0, 256 - MLA_Q_RANK), (0, 0), (0, MLA_HEAD_PAD - MLA_QK))).reshape(256, width).astype(BF16)
    wkv = w_ukv.reshape(MLA_KV_RANK, MLA_HEADS, MLA_NOPE + MLA_V)
    wk = jnp.pad(wkv[:, :, :MLA_NOPE], ((0, 0), (0, 0), (0, MLA_HEAD_PAD - MLA_NOPE))).reshape(MLA_KV_RANK, width)
    wv = wkv[:, :, MLA_NOPE:].reshape(MLA_KV_RANK, MLA_HEADS * MLA_V)
    wkv = jnp.concatenate([wk, wv], axis=1).astype(BF16)
    place = np.zeros((128, width), np.float32)
    for h in range(MLA_HEADS):
        for j in range(MLA_ROPE):
            place[j, h * MLA_HEAD_PAD + MLA_NOPE + j] = 1.0
    place = jnp.asarray(place).astype(BF16)
    gq = jnp.pad(q_norm_g, (0, 256 - MLA_Q_RANK)).reshape(1, 256)
    wts = (gq, kv_norm_g.reshape(1, MLA_KV_RANK), wq, wkv, place)
    pos = jnp.arange(n)
    row, colp = (pos // GRID_W).astype(F32), (pos % GRID_W).astype(F32)
    n_freq = MLA_ROPE // 4
    inv = ROPE_BASE ** (-jnp.arange(n_freq, dtype=F32) / n_freq)
    ang = jnp.concatenate([row[:, None] * inv, colp[:, None] * inv], axis=-1)
    ones, zeros = jnp.ones((n, MLA_NOPE), F32), jnp.zeros((n, MLA_NOPE), F32)
    tail = MLA_HEAD_PAD - MLA_QK
    cos = jnp.concatenate([ones, jnp.cos(ang), jnp.cos(ang), jnp.ones((n, tail), F32)], axis=1)
    sin = jnp.concatenate([zeros, -jnp.sin(ang), jnp.sin(ang), jnp.zeros((n, tail), F32)], axis=1)
    rope = (jnp.tile(cos, (1, MLA_HEADS)), jnp.tile(sin, (1, MLA_HEADS)))
    q_c, k_c, v_c = _mla_prep(p_ctx, wts)
    q_l, k_l, v_l = _mla_prep(p_lat, wts, rope)
    y_lat = _attention(q_l, jnp.concatenate([k_c, k_l], axis=1), jnp.concatenate([v_c, v_l], axis=1))
    y_ctx = _attention(q_c, k_c, v_c) if need_ctx else None
    return y_lat, y_ctx


def _hy_filter_kernel(z_ref, w1_ref, b1_ref, w2_ref, b2_ref, w3_ref, fr_ref, dl_ref, k_ref):
    z = z_ref[...]
    fr = fr_ref[...]
    dotp = lambda a, b: jnp.dot(a, b, preferred_element_type=F32, precision=HIGHEST)
    hid = jnp.sin(fr * (dotp(z, w1_ref[...]) + b1_ref[...]))
    hid = jnp.sin(fr * (dotp(hid, w2_ref[...]) + b2_ref[...]))
    k = dotp(hid, w3_ref[...]) * jnp.exp(-z[:, 0:1] * jnp.abs(dl_ref[...]))
    ss = jnp.sum(k * k, axis=0, keepdims=True)
    r = lax.rsqrt(ss[:, 0:W_GROUP] + ss[:, W_GROUP:] + EPS)
    k_ref[...] = (k * jnp.concatenate([r, r], axis=1)).astype(k_ref.dtype)


def _hy_filter(n, prm):
    w1, b1, w2, b2, w3, freq, deltas = prm
    pos = jnp.arange(n, dtype=F32)
    t01 = pos / (n - 1)
    bands = jnp.linspace(1e-4, HY_BANDS - 1, HY_BANDS, dtype=F32)
    ang = (2.0 * math.pi / n) * pos[:, None] * bands[None, :]
    z = jnp.concatenate([t01[:, None], jnp.cos(ang), -jnp.sin(ang)], axis=-1)
    emb = z.shape[1]
    z = jnp.pad(z, ((0, 0), (0, 128 - emb)))
    w1p = jnp.pad(w1, ((0, 128 - emb), (0, 0)))
    ins = [z, w1p, b1.reshape(1, -1), w2, b2.reshape(1, -1), w3, freq.reshape(1, -1), deltas.reshape(1, -1)]
    full = lambda a: pl.BlockSpec(a.shape, lambda o: (0,) * a.ndim)
    ncol = w3.shape[1]
    cw = 2 * W_GROUP
    in_specs = [full(a) for a in ins]
    in_specs[5] = pl.BlockSpec((HY_HIDDEN, cw), lambda o: (0, o))
    in_specs[7] = pl.BlockSpec((1, cw), lambda o: (0, o))
    return _call(_hy_filter_kernel, name="hy_filter", grid=(ncol // cw,), in_specs=in_specs,
                 out_specs=pl.BlockSpec((n, cw), lambda o: (0, o)),
                 out_shape=jax.ShapeDtypeStruct((n, ncol), BF16))(*ins)


def _dwconv_rows(x, w_ref, b_ref):
    n = x.shape[0]
    r = lax.broadcasted_iota(jnp.int32, x.shape, 0)
    prev = jnp.where(r == 0, 0.0, pltpu.roll(x, 1, axis=0))
    nxt = jnp.where(r == n - 1, 0.0, pltpu.roll(x, n - 1, axis=0))
    return prev * w_ref[0:1, :] + x * w_ref[1:2, :] + nxt * w_ref[2:3, :] + b_ref[...]


def _hy_pre_kernel(p_ref, w_ref, b_ref, o_ref):
    o_ref[0] = _dwconv_rows(p_ref[0].astype(F32), w_ref, b_ref).astype(o_ref.dtype)


def _hy_pre(p, conv_w, conv_b):
    bsz, n, ch = p.shape
    cb = 256
    return _call(
        _hy_pre_kernel, name="hy_dwconv", grid=(bsz, ch // cb),
        in_specs=[pl.BlockSpec((1, n, cb), lambda b, j: (b, 0, j)), pl.BlockSpec((3, cb), lambda b, j: (0, j)),
                  pl.BlockSpec((1, cb), lambda b, j: (0, j))],
        out_specs=pl.BlockSpec((1, n, cb), lambda b, j: (b, 0, j)),
        out_shape=jax.ShapeDtypeStruct((bsz, n, ch), BF16))(p, conv_w, conv_b.reshape(1, ch))


def _fft_consts(n_tok):
    n = 2 * n_tok
    n2n = FFT_N2
    n1n = n // n2n
    k1 = np.arange(n1n)[None, :, None]
    n1 = np.arange(n1n // 2)[None, None, :]
    n2 = np.arange(n2n)[:, None, None]
    th = 2.0 * np.pi * ((k1 * (n2n * n1 + n2)) % n) / n
    fa = np.concatenate([np.cos(th), -np.sin(th)], axis=1)
    fat = np.transpose(fa, (0, 2, 1)) / n
    a = np.arange(n2n)
    t2 = 2.0 * np.pi * ((a[:, None] * a[None, :]) % n2n) / n2n
    fr, fi = np.cos(t2), -np.sin(t2)
    f2 = np.block([[fr, -fi], [fi, fr]])
    f2i = np.block([[fr, fi], [-fi, fr]])
    as_bf16 = lambda m: jnp.asarray(m, F32).astype(BF16)
    return as_bf16(fa), as_bf16(fat), as_bf16(f2), as_bf16(f2i)


def _fft_a_kernel(x_ref, fa_ref, o_ref, *, nb, cw):
    for j in range(nb):
        o_ref[0, :, j * cw:(j + 1) * cw] = _dot(fa_ref[j], x_ref[0, :, j * cw:(j + 1) * cw]).astype(o_ref.dtype)


def _fft_a(x, fa, ctot):
    bsz, h1, _ = x.shape
    cw, nb = 256, 8
    ncb = ctot // cw
    if ncb == 1:
        blkw = nb * cw
        grid = (bsz, FFT_N2 // nb)
        xs = pl.BlockSpec((1, h1, blkw), lambda b, j: (b, 0, j))
        fs = pl.BlockSpec((nb, 4 * h1, h1), lambda b, j: (j, 0, 0))
        os = pl.BlockSpec((1, 4 * h1, blkw), lambda b, j: (b, 0, j))
        kern = functools.partial(_fft_a_kernel, nb=nb, cw=cw)
    else:
        grid = (bsz, FFT_N2 * ncb)
        xs = pl.BlockSpec((1, h1, cw), lambda b, j: (b, 0, j))
        fs = pl.BlockSpec((1, 4 * h1, h1), lambda b, j: (j // ncb, 0, 0))
        os = pl.BlockSpec((1, 4 * h1, cw), lambda b, j: (b, 0, j))
        kern = functools.partial(_fft_a_kernel, nb=1, cw=cw)
    return _call(kern, name="hy_fft_a", grid=grid, in_specs=[xs, fs], out_specs=os,
                 out_shape=jax.ShapeDtypeStruct((bsz, 4 * h1, FFT_N2 * ctot), BF16))(x, fa)


def _fft_spec_kernel(a_ref, k0_ref, f2_ref, h_ref, *, kb):
    c = W_GROUP
    for j in range(kb):
        for o in range(HY_ORDER):
            af = jnp.concatenate([a_ref[0, 0, j, :, 2 * o * c:(2 * o + 1) * c],
                                  a_ref[0, 1, j, :, 2 * o * c:(2 * o + 1) * c]], axis=0)
            ab = jnp.concatenate([a_ref[0, 0, j, :, (2 * o + 1) * c:(2 * o + 2) * c],
                                  a_ref[0, 1, j, :, (2 * o + 1) * c:(2 * o + 2) * c]], axis=0)
            xf, xb = _dot(f2_ref[...], af), _dot(f2_ref[...], ab)
            k0 = k0_ref[:, (2 * o + 1) * c:(2 * o + 2) * c].astype(F32)
            h_ref[o, j, 0:FFT_N2, :] = xf[0:FFT_N2] + xb[0:FFT_N2] - k0
            h_ref[o, j, FFT_N2:, :] = xf[FFT_N2:] - xb[FFT_N2:]


def _fft_spec(a5, k0, f2):
    n1n = a5.shape[2]
    kb = 8
    ctot = a5.shape[4]
    return _call(
        functools.partial(_fft_spec_kernel, kb=kb), name="hy_fft_spec", grid=(n1n // kb,),
        in_specs=[pl.BlockSpec((1, 2, kb, FFT_N2, ctot), lambda i: (0, 0, i, 0, 0)),
                  pl.BlockSpec((1, ctot), lambda i: (0, 0)), pl.BlockSpec(f2.shape, lambda i: (0, 0))],
        out_specs=pl.BlockSpec((HY_ORDER, kb, 2 * FFT_N2, W_GROUP), lambda i: (0, i, 0, 0)),
        out_shape=jax.ShapeDtypeStruct((HY_ORDER, n1n, 2 * FFT_N2, W_GROUP), F32))(a5, k0, f2)


def _fft_c_kernel(a_ref, h_ref, f2_ref, f2i_ref, o_ref, *, kb):
    for j in range(kb):
        a = jnp.concatenate([a_ref[0, 0, j], a_ref[0, 1, j]], axis=0)
        x = _dot(f2_ref[...], a)
        xr, xi = x[0:FFT_N2], x[FFT_N2:]
        hr, hi = h_ref[0, j, 0:FFT_N2, :], h_ref[0, j, FFT_N2:, :]
        y = jnp.concatenate([xr * hr - xi * hi, xr * hi + xi * hr], axis=0).astype(BF16)
        z = _dot(f2i_ref[...], y)
        o_ref[0, 0, j] = z[0:FFT_N2].astype(o_ref.dtype)
        o_ref[0, 1, j] = z[FFT_N2:].astype(o_ref.dtype)


def _fft_c(a5, spec, order, f2, f2i):
    bsz, _, n1n, _, c = a5.shape
    kb = 8
    return _call(
        functools.partial(_fft_c_kernel, kb=kb), name="hy_fft_c", grid=(bsz, n1n // kb),
        in_specs=[pl.BlockSpec((1, 2, kb, FFT_N2, c), lambda b, i: (b, 0, i, 0, 0)),
                  pl.BlockSpec((1, kb, 2 * FFT_N2, c), lambda b, i: (order, i, 0, 0)),
                  pl.BlockSpec(f2.shape, lambda b, i: (0, 0)), pl.BlockSpec(f2i.shape, lambda b, i: (0, 0))],
        out_specs=pl.BlockSpec((1, 2, kb, FFT_N2, c), lambda b, i: (b, 0, i, 0, 0)),
        out_shape=jax.ShapeDtypeStruct(a5.shape, BF16))(a5, spec, f2, f2i)


def _fft_ainv_kernel(z_ref, fat_ref, x_ref, v_ref, b_ref, o_ref, *, nb, cw):
    for j in range(nb):
        sl = slice(j * cw, (j + 1) * cw)
        y = _dot(fat_ref[j], z_ref[0, :, sl])
        v = v_ref[0, :, sl].astype(F32)
        o_ref[0, :, sl] = (x_ref[0, :, sl].astype(F32) * (y + v * b_ref[...])).astype(o_ref.dtype)


def _fft_ainv(z, fat, gate, v, bias):
    bsz, r4, _ = z.shape
    h1 = r4 // 4
    cw, nb = 256, 8
    dat = pl.BlockSpec((1, h1, nb * cw), lambda b, j: (b, 0, j))
    return _call(
        functools.partial(_fft_ainv_kernel, nb=nb, cw=cw), name="hy_fft_ainv", grid=(bsz, FFT_N2 // nb),
        in_specs=[pl.BlockSpec((1, r4, nb * cw), lambda b, j: (b, 0, j)),
                  pl.BlockSpec((nb, h1, r4), lambda b, j: (j, 0, 0)), dat, dat,
                  pl.BlockSpec((1, cw), lambda b, j: (0, 0))],
        out_specs=dat, out_shape=jax.ShapeDtypeStruct((bsz, h1, FFT_N2 * cw), BF16))(z, fat, gate, v, bias)


def _hyena_lat(p, prm):
    conv_w, conv_b, w1, b1, w2, b2, w3, freq, deltas, bias = prm
    bsz, n, _ = p.shape
    c = W_GROUP
    h1 = n // FFT_N2
    fa, fat, f2, f2i = _fft_consts(n)
    kf = _hy_filter(n, (w1, b1, w2, b2, w3, freq, deltas))
    ak = _fft_a(kf.reshape(1, h1, FFT_N2 * 4 * c), fa, 4 * c)
    spec = _fft_spec(ak.reshape(1, 2, 2 * h1, FFT_N2, 4 * c), kf[0:1], f2)
    u = _hy_pre(p, conv_w, conv_b)
    perm = lambda a: a.reshape(bsz, h1, FFT_N2 * c)
    x1, x2, v = perm(u[..., 0:c]), perm(u[..., c:2 * c]), perm(u[..., 2 * c:])
    z = v
    for o, gate in enumerate((x1, x2)):
        a = _fft_a(z, fa, c).reshape(bsz, 2, 2 * h1, FFT_N2, c)
        zc = _fft_c(a, spec, o, f2, f2i).reshape(bsz, 4 * h1, FFT_N2 * c)
        z = _fft_ainv(zc, fat, gate, z, bias[o].reshape(1, c))
    return z.reshape(bsz, n, c)


def _hy_ctx_kernel(p_ref, cw_ref, cb_ref, k_ref, fd_ref, fdi_ref, bias_ref, o_ref):
    c = W_GROUP
    u = _dwconv_rows(p_ref[0].astype(F32), cw_ref, cb_ref)
    nf = fd_ref.shape[0] // 2
    xk = _dot(fd_ref[...], k_ref[...])
    z = u[:, 2 * c:]
    for o, gate in enumerate((u[:, 0:c], u[:, c:2 * c])):
        kf, kb = xk[:, 2 * o * c:(2 * o + 1) * c], xk[:, (2 * o + 1) * c:(2 * o + 2) * c]
        k0 = k_ref[0:1, (2 * o + 1) * c:(2 * o + 2) * c].astype(F32)
        hr = kf[0:nf] + kb[0:nf] - k0
        hi = kf[nf:] - kb[nf:]
        x = _dot(fd_ref[...], z.astype(BF16))
        xr, xi = x[0:nf], x[nf:]
        y = jnp.concatenate([xr * hr - xi * hi, xr * hi + xi * hr], axis=0).astype(BF16)
        z = gate * (_dot(fdi_ref[...], y) + z * bias_ref[o:o + 1, :])
    o_ref[0] = z.astype(o_ref.dtype)


def _hyena_ctx(p, prm):
    conv_w, conv_b, w1, b1, w2, b2, w3, freq, deltas, bias = prm
    bsz, n, ch = p.shape
    kf = _hy_filter(n, (w1, b1, w2, b2, w3, freq, deltas))
    nn = 2 * n
    th = 2.0 * np.pi * ((np.arange(nn)[:, None] * np.arange(n)[None, :]) % nn) / nn
    fd = np.concatenate([np.cos(th), -np.sin(th)], axis=0)
    fdj = jnp.asarray(fd, F32).astype(BF16)
    fdi = jnp.asarray(fd.T / nn, F32).astype(BF16)
    ins = [p, conv_w, conv_b.reshape(1, ch), kf, fdj, fdi, bias]
    full = lambda a: pl.BlockSpec(a.shape, lambda b: (0,) * a.ndim)
    return _call(
        _hy_ctx_kernel, name="hy_ctx", grid=(bsz,),
        in_specs=[pl.BlockSpec((1, n, ch), lambda b: (b, 0, 0))] + [full(a) for a in ins[1:]],
        out_specs=pl.BlockSpec((1, n, W_GROUP), lambda b: (b, 0, 0)),
        out_shape=jax.ShapeDtypeStruct((bsz, n, W_GROUP), BF16))(*ins)


def _outproj_kernel(y0_ref, y1_ref, y2_ref, y3_ref, w_ref, x_ref, mod_ref, g_ref, o_ref):
    c = W_GROUP
    y = (_dot(y0_ref[0], w_ref[0:c, :]) + _dot(y1_ref[0], w_ref[c:2 * c, :])
         + _dot(y2_ref[0], w_ref[2 * c:3 * c, :]) + _dot(y3_ref[0], w_ref[3 * c:, :]))
    o_ref[0] = x_ref[0] + mod_ref[0, 2:3, :] * _rms(y, g_ref[...])


def _outproj(ys, w, x, mod, g):
    bsz, n, _ = x.shape
    tm = min(512, n)
    row = lambda width: pl.BlockSpec((1, tm, width), lambda b, i: (b, i, 0))
    return _call(
        _outproj_kernel, name="outproj", grid=(bsz, n // tm),
        in_specs=[row(W_GROUP)] * 4 + [pl.BlockSpec(w.shape, lambda b, i: (0, 0)), row(D_MODEL),
                                       pl.BlockSpec((1, 6, D_MODEL), lambda b, i: (b, 0, 0)),
                                       pl.BlockSpec((1, D_MODEL), lambda b, i: (0, 0))],
        out_specs=row(D_MODEL), out_shape=jax.ShapeDtypeStruct(x.shape, F32))(*ys, w, x, mod, g)


def _ffn_kernel(xp_ref, x_ref, xn_ref, mod_ref, g2_ref, g3_ref, wup_ref, cw_ref, cb_ref, wdn_ref, o_ref,
                h_scr, u_scr, acc_scr, *, tm, nchunk):
    i, nb = pl.program_id(1), pl.num_programs(1)
    shift, scale = mod_ref[0, 3:4, :], mod_ref[0, 4:5, :]
    pre = lambda x: _rms(x, g2_ref[...]) * (1.0 + scale) + shift
    h_scr[0:8, :] = (pre(xp_ref[0]) * (i > 0).astype(F32)).astype(BF16)
    h_scr[8:8 + tm, :] = pre(x_ref[0]).astype(BF16)
    h_scr[8 + tm:, :] = (pre(xn_ref[0]) * (i < nb - 1).astype(F32)).astype(BF16)
    u_scr[0] = _dot(h_scr[...], wup_ref[0])
    for c in range(nchunk):
        ub = u_scr.at[c % 2]
        if c + 1 < nchunk:
            u_scr[(c + 1) % 2] = _dot(h_scr[...], wup_ref[c + 1])
        cw = cw_ref[c]
        u = (ub[7:7 + tm, :] * cw[0:1, :] + ub[8:8 + tm, :] * cw[1:2, :]
             + ub[9:9 + tm, :] * cw[2:3, :] + cb_ref[c])
        act = (_silu(u[:, 0:FF_CHUNK]) * u[:, FF_CHUNK:]).astype(BF16)
        if c == 0:
            acc_scr[...] = _dot(act, wdn_ref[c])
        else:
            acc_scr[...] += _dot(act, wdn_ref[c])
    o_ref[0] = x_ref[0] + mod_ref[0, 5:6, :] * _rms(acc_scr[...], g3_ref[...])


def _ffn(x, mod, g2, g3, w_up, conv_w, conv_b, w_down):
    bsz, n, _ = x.shape
    tm = min(256, n)
    nb = n // tm
    nchunk = D_FF // FF_CHUNK
    inter = lambda a: jnp.concatenate([a[..., :D_FF].reshape(a.shape[:-1] + (nchunk, FF_CHUNK)),
                                       a[..., D_FF:].reshape(a.shape[:-1] + (nchunk, FF_CHUNK))], axis=-1)
    wup = jnp.transpose(inter(w_up), (1, 0, 2)).astype(BF16)
    cw = jnp.transpose(inter(conv_w), (1, 0, 2))
    cb = inter(conv_b).reshape(nchunk, 1, 2 * FF_CHUNK)
    wdn = w_down.reshape(nchunk, FF_CHUNK, D_MODEL).astype(BF16)
    r8 = tm // 8
    last8 = n // 8 - 1
    full = lambda a: pl.BlockSpec(a.shape, lambda b, i: (0,) * a.ndim)
    vec = pl.BlockSpec((1, D_MODEL), lambda b, i: (0, 0))
    return _call(
        functools.partial(_ffn_kernel, tm=tm, nchunk=nchunk), name="convffn", grid=(bsz, nb),
        in_specs=[pl.BlockSpec((1, 8, D_MODEL), lambda b, i: (b, jnp.maximum(i * r8 - 1, 0), 0)),
                  pl.BlockSpec((1, tm, D_MODEL), lambda b, i: (b, i, 0)),
                  pl.BlockSpec((1, 8, D_MODEL), lambda b, i: (b, jnp.minimum((i + 1) * r8, last8), 0)),
                  pl.BlockSpec((1, 6, D_MODEL), lambda b, i: (b, 0, 0)), vec, vec,
                  full(wup), full(cw), full(cb), full(wdn)],
        out_specs=pl.BlockSpec((1, tm, D_MODEL), lambda b, i: (b, i, 0)),
        out_shape=jax.ShapeDtypeStruct(x.shape, F32),
        scratch_shapes=[pltpu.VMEM((tm + 16, D_MODEL), BF16), pltpu.VMEM((2, tm + 16, 2 * FF_CHUNK), F32),
                        pltpu.VMEM((tm, D_MODEL), F32)])(x, x, x, mod, g2, g3, wup, cw, cb, wdn)


def kernel(x, c, ctx, c_ctx, ada_w, ada_b, norm_g, w_in, w_out, s5_lam_re, s5_lam_im, s5_log_dt, s5_b_re, s5_b_im, s5_c_re, s5_c_im, s5_d, s5_glu_w, s5_glu_b, hy_conv_w, hy_conv_b, hy_w1, hy_b1, hy_w2, hy_b2, hy_w3, hy_freq, hy_deltas, hy_bias, ret_decay_exp, ret_gn_g, mla_q_norm_g, mla_kv_norm_g, mla_w_uq, mla_w_ukv, ffn_w_up, ffn_conv_w, ffn_conv_b, ffn_w_down):
    depth = ada_w.shape[0]
    mod_lat, mod_ctx = _adaln(c, c_ctx, ada_w, ada_b)
    for l in range(depth):
        need_ctx = l < depth - 1
        ml, mc = mod_lat[l], mod_ctx[l]
        g = lambda j: norm_g[l, j].reshape(1, D_MODEL)
        w_in_l = _perm_w_in(w_in[l])
        pl_s5, pl_hy, pl_ret, pl_mla = _inproj(x, ml, g(0), w_in_l)
        pc_s5, pc_hy, pc_ret, pc_mla = _inproj(ctx, mc, g(0), w_in_l)
        s5_p = (s5_lam_re[l], s5_lam_im[l], s5_log_dt[l], s5_b_re[l], s5_b_im[l], s5_c_re[l], s5_c_im[l],
                s5_d[l], s5_glu_w[l], s5_glu_b[l])
        hy_p = (hy_conv_w[l], hy_conv_b[l], hy_w1[l], hy_b1[l], hy_w2[l], hy_b2[l], hy_w3[l], hy_freq[l],
                hy_deltas[l], hy_bias[l])
        y_s5, yc_s5 = _s5_mixer(pl_s5, pc_s5, s5_p, need_ctx)
        y_hy = _hyena_lat(pl_hy, hy_p)
        y_ret, yc_ret = _ret_mixer(pl_ret, pc_ret, (ret_decay_exp[l], ret_gn_g[l]), need_ctx)
        y_mla, yc_mla = _mla_mixer(pl_mla, pc_mla, (mla_q_norm_g[l], mla_kv_norm_g[l], mla_w_uq[l], mla_w_ukv[l]),
                                   need_ctx)
        w_out_l = w_out[l].astype(BF16)
        ffn_p = (ffn_w_up[l], ffn_conv_w[l], ffn_conv_b[l], ffn_w_down[l])
        x = _outproj((y_s5, y_hy, y_ret, y_mla), w_out_l, x, ml, g(1))
        x = _ffn(x, ml, g(2), g(3), *ffn_p)
        if need_ctx:
            yc_hy = _hyena_ctx(pc_hy, hy_p)
            ctx = _outproj((yc_s5, yc_hy, yc_ret, yc_mla), w_out_l, ctx, mc, g(1))
            ctx = _ffn(ctx, mc, g(2), g(3), *ffn_p)
    return x
```

```python
import functools
import math

import numpy as np
import jax
import jax.numpy as jnp
from jax import lax
from jax.experimental import pallas as pl
from jax.experimental.pallas import tpu as pltpu

F32 = jnp.float32
BF16 = jnp.bfloat16
HIGHEST = lax.Precision.HIGHEST

EPS = 1e-6
D_MODEL = 1024
W_GROUP = 256
GRID_W = 64
S5_CH, S5_GROUPS, S5_STATE = 16, 16, 64
S5_COLS = S5_GROUPS * S5_STATE
S5_SUB = 16
S5_GRP = 128
HY_ORDER, HY_BANDS, HY_HIDDEN = 2, 16, 64
RET_HEADS, RET_DK = 4, 64
RET_ROPE_BASE = 10000.0
MLA_HEADS, MLA_NOPE, MLA_ROPE, MLA_V = 4, 64, 32, 64
MLA_Q_RANK, MLA_KV_RANK = 192, 128
MLA_QK = MLA_NOPE + MLA_ROPE
MLA_HEAD_PAD = 128
ROPE_BASE = 10000.0
D_FF = 2816
FF_CHUNK = 256
N_IN = 2560
FFT_N2 = 64

VMEM_LIMIT_BYTES = 56 * 1024 * 1024


def _call(kernel, *, name, grid, in_specs, out_specs, out_shape, scratch_shapes=()):
    return pl.pallas_call(
        kernel, name=name, grid=grid, in_specs=in_specs, out_specs=out_specs, out_shape=out_shape,
        scratch_shapes=scratch_shapes,
        compiler_params=pltpu.CompilerParams(dimension_semantics=("arbitrary",) * len(grid),
                                             vmem_limit_bytes=VMEM_LIMIT_BYTES))


def _dot(a, b):
    return jnp.dot(a, b, preferred_element_type=F32)


def _dot_nt(a, b):
    return lax.dot_general(a, b, (((1,), (1,)), ((), ())), preferred_element_type=F32)


def _dot_tn(a, b):
    return lax.dot_general(a, b, (((0,), (0,)), ((), ())), preferred_element_type=F32)


def _rms(x, g):
    return x * lax.rsqrt(jnp.mean(x * x, axis=-1, keepdims=True) + EPS) * g


def _silu(x):
    return x * jax.nn.sigmoid(x)


def _mod_kernel(c_ref, w_ref, b_ref, o_ref):
    s = _silu(c_ref[...])
    o_ref[0] = jnp.dot(s, w_ref[0], preferred_element_type=F32, precision=HIGHEST) + b_ref[0]


def _adaln(c, c_ctx, ada_w, ada_b):
    bsz, depth, n6 = c.shape[0], ada_w.shape[0], ada_w.shape[2]
    rows = 8
    assert bsz + 1 <= rows
    cc = jnp.concatenate([c, c_ctx[None], jnp.zeros((rows - bsz - 1, D_MODEL), F32)], axis=0)
    tn = 512
    out = _call(
        _mod_kernel, name="adaln", grid=(depth, n6 // tn),
        in_specs=[pl.BlockSpec((rows, D_MODEL), lambda l, j: (0, 0)),
                  pl.BlockSpec((1, D_MODEL, tn), lambda l, j: (l, 0, j)),
                  pl.BlockSpec((1, 1, tn), lambda l, j: (l, 0, j))],
        out_specs=pl.BlockSpec((1, rows, tn), lambda l, j: (l, 0, j)),
        out_shape=jax.ShapeDtypeStruct((depth, rows, n6), F32))(cc, ada_w, ada_b.reshape(depth, 1, n6))
    mod = out.reshape(depth, rows, 6, D_MODEL)
    return mod[:, :bsz], jnp.broadcast_to(mod[:, bsz:bsz + 1], (depth, bsz, 6, D_MODEL))


def _inproj_kernel(x_ref, mod_ref, g_ref, w_ref, s5_ref, hy_ref, ret_ref, mla_ref):
    h = _rms(x_ref[0], g_ref[...]) * (1.0 + mod_ref[0, 1:2, :]) + mod_ref[0, 0:1, :]
    hb = h.astype(BF16)
    s5_ref[0] = _dot(hb, w_ref[:, 0:256]).astype(BF16)
    hy_ref[0] = _dot(hb, w_ref[:, 256:1024]).astype(BF16)
    ret_ref[0] = _dot(hb, w_ref[:, 1024:2048]).astype(BF16)
    mla_ref[0] = _dot(hb, w_ref[:, 2048:2560]).astype(BF16)


def _inproj(x, mod, g, w):
    bsz, n, _ = x.shape
    tm = min(512, n)
    row = lambda width: pl.BlockSpec((1, tm, width), lambda b, i: (b, i, 0))
    shp = lambda width: jax.ShapeDtypeStruct((bsz, n, width), BF16)
    return _call(
        _inproj_kernel, name="inproj", grid=(bsz, n // tm),
        in_specs=[row(D_MODEL), pl.BlockSpec((1, 6, D_MODEL), lambda b, i: (b, 0, 0)),
                  pl.BlockSpec((1, D_MODEL), lambda b, i: (0, 0)),
                  pl.BlockSpec((D_MODEL, N_IN), lambda b, i: (0, 0))],
        out_specs=[row(256), row(768), row(1024), row(512)],
        out_shape=[shp(256), shp(768), shp(1024), shp(512)])(x, mod, g, w)


def _perm_w_in(w):
    z = lambda k: jnp.zeros((D_MODEL, k), w.dtype)
    return jnp.concatenate([w[:, :2048], w[:, 2048:2240], z(64), w[:, 2240:2368], w[:, 2368:2400], z(96)],
                           axis=1).astype(BF16)


def _gelu_tanh(x):
    return 0.5 * x * (1.0 + jnp.tanh(math.sqrt(2.0 / math.pi) * (x + 0.044715 * (x * x * x))))


def _s5_kernel(*refs, rev, tm, final):
    if final:
        (u_ref, bblk_ref, cblk_ref, tab_ref, tri_ref, h0_ref, yprev_ref, d_ref, gw_ref, gb_ref,
         y_ref, hfin_ref, hs_scr, carry_scr) = refs
    else:
        u_ref, bblk_ref, cblk_ref, tab_ref, tri_ref, h0_ref, y_ref, hfin_ref, hs_scr, carry_scr = refs
    nc = S5_COLS

    @pl.when(pl.program_id(1) == 0)
    def _():
        carry_scr[...] = h0_ref[0]

    car_r, car_i = tab_ref[4], tab_ref[5]
    nsub = S5_GRP // S5_SUB

    def cmul(t, xr, xi):
        tr, ti = tab_ref[t][None], tab_ref[t + 1][None]
        xr3, xi3 = xr.reshape(nsub, S5_SUB, nc), xi.reshape(nsub, S5_SUB, nc)
        return ((tr * xr3 - ti * xi3).reshape(S5_GRP, nc), (tr * xi3 + ti * xr3).reshape(S5_GRP, nc))

    order = list(range(tm // S5_GRP))
    subs = list(range(S5_GRP // S5_SUB))
    if rev:
        order, subs = order[::-1], subs[::-1]
    rows = lambda g: slice(g * S5_GRP, (g + 1) * S5_GRP)

    def project(g):
        return _dot(u_ref[0, rows(g), :], bblk_ref[...])

    def cumsum(bu):
        br, bi = bu[:, 0:nc], bu[:, nc:]
        gr, gi = cmul(0, br, bi)
        return _dot(tri_ref[...], jnp.concatenate([gr, gi], axis=1).astype(BF16))

    def scan(g, cs, cr, ci):
        csr, csi = cs[:, 0:nc], cs[:, nc:]
        wr, wi = cmul(2, csr, csi)
        r0 = g * S5_GRP
        for s in subs:
            a0 = s * S5_SUB
            hr = wr[a0:a0 + S5_SUB] + car_r * cr - car_i * ci
            hi = wi[a0:a0 + S5_SUB] + car_r * ci + car_i * cr
            hs_scr[r0 + a0:r0 + a0 + S5_SUB, 0:nc] = hr.astype(BF16)
            hs_scr[r0 + a0:r0 + a0 + S5_SUB, nc:] = hi.astype(BF16)
            edge = 0 if rev else S5_SUB - 1
            cr, ci = hr[edge:edge + 1], hi[edge:edge + 1]
        return cr, ci

    def readout(g):
        y = _dot(hs_scr[rows(g), :], cblk_ref[...])
        if final:
            yt = _gelu_tanh(yprev_ref[0, rows(g), :] + y + d_ref[...] * u_ref[0, rows(g), :].astype(F32))
            z = _dot(yt.astype(BF16), gw_ref[...]) + gb_ref[...]
            y_ref[0, rows(g), :] = (yt * jax.nn.sigmoid(z)).astype(y_ref.dtype)
        else:
            y_ref[0, rows(g), :] = y

    cr, ci = carry_scr[:, 0:nc], carry_scr[:, nc:]
    ng = len(order)
    bus = {0: project(order[0])}
    if ng > 1:
        bus[1] = project(order[1])
    css = {0: cumsum(bus.pop(0))}
    for i in range(ng):
        if i + 2 < ng:
            bus[i + 2] = project(order[i + 2])
        if i + 1 < ng:
            css[i + 1] = cumsum(bus.pop(i + 1))
        cr, ci = scan(order[i], css.pop(i), cr, ci)
        readout(order[i])
    carry = jnp.concatenate([cr, ci], axis=1)
    carry_scr[...] = carry
    hfin_ref[0] = carry


def _s5_dir(u, bblk, cblk, tab, tri, h0, rev, final=None):
    bsz, n, _ = u.shape
    tm = min(512, n)
    nb = n // tm
    blk = (lambda i: nb - 1 - i) if rev else (lambda i: i)
    row = pl.BlockSpec((1, tm, W_GROUP), lambda b, i: (b, blk(i), 0))
    full = lambda a: pl.BlockSpec(a.shape, lambda b, i: (0,) * a.ndim)
    state = pl.BlockSpec((1, 1, 2 * S5_COLS), lambda b, i: (b, 0, 0))
    ins = [u, bblk, cblk, tab, tri, h0]
    in_specs = [row, full(bblk), full(cblk), full(tab), full(tri), state]
    if final is not None:
        ins += list(final)
        in_specs += [row] + [full(a) for a in final[1:]]
    return _call(
        functools.partial(_s5_kernel, rev=rev, tm=tm, final=final is not None),
        name="s5_bwd" if rev else "s5_fwd", grid=(bsz, nb), in_specs=in_specs,
        out_specs=[row, state],
        out_shape=[jax.ShapeDtypeStruct((bsz, n, W_GROUP), BF16 if final is not None else F32),
                   jax.ShapeDtypeStruct((bsz, 1, 2 * S5_COLS), F32)],
        scratch_shapes=[pltpu.VMEM((tm, 2 * S5_COLS), BF16), pltpu.VMEM((1, 2 * S5_COLS), F32)])(*ins)


def _s5_weights(lam_re, lam_im, log_dt, b_re, b_im, c_re, c_im):
    g, p, ch = S5_GROUPS, S5_STATE, S5_CH
    dt = jnp.exp(log_dt)[..., None]
    re_dt = (lam_re * dt).reshape(2, 1, g * p)
    im_dt = (lam_im * dt).reshape(2, 1, g * p)
    lam = lax.complex(lam_re, lam_im)
    abar = jnp.exp(lax.complex(lam_re * dt, lam_im * dt))
    bbar = ((abar - 1.0) / lam)[..., None] * lax.complex(b_re, b_im)
    eye = jnp.eye(g, dtype=F32)
    place_b = lambda a: jnp.einsum('dgpc,gh->dgchp', a, eye).reshape(2, g * ch, g * p)
    bblk = jnp.concatenate([place_b(jnp.real(bbar)), place_b(jnp.imag(bbar))], axis=2).astype(BF16)
    place_c = lambda a: jnp.einsum('dgcp,gh->dgphc', a, eye).reshape(2, g * p, g * ch)
    cblk = jnp.concatenate([place_c(c_re), place_c(-c_im)], axis=1).astype(BF16)

    def powers(k, d):
        kk = k[:, None]
        mag = jnp.exp(kk * re_dt[d])
        return [mag * jnp.cos(kk * im_dt[d]), mag * jnp.sin(kk * im_dt[d])]

    i_sub = jnp.arange(S5_SUB, dtype=F32)
    tab_f = jnp.stack(powers(-i_sub, 0) + powers(i_sub, 0) + powers(i_sub + 1.0, 0))
    tab_b = jnp.stack(powers(i_sub, 1) + powers(-i_sub, 1) + powers(S5_SUB - i_sub, 1))
    ii = np.arange(S5_GRP)
    same = (ii[:, None] // S5_SUB) == (ii[None, :] // S5_SUB)
    tri_f = jnp.asarray(same & (ii[None, :] <= ii[:, None]), F32).astype(BF16)
    tri_b = jnp.asarray(same & (ii[None, :] >= ii[:, None]), F32).astype(BF16)
    return (bblk[0], cblk[0], tab_f, tri_f), (bblk[1], cblk[1], tab_b, tri_b)


def _s5_mixer(u_lat, u_ctx, prm, need_ctx):
    (lam_re, lam_im, log_dt, b_re, b_im, c_re, c_im, d_skip, glu_w, glu_b) = prm
    wf, wb = _s5_weights(lam_re, lam_im, log_dt, b_re, b_im, c_re, c_im)
    bsz = u_lat.shape[0]
    zero = jnp.zeros((bsz, 1, 2 * S5_COLS), F32)
    fin = lambda yprev: (yprev, d_skip.reshape(1, W_GROUP), glu_w.astype(BF16), glu_b.reshape(1, W_GROUP))
    yc_f, hc_f = _s5_dir(u_ctx, *wf, zero, rev=False)
    yc, hc_b = _s5_dir(u_ctx, *wb, zero, rev=True, final=fin(yc_f))
    yl_f, _ = _s5_dir(u_lat, *wf, hc_f, rev=False)
    yl, _ = _s5_dir(u_lat, *wb, hc_b, rev=True, final=fin(yl_f))
    return yl, (yc if need_ctx else None)


def _rot_half(x, half, period):
    lane = lax.broadcasted_iota(jnp.int32, x.shape, 1)
    width = x.shape[1]
    return jnp.where((lane % period) < half, pltpu.roll(x, width - half, axis=1), pltpu.roll(x, half, axis=1))


def _ret_kernel(*refs, rev, tm, rotate, final):
    refs = list(refs)
    q_ref, k_ref, v_ref = refs[:3]
    del refs[:3]
    if rotate:
        cos_ref, sin_ref = refs[:2]
        del refs[:2]
    lgt_ref, lgc_ref, s0_ref = refs[:3]
    del refs[:3]
    if final:
        of_ref, g_ref, gn_ref, avg_ref = refs[:4]
        del refs[:4]
    o_ref, sfin_ref, d_scr, xz_scr, s_scr = refs
    hd = RET_HEADS * RET_DK

    @pl.when((pl.program_id(0) == 0) & (pl.program_id(1) == 0))
    def _():
        ri = lax.broadcasted_iota(jnp.int32, (tm, tm), 0)
        ci = lax.broadcasted_iota(jnp.int32, (tm, tm), 1)
        diff = (ci - ri) if rev else (ri - ci)
        dpos = jnp.maximum(diff, 0).astype(F32)
        for h in range(RET_HEADS):
            d_scr[h] = jnp.where(diff >= 0, jnp.exp(dpos * lgt_ref[h]), 0.0)
        pos = lax.broadcasted_iota(jnp.int32, (tm, hd), 0).astype(F32)
        lgc = lgc_ref[...]
        xz_scr[0] = jnp.exp(((tm - pos) if rev else (pos + 1.0)) * lgc)
        xz_scr[1] = jnp.exp((pos if rev else (tm - 1.0 - pos)) * lgc)

    @pl.when(pl.program_id(1) == 0)
    def _():
        s_scr[...] = s0_ref[0]

    q = q_ref[0].astype(F32)
    k = k_ref[0].astype(F32) * (RET_DK ** -0.5)
    if rotate:
        cos, sin = cos_ref[...], sin_ref[...]
        q = q * cos + _rot_half(q, RET_DK // 2, RET_DK) * sin
        k = k * cos + _rot_half(k, RET_DK // 2, RET_DK) * sin
    qb, kb, v = q.astype(BF16), k.astype(BF16), v_ref[0]
    head = lax.broadcasted_iota(jnp.int32, (tm, hd), 1) // RET_DK
    o = _dot((q * xz_scr[0]).astype(BF16), s_scr[...].astype(BF16))
    for h in range(RET_HEADS):
        m = head == h
        s = _dot_nt(jnp.where(m, qb, jnp.zeros_like(qb)), kb)
        oh = _dot((s * d_scr[h]).astype(BF16), v)
        o = o + jnp.where(m, oh, 0.0)
    ds = _dot_tn(kb, (v.astype(F32) * xz_scr[1]).astype(BF16))
    rh = lax.broadcasted_iota(jnp.int32, (hd, hd), 0) // RET_DK
    ch = lax.broadcasted_iota(jnp.int32, (hd, hd), 1) // RET_DK
    s_new = s_scr[...] * jnp.exp(tm * lgc_ref[...]) + jnp.where(rh == ch, ds, 0.0)
    s_scr[...] = s_new
    sfin_ref[0] = s_new
    if final:
        o = o + of_ref[0]
        o2 = o * o
        hi = o2.astype(BF16)
        lo = (o2 - hi.astype(F32)).astype(BF16)
        ms = _dot(hi, avg_ref[...]) + _dot(lo, avg_ref[...])
        g = g_ref[0].astype(F32)
        o_ref[0] = (_silu(g) * (o * lax.rsqrt(ms + EPS) * gn_ref[...])).astype(o_ref.dtype)
    else:
        o_ref[0] = o


def _ret_dir(p, lg, s0, rev, rope=None, final=None):
    bsz, n, _ = p.shape
    tm = min(256, n)
    nb = n // tm
    hd = RET_HEADS * RET_DK
    blk = (lambda i: nb - 1 - i) if rev else (lambda i: i)
    col = lambda j: pl.BlockSpec((1, tm, hd), lambda b, i: (b, blk(i), j))
    full = lambda a: pl.BlockSpec(a.shape, lambda b, i: (0,) * a.ndim)
    state = pl.BlockSpec((1, hd, hd), lambda b, i: (b, 0, 0))
    lgt = jnp.broadcast_to(lg[:, None, None], (RET_HEADS, 1, tm))
    lgc = jnp.repeat(lg, RET_DK).reshape(1, hd)
    ins, in_specs = [p, p, p], [col(0), col(1), col(2)]
    if rope is not None:
        ins += list(rope)
        in_specs += [pl.BlockSpec((tm, hd), lambda b, i: (blk(i), 0))] * 2
    ins += [lgt, lgc, s0]
    in_specs += [full(lgt), full(lgc), state]
    if final is not None:
        of, gn = final
        avg = jnp.asarray(np.kron(np.eye(RET_HEADS), np.full((RET_DK, RET_DK), 1.0 / RET_DK)), F32).astype(BF16)
        ins += [of, p, gn.reshape(1, hd), avg]
        in_specs += [pl.BlockSpec((1, tm, hd), lambda b, i: (b, blk(i), 0)), col(3), full(gn.reshape(1, hd)), full(avg)]
    return _call(
        functools.partial(_ret_kernel, rev=rev, tm=tm, rotate=rope is not None, final=final is not None),
        name="ret_bwd" if rev else "ret_fwd", grid=(bsz, nb), in_specs=in_specs,
        out_specs=[pl.BlockSpec((1, tm, hd), lambda b, i: (b, blk(i), 0)), state],
        out_shape=[jax.ShapeDtypeStruct((bsz, n, hd), BF16 if final is not None else F32),
                   jax.ShapeDtypeStruct((bsz, hd, hd), F32)],
        scratch_shapes=[pltpu.VMEM((RET_HEADS, tm, tm), F32), pltpu.VMEM((2, tm, hd), F32),
                        pltpu.VMEM((hd, hd), F32)])(*ins)


def _ret_mixer(p_lat, p_ctx, prm, need_ctx):
    decay_exp, gn_g = prm
    lg = jnp.log1p(-jnp.exp2(-decay_exp))
    bsz, n, _ = p_lat.shape
    hd = RET_HEADS * RET_DK
    theta = RET_ROPE_BASE ** (-jnp.linspace(0.0, 1.0, RET_DK // 2, dtype=F32))
    ang = jnp.arange(n, dtype=F32)[:, None] * theta
    cos = jnp.tile(jnp.cos(ang), (1, 2 * RET_HEADS))
    sin = jnp.tile(jnp.concatenate([-jnp.sin(ang), jnp.sin(ang)], axis=1), (1, RET_HEADS))
    zero = jnp.zeros((bsz, hd, hd), F32)
    oc_f, sc_f = _ret_dir(p_ctx, lg[0], zero, rev=False)
    yc, sc_b = _ret_dir(p_ctx, lg[1], zero, rev=True, final=(oc_f, gn_g))
    ol_f, _ = _ret_dir(p_lat, lg[0], sc_f, rev=False, rope=(cos, sin))
    yl, _ = _ret_dir(p_lat, lg[1], sc_b, rev=True, rope=(cos, sin), final=(ol_f, gn_g))
    return yl, (yc if need_ctx else None)


def _mla_prep_kernel(*refs, rotate):
    if rotate:
        p_ref, gq_ref, gkv_ref, wq_ref, wkv_ref, e_ref, cos_ref, sin_ref, q_ref, k_ref, v_ref = refs
    else:
        p_ref, gq_ref, gkv_ref, wq_ref, wkv_ref, e_ref, q_ref, k_ref, v_ref = refs
    p = p_ref[0]
    cq = p[:, 0:256].astype(F32)
    cqn = cq * lax.rsqrt(jnp.sum(cq * cq, axis=-1, keepdims=True) * (1.0 / MLA_Q_RANK) + EPS) * gq_ref[...]
    q = _dot(cqn.astype(BF16), wq_ref[...])
    ckv = p[:, 256:384].astype(F32)
    kv = _dot(_rms(ckv, gkv_ref[...]).astype(BF16), wkv_ref[...])
    width = MLA_HEADS * MLA_HEAD_PAD
    k = kv[:, 0:width] + _dot(p[:, 384:512], e_ref[...])
    if rotate:
        cos, sin = cos_ref[...], sin_ref[...]
        lane = lax.broadcasted_iota(jnp.int32, q.shape, 1) % MLA_HEAD_PAD
        first = lane < MLA_NOPE + MLA_ROPE // 2
        rot = lambda x: jnp.where(first, pltpu.roll(x, width - MLA_ROPE // 2, axis=1),
                                  pltpu.roll(x, MLA_ROPE // 2, axis=1))
        q = q * cos + rot(q) * sin
        k = k * cos + rot(k) * sin
    q_ref[0] = (q * (MLA_QK ** -0.5 * math.log2(math.e))).astype(BF16)
    k_ref[0] = k.astype(BF16)
    v_ref[0] = kv[:, width:].astype(BF16)


def _mla_prep(p, wts, rope=None):
    bsz, n, _ = p.shape
    tm = min(512, n)
    width = MLA_HEADS * MLA_HEAD_PAD
    full = lambda a: pl.BlockSpec(a.shape, lambda b, i: (0,) * a.ndim)
    row = lambda w: pl.BlockSpec((1, tm, w), lambda b, i: (b, i, 0))
    ins = [p] + list(wts)
    in_specs = [row(512)] + [full(a) for a in wts]
    if rope is not None:
        ins += list(rope)
        in_specs += [pl.BlockSpec((tm, width), lambda b, i: (i, 0))] * 2
    return _call(
        functools.partial(_mla_prep_kernel, rotate=rope is not None), name="mla_prep", grid=(bsz, n // tm),
        in_specs=in_specs, out_specs=[row(width), row(width), row(MLA_HEADS * MLA_V)],
        out_shape=[jax.ShapeDtypeStruct((bsz, n, width), BF16), jax.ShapeDtypeStruct((bsz, n, width), BF16),
                   jax.ShapeDtypeStruct((bsz, n, MLA_HEADS * MLA_V), BF16)])(*ins)


def _kv_blocks(nk, target=1152):
    assert nk % 128 == 0
    n128 = nk // 128
    nblk = max(1, -(-nk // target))
    sizes = [(n128 // nblk + (1 if j < n128 % nblk else 0)) * 128 for j in range(nblk)]
    starts = [sum(sizes[:j]) for j in range(nblk)]
    return list(zip(starts, sizes))


def _attn_kernel(q_ref, k_ref, v_ref, o_ref, *, blocks):
    q = q_ref[0]
    tq = q.shape[0]
    head = lax.broadcasted_iota(jnp.int32, (tq, MLA_HEADS * MLA_V), 1) // MLA_V
    units = [(j, h) for j in range(len(blocks)) for h in range(MLA_HEADS)]

    def scores(j, h):
        k0, kn = blocks[j]
        c0 = h * MLA_HEAD_PAD
        return _dot_nt(q[:, c0:c0 + MLA_HEAD_PAD], k_ref[0, k0:k0 + kn, c0:c0 + MLA_HEAD_PAD])

    m = [jnp.full((tq, 1), -jnp.inf, F32)] * MLA_HEADS
    l = [jnp.zeros((tq, 1), F32)] * MLA_HEADS
    acc = jnp.zeros((tq, MLA_HEADS * MLA_V), F32)
    s_next = scores(*units[0])
    for idx, (j, h) in enumerate(units):
        s = s_next
        if idx + 1 < len(units):
            s_next = scores(*units[idx + 1])
        k0, kn = blocks[j]
        m_new = jnp.maximum(m[h], jnp.max(s, axis=-1, keepdims=True))
        e = jnp.exp2(s - m_new)
        alpha = jnp.exp2(m[h] - m_new)
        l[h] = alpha * l[h] + jnp.sum(e, axis=-1, keepdims=True)
        m[h] = m_new
        pv = _dot(e.astype(BF16), v_ref[0, k0:k0 + kn, :])
        acc = jnp.where(head == h, alpha * acc + pv, acc)
    inv = jnp.zeros((tq, MLA_HEADS * MLA_V), F32)
    for h in range(MLA_HEADS):
        inv = jnp.where(head == h, 1.0 / l[h], inv)
    o_ref[0] = (acc * inv).astype(o_ref.dtype)


def _attention(q, k, v):
    bsz, n, width = q.shape
    nk = k.shape[1]
    tq = min(512, n)
    return _call(
        functools.partial(_attn_kernel, blocks=_kv_blocks(nk)), name="mla_attn", grid=(bsz, n // tq),
        in_specs=[pl.BlockSpec((1, tq, width), lambda b, i: (b, i, 0)),
                  pl.BlockSpec((1, nk, width), lambda b, i: (b, 0, 0)),
                  pl.BlockSpec((1, nk, MLA_HEADS * MLA_V), lambda b, i: (b, 0, 0))],
        out_specs=pl.BlockSpec((1, tq, MLA_HEADS * MLA_V), lambda b, i: (b, i, 0)),
        out_shape=jax.ShapeDtypeStruct((bsz, n, MLA_HEADS * MLA_V), BF16))(q, k, v)


def _mla_mixer(p_lat, p_ctx, prm, need_ctx):
    q_norm_g, kv_norm_g, w_uq, w_ukv = prm
    n = p_lat.shape[1]
    width = MLA_HEADS * MLA_HEAD_PAD
    wq = jnp.pad(w_uq.reshape(MLA_Q_RANK, MLA_HEADS, MLA_QK),
                 ((0, 256 - MLA_Q_RANK), (0, 0), (0, MLA_HEAD_PAD - MLA_QK))).reshape(256, width).astype(BF16)
    wkv = w_ukv.reshape(MLA_KV_RANK, MLA_HEADS, MLA_NOPE + MLA_V)
    wk = jnp.pad(wkv[:, :, :MLA_NOPE], ((0, 0), (0, 0), (0, MLA_HEAD_PAD - MLA_NOPE))).reshape(MLA_KV_RANK, width)
    wv = wkv[:, :, MLA_NOPE:].reshape(MLA_KV_RANK, MLA_HEADS * MLA_V)
    wkv = jnp.concatenate([wk, wv], axis=1).astype(BF16)
    place = np.zeros((128, width), np.float32)
    for h in range(MLA_HEADS):
        for j in range(MLA_ROPE):
            place[j, h * MLA_HEAD_PAD + MLA_NOPE + j] = 1.0
    place = jnp.asarray(place).astype(BF16)
    gq = jnp.pad(q_norm_g, (0, 256 - MLA_Q_RANK)).reshape(1, 256)
    wts = (gq, kv_norm_g.reshape(1, MLA_KV_RANK), wq, wkv, place)
    pos = jnp.arange(n)
    row, colp = (pos // GRID_W).astype(F32), (pos % GRID_W).astype(F32)
    n_freq = MLA_ROPE // 4
    inv = ROPE_BASE ** (-jnp.arange(n_freq, dtype=F32) / n_freq)
    ang = jnp.concatenate([row[:, None] * inv, colp[:, None] * inv], axis=-1)
    ones, zeros = jnp.ones((n, MLA_NOPE), F32), jnp.zeros((n, MLA_NOPE), F32)
    tail = MLA_HEAD_PAD - MLA_QK
    cos = jnp.concatenate([ones, jnp.cos(ang), jnp.cos(ang), jnp.ones((n, tail), F32)], axis=1)
    sin = jnp.concatenate([zeros, -jnp.sin(ang), jnp.sin(ang), jnp.zeros((n, tail), F32)], axis=1)
    rope = (jnp.tile(cos, (1, MLA_HEADS)), jnp.tile(sin, (1, MLA_HEADS)))
    q_c, k_c, v_c = _mla_prep(p_ctx, wts)
    q_l, k_l, v_l = _mla_prep(p_lat, wts, rope)
    y_lat = _attention(q_l, jnp.concatenate([k_c, k_l], axis=1), jnp.concatenate([v_c, v_l], axis=1))
    y_ctx = _attention(q_c, k_c, v_c) if need_ctx else None
    return y_lat, y_ctx


def _hy_filter_kernel(z_ref, zt_ref, w1t_ref, b1_ref, w2t_ref, b2_ref, w3_ref, fr_ref, dl_ref, k_ref, hid_scr):
    dotp = lambda a, b: jnp.dot(a, b, preferred_element_type=F32, precision=HIGHEST)

    @pl.when(pl.program_id(0) == 0)
    def _():
        fr = fr_ref[...]
        hid = jnp.sin(fr * (dotp(w1t_ref[...], zt_ref[...]) + b1_ref[...]))
        hid = jnp.sin(fr * (dotp(w2t_ref[...], hid) + b2_ref[...]))
        hid_scr[...] = hid.T

    k = dotp(hid_scr[...], w3_ref[...]) * jnp.exp(-z_ref[:, 0:1] * jnp.abs(dl_ref[...]))
    ss = jnp.sum(k * k, axis=0, keepdims=True)
    r = lax.rsqrt(ss[:, 0:W_GROUP] + ss[:, W_GROUP:] + EPS)
    k_ref[...] = (k * jnp.concatenate([r, r], axis=1)).astype(k_ref.dtype)


def _hy_filter(n, prm):
    w1, b1, w2, b2, w3, freq, deltas = prm
    pos = jnp.arange(n, dtype=F32)
    t01 = pos / (n - 1)
    bands = jnp.linspace(1e-4, HY_BANDS - 1, HY_BANDS, dtype=F32)
    ang = (2.0 * math.pi / n) * pos[:, None] * bands[None, :]
    z = jnp.concatenate([t01[:, None], jnp.cos(ang), -jnp.sin(ang)], axis=-1)
    emb = z.shape[1]
    z = jnp.pad(z, ((0, 0), (0, 128 - emb)))
    w1p = jnp.pad(w1, ((0, 128 - emb), (0, 0)))
    col = lambda a: a.reshape(-1, 1)
    ins = [z, z.T, w1p.T, col(b1), w2.T, col(b2), w3, col(freq), deltas.reshape(1, -1)]
    full = lambda a: pl.BlockSpec(a.shape, lambda o: (0,) * a.ndim)
    ncol = w3.shape[1]
    cw = 2 * W_GROUP
    in_specs = [full(a) for a in ins]
    in_specs[6] = pl.BlockSpec((HY_HIDDEN, cw), lambda o: (0, o))
    in_specs[8] = pl.BlockSpec((1, cw), lambda o: (0, o))
    return _call(_hy_filter_kernel, name="hy_filter", grid=(ncol // cw,), in_specs=in_specs,
                 out_specs=pl.BlockSpec((n, cw), lambda o: (0, o)),
                 out_shape=jax.ShapeDtypeStruct((n, ncol), BF16),
                 scratch_shapes=[pltpu.VMEM((n, HY_HIDDEN), F32)])(*ins)


def _dwconv_rows(x, w_ref, b_ref):
    n = x.shape[0]
    r = lax.broadcasted_iota(jnp.int32, x.shape, 0)
    prev = jnp.where(r == 0, 0.0, pltpu.roll(x, 1, axis=0))
    nxt = jnp.where(r == n - 1, 0.0, pltpu.roll(x, n - 1, axis=0))
    return prev * w_ref[0:1, :] + x * w_ref[1:2, :] + nxt * w_ref[2:3, :] + b_ref[...]


def _fft_consts(n_tok):
    n = 2 * n_tok
    n2n = FFT_N2
    n1n = n // n2n
    k1 = np.arange(n1n)[None, :, None]
    n1 = np.arange(n1n // 2)[None, None, :]
    n2 = np.arange(n2n)[:, None, None]
    th = 2.0 * np.pi * ((k1 * (n2n * n1 + n2)) % n) / n
    fa = np.concatenate([np.cos(th), -np.sin(th)], axis=1)
    fat = np.transpose(fa, (0, 2, 1)) / n
    a = np.arange(n2n)
    t2 = 2.0 * np.pi * ((a[:, None] * a[None, :]) % n2n) / n2n
    fr, fi = np.cos(t2), -np.sin(t2)
    f2 = np.block([[fr, -fi], [fi, fr]])
    f2i = np.block([[fr, fi], [-fi, fr]])
    as_bf16 = lambda m: jnp.asarray(m, F32).astype(BF16)
    return as_bf16(fa), as_bf16(fat), as_bf16(f2), as_bf16(f2i)


def _fft_a_kernel(x_ref, fa_ref, o_ref, *, nb, cw):
    for j in range(nb):
        o_ref[:, j * cw:(j + 1) * cw] = _dot(fa_ref[j], x_ref[:, j * cw:(j + 1) * cw]).astype(o_ref.dtype)


def _fft_a(x, fa, ctot):
    h1 = x.shape[0]
    nb = 4
    return _call(
        functools.partial(_fft_a_kernel, nb=nb, cw=ctot), name="hy_fft_a", grid=(FFT_N2 // nb,),
        in_specs=[pl.BlockSpec((h1, nb * ctot), lambda j: (0, j)), pl.BlockSpec((nb, 4 * h1, h1), lambda j: (j, 0, 0))],
        out_specs=pl.BlockSpec((4 * h1, nb * ctot), lambda j: (0, j)),
        out_shape=jax.ShapeDtypeStruct((4 * h1, FFT_N2 * ctot), BF16))(x, fa)


def _fft_spec_kernel(a_ref, k0_ref, f2_ref, h_ref, *, kb):
    c = W_GROUP
    for j in range(kb):
        for o in range(HY_ORDER):
            af = jnp.concatenate([a_ref[0, 0, j, :, 2 * o * c:(2 * o + 1) * c],
                                  a_ref[0, 1, j, :, 2 * o * c:(2 * o + 1) * c]], axis=0)
            ab = jnp.concatenate([a_ref[0, 0, j, :, (2 * o + 1) * c:(2 * o + 2) * c],
                                  a_ref[0, 1, j, :, (2 * o + 1) * c:(2 * o + 2) * c]], axis=0)
            xf, xb = _dot(f2_ref[...], af), _dot(f2_ref[...], ab)
            k0 = k0_ref[:, (2 * o + 1) * c:(2 * o + 2) * c].astype(F32)
            h_ref[o, j, 0:FFT_N2, :] = (xf[0:FFT_N2] + xb[0:FFT_N2] - k0).astype(h_ref.dtype)
            h_ref[o, j, FFT_N2:, :] = (xf[FFT_N2:] - xb[FFT_N2:]).astype(h_ref.dtype)


def _fft_spec(a5, k0, f2):
    n1n = a5.shape[2]
    kb = 8
    ctot = a5.shape[4]
    return _call(
        functools.partial(_fft_spec_kernel, kb=kb), name="hy_fft_spec", grid=(n1n // kb,),
        in_specs=[pl.BlockSpec((1, 2, kb, FFT_N2, ctot), lambda i: (0, 0, i, 0, 0)),
                  pl.BlockSpec((1, ctot), lambda i: (0, 0)), pl.BlockSpec(f2.shape, lambda i: (0, 0))],
        out_specs=pl.BlockSpec((HY_ORDER, kb, 2 * FFT_N2, W_GROUP), lambda i: (0, i, 0, 0)),
        out_shape=jax.ShapeDtypeStruct((HY_ORDER, n1n, 2 * FFT_N2, W_GROUP), BF16))(a5, k0, f2)


HY_SLAB = 128
PITCH_PAD = 8
HY_UNROLL = 8


def _hy_fused_kernel(x1_ref, x2_ref, v_ref, w1_ref, w2_ref, wv_ref, b1_ref, b2_ref, bv_ref, spec_ref,
                     fa_ref, fat_ref, f2_ref, f2i_ref, bias_ref, o_ref, x_scr, a_scr, z_scr, y_scr, g_scr, *, n):
    n2n, h1 = FFT_N2, n // FFT_N2
    n1n = 2 * h1
    px, pa, pz = n2n + PITCH_PAD, 2 * n1n + PITCH_PAD, 2 * n2n + PITCH_PAD
    al = lambda i: pl.multiple_of(i, 8)

    g_scr[0] = _dwconv_rows(x1_ref[0].astype(F32), w1_ref, b1_ref).astype(BF16)
    g_scr[1] = _dwconv_rows(x2_ref[0].astype(F32), w2_ref, b2_ref).astype(BF16)
    u = _dwconv_rows(v_ref[0].astype(F32), wv_ref, bv_ref)
    for i in range(h1):
        x_scr[i * px:i * px + n2n, :] = u[i * n2n:(i + 1) * n2n]

    for o in range(HY_ORDER):
        def stage_a(t, carry):
            for jj in range(HY_UNROLL):
                n2 = t * HY_UNROLL + jj
                xs = x_scr[pl.ds(n2, h1, stride=px), :].astype(BF16)
                a_scr[pl.ds(al(n2 * pa), 2 * n1n), :] = _dot(fa_ref[n2], xs)
            return carry

        def stage_c(t, carry):
            for jj in range(2 * HY_UNROLL):
                k1 = t * (2 * HY_UNROLL) + jj
                a = jnp.concatenate([a_scr[pl.ds(k1, n2n, stride=pa), :],
                                     a_scr[pl.ds(n1n + k1, n2n, stride=pa), :]], axis=0).astype(BF16)
                x = _dot(f2_ref[...], a)
                xr, xi = x[0:n2n], x[n2n:]
                hr = spec_ref[o, k1, 0:n2n, :].astype(F32)
                hi = spec_ref[o, k1, n2n:, :].astype(F32)
                y = jnp.concatenate([xr * hr - xi * hi, xr * hi + xi * hr], axis=0).astype(BF16)
                z_scr[pl.ds(al(k1 * pz), 2 * n2n), :] = _dot(f2i_ref[...], y)
            return carry

        def stage_a_inv(t, carry):
            for jj in range(HY_UNROLL):
                n2 = t * HY_UNROLL + jj
                z = jnp.concatenate([z_scr[pl.ds(n2, n1n, stride=pz), :],
                                     z_scr[pl.ds(n2n + n2, n1n, stride=pz), :]], axis=0).astype(BF16)
                y_scr[pl.ds(al(n2 * px), h1), :] = _dot(fat_ref[n2], z)
            return carry

        def gate(t, carry):
            for jj in range(HY_UNROLL):
                n1 = t * HY_UNROLL + jj
                conv = y_scr[pl.ds(n1, n2n, stride=px), :]
                r0 = al(n1 * px)
                g = g_scr[o, pl.ds(pl.multiple_of(n1 * n2n, n2n), n2n), :].astype(F32)
                zn = g * (conv + x_scr[pl.ds(r0, n2n), :] * bias_ref[o:o + 1, :])
                if o + 1 < HY_ORDER:
                    x_scr[pl.ds(r0, n2n), :] = zn
                else:
                    o_ref[0, pl.ds(pl.multiple_of(n1 * n2n, n2n), n2n), :] = zn.astype(o_ref.dtype)
            return carry

        lax.fori_loop(0, n2n // HY_UNROLL, stage_a, 0)
        lax.fori_loop(0, n1n // (2 * HY_UNROLL), stage_c, 0)
        lax.fori_loop(0, n2n // HY_UNROLL, stage_a_inv, 0)
        lax.fori_loop(0, h1 // HY_UNROLL, gate, 0)


def _hyena_lat(p, prm):
    conv_w, conv_b, w1, b1, w2, b2, w3, freq, deltas, bias = prm
    bsz, n, ch = p.shape
    c = W_GROUP
    h1 = n // FFT_N2
    n1n = 2 * h1
    fa, fat, f2, f2i = _fft_consts(n)
    kf = _hy_filter(n, (w1, b1, w2, b2, w3, freq, deltas))
    ak = _fft_a(kf.reshape(h1, FFT_N2 * 4 * c), fa, 4 * c)
    spec = _fft_spec(ak.reshape(1, 2, n1n, FFT_N2, 4 * c), kf[0:1], f2)
    ns = c // HY_SLAB
    px, pa, pz = FFT_N2 + PITCH_PAD, 2 * n1n + PITCH_PAD, 2 * FFT_N2 + PITCH_PAD
    once = pl.Buffered(1)
    part = lambda a, k, rows: pl.BlockSpec((rows, HY_SLAB), lambda s, b: (0, k * ns + s))
    data = lambda k: pl.BlockSpec((1, n, HY_SLAB), lambda s, b: (b, 0, k * ns + s))
    full = lambda a: pl.BlockSpec(a.shape, lambda s, b: (0,) * a.ndim, pipeline_mode=once)
    cb = conv_b.reshape(1, ch)
    return _call(
        functools.partial(_hy_fused_kernel, n=n), name="hy_fused", grid=(ns, bsz),
        in_specs=[data(0), data(1), data(2), part(conv_w, 0, 3), part(conv_w, 1, 3), part(conv_w, 2, 3),
                  part(cb, 0, 1), part(cb, 1, 1), part(cb, 2, 1),
                  pl.BlockSpec((HY_ORDER, n1n, 2 * FFT_N2, HY_SLAB), lambda s, b: (0, 0, 0, s), pipeline_mode=once),
                  full(fa), full(fat), full(f2), full(f2i),
                  pl.BlockSpec((HY_ORDER, HY_SLAB), lambda s, b: (0, s))],
        out_specs=pl.BlockSpec((1, n, HY_SLAB), lambda s, b: (b, 0, s)),
        out_shape=jax.ShapeDtypeStruct((bsz, n, c), BF16),
        scratch_shapes=[pltpu.VMEM((h1 * px, HY_SLAB), F32), pltpu.VMEM((FFT_N2 * pa, HY_SLAB), F32),
                        pltpu.VMEM((n1n * pz, HY_SLAB), F32), pltpu.VMEM((FFT_N2 * px, HY_SLAB), F32),
                        pltpu.VMEM((HY_ORDER, n, HY_SLAB), BF16)])(
        p, p, p, conv_w, conv_w, conv_w, cb, cb, cb, spec, fa, fat, f2, f2i, bias)


def _hy_ctx_kernel(p_ref, cw_ref, cb_ref, k_ref, fd_ref, fdi_ref, bias_ref, o_ref):
    c = W_GROUP
    u = _dwconv_rows(p_ref[0].astype(F32), cw_ref, cb_ref)
    nf = fd_ref.shape[0] // 2
    xk = _dot(fd_ref[...], k_ref[...])
    z = u[:, 2 * c:]
    for o, gate in enumerate((u[:, 0:c], u[:, c:2 * c])):
        kf, kb = xk[:, 2 * o * c:(2 * o + 1) * c], xk[:, (2 * o + 1) * c:(2 * o + 2) * c]
        k0 = k_ref[0:1, (2 * o + 1) * c:(2 * o + 2) * c].astype(F32)
        hr = kf[0:nf] + kb[0:nf] - k0
        hi = kf[nf:] - kb[nf:]
        x = _dot(fd_ref[...], z.astype(BF16))
        xr, xi = x[0:nf], x[nf:]
        y = jnp.concatenate([xr * hr - xi * hi, xr * hi + xi * hr], axis=0).astype(BF16)
        z = gate * (_dot(fdi_ref[...], y) + z * bias_ref[o:o + 1, :])
    o_ref[0] = z.astype(o_ref.dtype)


def _hyena_ctx(p, prm):
    conv_w, conv_b, w1, b1, w2, b2, w3, freq, deltas, bias = prm
    bsz, n, ch = p.shape
    kf = _hy_filter(n, (w1, b1, w2, b2, w3, freq, deltas))
    nn = 2 * n
    th = 2.0 * np.pi * ((np.arange(nn)[:, None] * np.arange(n)[None, :]) % nn) / nn
    fd = np.concatenate([np.cos(th), -np.sin(th)], axis=0)
    fdj = jnp.asarray(fd, F32).astype(BF16)
    fdi = jnp.asarray(fd.T / nn, F32).astype(BF16)
    ins = [p, conv_w, conv_b.reshape(1, ch), kf, fdj, fdi, bias]
    full = lambda a: pl.BlockSpec(a.shape, lambda b: (0,) * a.ndim)
    return _call(
        _hy_ctx_kernel, name="hy_ctx", grid=(bsz,),
        in_specs=[pl.BlockSpec((1, n, ch), lambda b: (b, 0, 0))] + [full(a) for a in ins[1:]],
        out_specs=pl.BlockSpec((1, n, W_GROUP), lambda b: (b, 0, 0)),
        out_shape=jax.ShapeDtypeStruct((bsz, n, W_GROUP), BF16))(*ins)


def _outproj_kernel(y0_ref, y1_ref, y2_ref, y3_ref, w_ref, x_ref, mod_ref, g_ref, o_ref):
    c = W_GROUP
    y = (_dot(y0_ref[0], w_ref[0:c, :]) + _dot(y1_ref[0], w_ref[c:2 * c, :])
         + _dot(y2_ref[0], w_ref[2 * c:3 * c, :]) + _dot(y3_ref[0], w_ref[3 * c:, :]))
    o_ref[0] = x_ref[0] + mod_ref[0, 2:3, :] * _rms(y, g_ref[...])


def _outproj(ys, w, x, mod, g):
    bsz, n, _ = x.shape
    tm = min(512, n)
    row = lambda width: pl.BlockSpec((1, tm, width), lambda b, i: (b, i, 0))
    return _call(
        _outproj_kernel, name="outproj", grid=(bsz, n // tm),
        in_specs=[row(W_GROUP)] * 4 + [pl.BlockSpec(w.shape, lambda b, i: (0, 0)), row(D_MODEL),
                                       pl.BlockSpec((1, 6, D_MODEL), lambda b, i: (b, 0, 0)),
                                       pl.BlockSpec((1, D_MODEL), lambda b, i: (0, 0))],
        out_specs=row(D_MODEL), out_shape=jax.ShapeDtypeStruct(x.shape, F32))(*ys, w, x, mod, g)


def _ffn_kernel(xp_ref, x_ref, xn_ref, mod_ref, g2_ref, g3_ref, wup_ref, cw_ref, cb_ref, wdn_ref, o_ref,
                h_scr, u_scr, acc_scr, *, tm, nchunk):
    i, nb = pl.program_id(1), pl.num_programs(1)
    shift, scale = mod_ref[0, 3:4, :], mod_ref[0, 4:5, :]
    pre = lambda x: _rms(x, g2_ref[...]) * (1.0 + scale) + shift
    h_scr[0:8, :] = (pre(xp_ref[0]) * (i > 0).astype(F32)).astype(BF16)
    h_scr[8:8 + tm, :] = pre(x_ref[0]).astype(BF16)
    h_scr[8 + tm:, :] = (pre(xn_ref[0]) * (i < nb - 1).astype(F32)).astype(BF16)
    u_scr[0] = _dot(h_scr[...], wup_ref[0])
    for c in range(nchunk):
        ub = u_scr.at[c % 2]
        if c + 1 < nchunk:
            u_scr[(c + 1) % 2] = _dot(h_scr[...], wup_ref[c + 1])
        cw = cw_ref[c]
        u = (ub[7:7 + tm, :] * cw[0:1, :] + ub[8:8 + tm, :] * cw[1:2, :]
             + ub[9:9 + tm, :] * cw[2:3, :] + cb_ref[c])
        act = (_silu(u[:, 0:FF_CHUNK]) * u[:, FF_CHUNK:]).astype(BF16)
        if c == 0:
            acc_scr[...] = _dot(act, wdn_ref[c])
        else:
            acc_scr[...] += _dot(act, wdn_ref[c])
    o_ref[0] = x_ref[0] + mod_ref[0, 5:6, :] * _rms(acc_scr[...], g3_ref[...])


def _ffn(x, mod, g2, g3, w_up, conv_w, conv_b, w_down):
    bsz, n, _ = x.shape
    tm = min(256, n)
    nb = n // tm
    nchunk = D_FF // FF_CHUNK
    inter = lambda a: jnp.concatenate([a[..., :D_FF].reshape(a.shape[:-1] + (nchunk, FF_CHUNK)),
                                       a[..., D_FF:].reshape(a.shape[:-1] + (nchunk, FF_CHUNK))], axis=-1)
    wup = jnp.transpose(inter(w_up), (1, 0, 2)).astype(BF16)
    cw = jnp.transpose(inter(conv_w), (1, 0, 2))
    cb = inter(conv_b).reshape(nchunk, 1, 2 * FF_CHUNK)
    wdn = w_down.reshape(nchunk, FF_CHUNK, D_MODEL).astype(BF16)
    r8 = tm // 8
    last8 = n // 8 - 1
    full = lambda a: pl.BlockSpec(a.shape, lambda b, i: (0,) * a.ndim)
    vec = pl.BlockSpec((1, D_MODEL), lambda b, i: (0, 0))
    return _call(
        functools.partial(_ffn_kernel, tm=tm, nchunk=nchunk), name="convffn", grid=(bsz, nb),
        in_specs=[pl.BlockSpec((1, 8, D_MODEL), lambda b, i: (b, jnp.maximum(i * r8 - 1, 0), 0)),
                  pl.BlockSpec((1, tm, D_MODEL), lambda b, i: (b, i, 0)),
                  pl.BlockSpec((1, 8, D_MODEL), lambda b, i: (b, jnp.minimum((i + 1) * r8, last8), 0)),
                  pl.BlockSpec((1, 6, D_MODEL), lambda b, i: (b, 0, 0)), vec, vec,
                  full(wup), full(cw), full(cb), full(wdn)],
        out_specs=pl.BlockSpec((1, tm, D_MODEL), lambda b, i: (b, i, 0)),
        out_shape=jax.ShapeDtypeStruct(x.shape, F32),
        scratch_shapes=[pltpu.VMEM((tm + 16, D_MODEL), BF16), pltpu.VMEM((2, tm + 16, 2 * FF_CHUNK), F32),
                        pltpu.VMEM((tm, D_MODEL), F32)])(x, x, x, mod, g2, g3, wup, cw, cb, wdn)


def kernel(x, c, ctx, c_ctx, ada_w, ada_b, norm_g, w_in, w_out, s5_lam_re, s5_lam_im, s5_log_dt, s5_b_re, s5_b_im, s5_c_re, s5_c_im, s5_d, s5_glu_w, s5_glu_b, hy_conv_w, hy_conv_b, hy_w1, hy_b1, hy_w2, hy_b2, hy_w3, hy_freq, hy_deltas, hy_bias, ret_decay_exp, ret_gn_g, mla_q_norm_g, mla_kv_norm_g, mla_w_uq, mla_w_ukv, ffn_w_up, ffn_conv_w, ffn_conv_b, ffn_w_down):
    depth = ada_w.shape[0]
    mod_lat, mod_ctx = _adaln(c, c_ctx, ada_w, ada_b)
    for l in range(depth):
        need_ctx = l < depth - 1
        ml, mc = mod_lat[l], mod_ctx[l]
        g = lambda j: norm_g[l, j].reshape(1, D_MODEL)
        w_in_l = _perm_w_in(w_in[l])
        pl_s5, pl_hy, pl_ret, pl_mla = _inproj(x, ml, g(0), w_in_l)
        pc_s5, pc_hy, pc_ret, pc_mla = _inproj(ctx, mc, g(0), w_in_l)
        s5_p = (s5_lam_re[l], s5_lam_im[l], s5_log_dt[l], s5_b_re[l], s5_b_im[l], s5_c_re[l], s5_c_im[l],
                s5_d[l], s5_glu_w[l], s5_glu_b[l])
        hy_p = (hy_conv_w[l], hy_conv_b[l], hy_w1[l], hy_b1[l], hy_w2[l], hy_b2[l], hy_w3[l], hy_freq[l],
                hy_deltas[l], hy_bias[l])
        y_s5, yc_s5 = _s5_mixer(pl_s5, pc_s5, s5_p, need_ctx)
        y_hy = _hyena_lat(pl_hy, hy_p)
        y_ret, yc_ret = _ret_mixer(pl_ret, pc_ret, (ret_decay_exp[l], ret_gn_g[l]), need_ctx)
        y_mla, yc_mla = _mla_mixer(pl_mla, pc_mla, (mla_q_norm_g[l], mla_kv_norm_g[l], mla_w_uq[l], mla_w_ukv[l]),
                                   need_ctx)
        w_out_l = w_out[l].astype(BF16)
        ffn_p = (ffn_w_up[l], ffn_conv_w[l], ffn_conv_b[l], ffn_w_down[l])
        x = _outproj((y_s5, y_hy, y_ret, y_mla), w_out_l, x, ml, g(1))
        x = _ffn(x, ml, g(2), g(3), *ffn_p)
        if need_ctx:
            yc_hy = _hyena_ctx(pc_hy, hy_p)
            ctx = _outproj((yc_s5, yc_hy, yc_ret, yc_mla), w_out_l, ctx, mc, g(1))
            ctx = _ffn(ctx, mc, g(2), g(3), *ffn_p)
    return x
```

```python
import functools
import math

import numpy as np
import jax
import jax.numpy as jnp
from jax import lax
from jax.experimental import pallas as pl
from jax.experimental.pallas import tpu as pltpu

F32 = jnp.float32
BF16 = jnp.bfloat16
HIGHEST = lax.Precision.HIGHEST

EPS = 1e-6
D_MODEL = 1024
W_GROUP = 256
GRID_W = 64
S5_CH, S5_GROUPS, S5_STATE = 16, 16, 64
S5_COLS = S5_GROUPS * S5_STATE
S5_SUB = 16
S5_GRP = 128
HY_ORDER, HY_BANDS, HY_HIDDEN = 2, 16, 64
RET_HEADS, RET_DK = 4, 64
RET_ROPE_BASE = 10000.0
MLA_HEADS, MLA_NOPE, MLA_ROPE, MLA_V = 4, 64, 32, 64
MLA_Q_RANK, MLA_KV_RANK = 192, 128
MLA_QK = MLA_NOPE + MLA_ROPE
MLA_HEAD_PAD = 128
ROPE_BASE = 10000.0
D_FF = 2816
FF_CHUNK = 256
FF_AHEAD = 2
N_IN = 2560
FFT_N2 = 64

VMEM_LIMIT_BYTES = 56 * 1024 * 1024


def _call(kernel, *, name, grid, in_specs, out_specs, out_shape, scratch_shapes=()):
    return pl.pallas_call(
        kernel, name=name, grid=grid, in_specs=in_specs, out_specs=out_specs, out_shape=out_shape,
        scratch_shapes=scratch_shapes,
        compiler_params=pltpu.CompilerParams(dimension_semantics=("arbitrary",) * len(grid),
                                             vmem_limit_bytes=VMEM_LIMIT_BYTES))


def _dot(a, b):
    return jnp.dot(a, b, preferred_element_type=F32)


def _dot_nt(a, b):
    return lax.dot_general(a, b, (((1,), (1,)), ((), ())), preferred_element_type=F32)


def _dot_tn(a, b):
    return lax.dot_general(a, b, (((0,), (0,)), ((), ())), preferred_element_type=F32)


def _rms(x, g):
    return x * lax.rsqrt(jnp.mean(x * x, axis=-1, keepdims=True) + EPS) * g


def _silu(x):
    return x * jax.nn.sigmoid(x)


def _mod_kernel(c_ref, w_ref, b_ref, o_ref):
    s = _silu(c_ref[...])
    o_ref[0] = jnp.dot(s, w_ref[0], preferred_element_type=F32, precision=HIGHEST) + b_ref[0]


def _adaln(c, c_ctx, ada_w, ada_b):
    bsz, depth, n6 = c.shape[0], ada_w.shape[0], ada_w.shape[2]
    rows = 8
    assert bsz + 1 <= rows
    cc = jnp.concatenate([c, c_ctx[None], jnp.zeros((rows - bsz - 1, D_MODEL), F32)], axis=0)
    tn = 512
    out = _call(
        _mod_kernel, name="adaln", grid=(depth, n6 // tn),
        in_specs=[pl.BlockSpec((rows, D_MODEL), lambda l, j: (0, 0)),
                  pl.BlockSpec((1, D_MODEL, tn), lambda l, j: (l, 0, j)),
                  pl.BlockSpec((1, 1, tn), lambda l, j: (l, 0, j))],
        out_specs=pl.BlockSpec((1, rows, tn), lambda l, j: (l, 0, j)),
        out_shape=jax.ShapeDtypeStruct((depth, rows, n6), F32))(cc, ada_w, ada_b.reshape(depth, 1, n6))
    mod = out.reshape(depth, rows, 6, D_MODEL)
    return mod[:, :bsz], jnp.broadcast_to(mod[:, bsz:bsz + 1], (depth, bsz, 6, D_MODEL))


def _inproj_kernel(x_ref, mod_ref, g_ref, w_ref, s5_ref, hy_ref, ret_ref, mla_ref):
    h = _rms(x_ref[0], g_ref[...]) * (1.0 + mod_ref[0, 1:2, :]) + mod_ref[0, 0:1, :]
    hb = h.astype(BF16)
    s5_ref[0] = _dot(hb, w_ref[:, 0:256]).astype(BF16)
    hy_ref[0] = _dot(hb, w_ref[:, 256:1024]).astype(BF16)
    ret_ref[0] = _dot(hb, w_ref[:, 1024:2048]).astype(BF16)
    mla_ref[0] = _dot(hb, w_ref[:, 2048:2560]).astype(BF16)


def _inproj(x, mod, g, w):
    bsz, n, _ = x.shape
    tm = min(512, n)
    row = lambda width: pl.BlockSpec((1, tm, width), lambda b, i: (b, i, 0))
    shp = lambda width: jax.ShapeDtypeStruct((bsz, n, width), BF16)
    return _call(
        _inproj_kernel, name="inproj", grid=(bsz, n // tm),
        in_specs=[row(D_MODEL), pl.BlockSpec((1, 6, D_MODEL), lambda b, i: (b, 0, 0)),
                  pl.BlockSpec((1, D_MODEL), lambda b, i: (0, 0)),
                  pl.BlockSpec((D_MODEL, N_IN), lambda b, i: (0, 0))],
        out_specs=[row(256), row(768), row(1024), row(512)],
        out_shape=[shp(256), shp(768), shp(1024), shp(512)])(x, mod, g, w)


def _perm_w_in(w):
    z = lambda k: jnp.zeros((D_MODEL, k), w.dtype)
    return jnp.concatenate([w[:, :2048], w[:, 2048:2240], z(64), w[:, 2240:2368], w[:, 2368:2400], z(96)],
                           axis=1).astype(BF16)


def _gelu_tanh(x):
    return 0.5 * x * (1.0 + jnp.tanh(math.sqrt(2.0 / math.pi) * (x + 0.044715 * (x * x * x))))


def _s5_kernel(*refs, rev, tm, final):
    if final:
        (u_ref, bblk_ref, cblk_ref, tab_ref, tri_ref, h0_ref, yprev_ref, d_ref, gw_ref, gb_ref,
         y_ref, hfin_ref, hs_scr, carry_scr) = refs
    else:
        u_ref, bblk_ref, cblk_ref, tab_ref, tri_ref, h0_ref, y_ref, hfin_ref, hs_scr, carry_scr = refs
    nc = S5_COLS

    @pl.when(pl.program_id(1) == 0)
    def _():
        carry_scr[...] = h0_ref[0]

    car_r, car_i = tab_ref[4], tab_ref[5]
    nsub = S5_GRP // S5_SUB

    def cmul(t, xr, xi):
        tr, ti = tab_ref[t][None], tab_ref[t + 1][None]
        xr3, xi3 = xr.reshape(nsub, S5_SUB, nc), xi.reshape(nsub, S5_SUB, nc)
        return ((tr * xr3 - ti * xi3).reshape(S5_GRP, nc), (tr * xi3 + ti * xr3).reshape(S5_GRP, nc))

    order = list(range(tm // S5_GRP))
    subs = list(range(S5_GRP // S5_SUB))
    if rev:
        order, subs = order[::-1], subs[::-1]
    rows = lambda g: slice(g * S5_GRP, (g + 1) * S5_GRP)

    def project(g):
        return _dot(u_ref[0, rows(g), :], bblk_ref[...])

    def cumsum(bu):
        br, bi = bu[:, 0:nc], bu[:, nc:]
        gr, gi = cmul(0, br, bi)
        return _dot(tri_ref[...], jnp.concatenate([gr, gi], axis=1).astype(BF16))

    def scan(g, cs, cr, ci):
        csr, csi = cs[:, 0:nc], cs[:, nc:]
        wr, wi = cmul(2, csr, csi)
        r0 = g * S5_GRP
        for s in subs:
            a0 = s * S5_SUB
            hr = wr[a0:a0 + S5_SUB] + car_r * cr - car_i * ci
            hi = wi[a0:a0 + S5_SUB] + car_r * ci + car_i * cr
            hs_scr[r0 + a0:r0 + a0 + S5_SUB, 0:nc] = hr.astype(BF16)
            hs_scr[r0 + a0:r0 + a0 + S5_SUB, nc:] = hi.astype(BF16)
            edge = 0 if rev else S5_SUB - 1
            cr, ci = hr[edge:edge + 1], hi[edge:edge + 1]
        return cr, ci

    def readout(g):
        y = _dot(hs_scr[rows(g), :], cblk_ref[...])
        if final:
            yt = _gelu_tanh(yprev_ref[0, rows(g), :] + y + d_ref[...] * u_ref[0, rows(g), :].astype(F32))
            z = _dot(yt.astype(BF16), gw_ref[...]) + gb_ref[...]
            y_ref[0, rows(g), :] = (yt * jax.nn.sigmoid(z)).astype(y_ref.dtype)
        else:
            y_ref[0, rows(g), :] = y

    cr, ci = carry_scr[:, 0:nc], carry_scr[:, nc:]
    ng = len(order)
    bus = {0: project(order[0])}
    if ng > 1:
        bus[1] = project(order[1])
    css = {0: cumsum(bus.pop(0))}
    for i in range(ng):
        if i + 2 < ng:
            bus[i + 2] = project(order[i + 2])
        if i + 1 < ng:
            css[i + 1] = cumsum(bus.pop(i + 1))
        cr, ci = scan(order[i], css.pop(i), cr, ci)
        readout(order[i])
    carry = jnp.concatenate([cr, ci], axis=1)
    carry_scr[...] = carry
    hfin_ref[0] = carry


def _s5_dir(u, bblk, cblk, tab, tri, h0, rev, final=None):
    bsz, n, _ = u.shape
    tm = min(512, n)
    nb = n // tm
    blk = (lambda i: nb - 1 - i) if rev else (lambda i: i)
    row = pl.BlockSpec((1, tm, W_GROUP), lambda b, i: (b, blk(i), 0))
    full = lambda a: pl.BlockSpec(a.shape, lambda b, i: (0,) * a.ndim)
    state = pl.BlockSpec((1, 1, 2 * S5_COLS), lambda b, i: (b, 0, 0))
    ins = [u, bblk, cblk, tab, tri, h0]
    in_specs = [row, full(bblk), full(cblk), full(tab), full(tri), state]
    if final is not None:
        ins += list(final)
        in_specs += [row] + [full(a) for a in final[1:]]
    return _call(
        functools.partial(_s5_kernel, rev=rev, tm=tm, final=final is not None),
        name="s5_bwd" if rev else "s5_fwd", grid=(bsz, nb), in_specs=in_specs,
        out_specs=[row, state],
        out_shape=[jax.ShapeDtypeStruct((bsz, n, W_GROUP), BF16 if final is not None else F32),
                   jax.ShapeDtypeStruct((bsz, 1, 2 * S5_COLS), F32)],
        scratch_shapes=[pltpu.VMEM((tm, 2 * S5_COLS), BF16), pltpu.VMEM((1, 2 * S5_COLS), F32)])(*ins)


def _s5_weights(lam_re, lam_im, log_dt, b_re, b_im, c_re, c_im):
    g, p, ch = S5_GROUPS, S5_STATE, S5_CH
    dt = jnp.exp(log_dt)[..., None]
    re_dt = (lam_re * dt).reshape(2, 1, g * p)
    im_dt = (lam_im * dt).reshape(2, 1, g * p)
    lam = lax.complex(lam_re, lam_im)
    abar = jnp.exp(lax.complex(lam_re * dt, lam_im * dt))
    bbar = ((abar - 1.0) / lam)[..., None] * lax.complex(b_re, b_im)
    eye = jnp.eye(g, dtype=F32)
    place_b = lambda a: jnp.einsum('dgpc,gh->dgchp', a, eye).reshape(2, g * ch, g * p)
    bblk = jnp.concatenate([place_b(jnp.real(bbar)), place_b(jnp.imag(bbar))], axis=2).astype(BF16)
    place_c = lambda a: jnp.einsum('dgcp,gh->dgphc', a, eye).reshape(2, g * p, g * ch)
    cblk = jnp.concatenate([place_c(c_re), place_c(-c_im)], axis=1).astype(BF16)

    def powers(k, d):
        kk = k[:, None]
        mag = jnp.exp(kk * re_dt[d])
        return [mag * jnp.cos(kk * im_dt[d]), mag * jnp.sin(kk * im_dt[d])]

    i_sub = jnp.arange(S5_SUB, dtype=F32)
    tab_f = jnp.stack(powers(-i_sub, 0) + powers(i_sub, 0) + powers(i_sub + 1.0, 0))
    tab_b = jnp.stack(powers(i_sub, 1) + powers(-i_sub, 1) + powers(S5_SUB - i_sub, 1))
    ii = np.arange(S5_GRP)
    same = (ii[:, None] // S5_SUB) == (ii[None, :] // S5_SUB)
    tri_f = jnp.asarray(same & (ii[None, :] <= ii[:, None]), F32).astype(BF16)
    tri_b = jnp.asarray(same & (ii[None, :] >= ii[:, None]), F32).astype(BF16)
    return (bblk[0], cblk[0], tab_f, tri_f), (bblk[1], cblk[1], tab_b, tri_b)


def _s5_mixer(u_lat, u_ctx, prm, need_ctx):
    (lam_re, lam_im, log_dt, b_re, b_im, c_re, c_im, d_skip, glu_w, glu_b) = prm
    wf, wb = _s5_weights(lam_re, lam_im, log_dt, b_re, b_im, c_re, c_im)
    bsz = u_lat.shape[0]
    zero = jnp.zeros((bsz, 1, 2 * S5_COLS), F32)
    fin = lambda yprev: (yprev, d_skip.reshape(1, W_GROUP), glu_w.astype(BF16), glu_b.reshape(1, W_GROUP))
    yc_f, hc_f = _s5_dir(u_ctx, *wf, zero, rev=False)
    yc, hc_b = _s5_dir(u_ctx, *wb, zero, rev=True, final=fin(yc_f))
    yl_f, _ = _s5_dir(u_lat, *wf, hc_f, rev=False)
    yl, _ = _s5_dir(u_lat, *wb, hc_b, rev=True, final=fin(yl_f))
    return yl, (yc if need_ctx else None)


def _rot_half(x, half, period):
    lane = lax.broadcasted_iota(jnp.int32, x.shape, 1)
    width = x.shape[1]
    return jnp.where((lane % period) < half, pltpu.roll(x, width - half, axis=1), pltpu.roll(x, half, axis=1))


def _ret_kernel(*refs, rev, tm, nchunks, rotate, final):
    refs = list(refs)
    q_ref, k_ref, v_ref = refs[:3]
    del refs[:3]
    if rotate:
        cos_ref, sin_ref = refs[:2]
        del refs[:2]
    lgt_ref, lgc_ref, s0_ref = refs[:3]
    del refs[:3]
    if final:
        of_ref, g_ref, gn_ref, avg_ref = refs[:4]
        del refs[:4]
    o_ref, sfin_ref, d_scr, xz_scr, s_scr = refs
    hd = RET_HEADS * RET_DK

    @pl.when((pl.program_id(0) == 0) & (pl.program_id(1) == 0))
    def _():
        ri = lax.broadcasted_iota(jnp.int32, (tm, tm), 0)
        ci = lax.broadcasted_iota(jnp.int32, (tm, tm), 1)
        diff = (ci - ri) if rev else (ri - ci)
        dpos = jnp.maximum(diff, 0).astype(F32)
        for h in range(RET_HEADS):
            d_scr[h] = jnp.where(diff >= 0, jnp.exp(dpos * lgt_ref[h]), 0.0)
        pos = lax.broadcasted_iota(jnp.int32, (tm, hd), 0).astype(F32)
        lgc = lgc_ref[...]
        xz_scr[0] = jnp.exp(((tm - pos) if rev else (pos + 1.0)) * lgc)
        xz_scr[1] = jnp.exp((pos if rev else (tm - 1.0 - pos)) * lgc)

    @pl.when(pl.program_id(1) == 0)
    def _():
        s_scr[...] = s0_ref[0]

    head = lax.broadcasted_iota(jnp.int32, (tm, hd), 1) // RET_DK
    rh = lax.broadcasted_iota(jnp.int32, (hd, hd), 0) // RET_DK
    ch = lax.broadcasted_iota(jnp.int32, (hd, hd), 1) // RET_DK
    chunks = list(range(nchunks))
    state = s_scr[...]
    for c in (chunks[::-1] if rev else chunks):
        rows = slice(c * tm, (c + 1) * tm)
        q = q_ref[0, rows, :].astype(F32)
        k = k_ref[0, rows, :].astype(F32) * (RET_DK ** -0.5)
        if rotate:
            cos, sin = cos_ref[rows, :], sin_ref[rows, :]
            q = q * cos + _rot_half(q, RET_DK // 2, RET_DK) * sin
            k = k * cos + _rot_half(k, RET_DK // 2, RET_DK) * sin
        qb, kb, v = q.astype(BF16), k.astype(BF16), v_ref[0, rows, :]
        o = _dot((q * xz_scr[0]).astype(BF16), state.astype(BF16))
        for h in range(RET_HEADS):
            m = head == h
            s = _dot_nt(jnp.where(m, qb, jnp.zeros_like(qb)), kb)
            oh = _dot((s * d_scr[h]).astype(BF16), v)
            o = o + jnp.where(m, oh, 0.0)
        ds = _dot_tn(kb, (v.astype(F32) * xz_scr[1]).astype(BF16))
        state = state * jnp.exp(tm * lgc_ref[...]) + jnp.where(rh == ch, ds, 0.0)
        if final:
            o = o + of_ref[0, rows, :]
            o2 = o * o
            hi = o2.astype(BF16)
            lo = (o2 - hi.astype(F32)).astype(BF16)
            ms = _dot(hi, avg_ref[...]) + _dot(lo, avg_ref[...])
            g = g_ref[0, rows, :].astype(F32)
            o_ref[0, rows, :] = (_silu(g) * (o * lax.rsqrt(ms + EPS) * gn_ref[...])).astype(o_ref.dtype)
        else:
            o_ref[0, rows, :] = o
    s_scr[...] = state
    sfin_ref[0] = state


def _ret_dir(p, lg, s0, rev, rope=None, final=None):
    bsz, n, _ = p.shape
    tm = min(256, n)
    nchunks = 2 if n % (2 * tm) == 0 else 1
    tb = tm * nchunks
    nb = n // tb
    hd = RET_HEADS * RET_DK
    blk = (lambda i: nb - 1 - i) if rev else (lambda i: i)
    col = lambda j: pl.BlockSpec((1, tb, hd), lambda b, i: (b, blk(i), j))
    full = lambda a: pl.BlockSpec(a.shape, lambda b, i: (0,) * a.ndim)
    state = pl.BlockSpec((1, hd, hd), lambda b, i: (b, 0, 0))
    lgt = jnp.broadcast_to(lg[:, None, None], (RET_HEADS, 1, tm))
    lgc = jnp.repeat(lg, RET_DK).reshape(1, hd)
    ins, in_specs = [p, p, p], [col(0), col(1), col(2)]
    if rope is not None:
        ins += list(rope)
        in_specs += [pl.BlockSpec((tb, hd), lambda b, i: (blk(i), 0))] * 2
    ins += [lgt, lgc, s0]
    in_specs += [full(lgt), full(lgc), state]
    if final is not None:
        of, gn = final
        avg = jnp.asarray(np.kron(np.eye(RET_HEADS), np.full((RET_DK, RET_DK), 1.0 / RET_DK)), F32).astype(BF16)
        ins += [of, p, gn.reshape(1, hd), avg]
        in_specs += [pl.BlockSpec((1, tb, hd), lambda b, i: (b, blk(i), 0)), col(3), full(gn.reshape(1, hd)), full(avg)]
    return _call(
        functools.partial(_ret_kernel, rev=rev, tm=tm, nchunks=nchunks, rotate=rope is not None,
                          final=final is not None),
        name="ret_bwd" if rev else "ret_fwd", grid=(bsz, nb), in_specs=in_specs,
        out_specs=[pl.BlockSpec((1, tb, hd), lambda b, i: (b, blk(i), 0)), state],
        out_shape=[jax.ShapeDtypeStruct((bsz, n, hd), BF16 if final is not None else F32),
                   jax.ShapeDtypeStruct((bsz, hd, hd), F32)],
        scratch_shapes=[pltpu.VMEM((RET_HEADS, tm, tm), F32), pltpu.VMEM((2, tm, hd), F32),
                        pltpu.VMEM((hd, hd), F32)])(*ins)


def _ret_mixer(p_lat, p_ctx, prm, need_ctx):
    decay_exp, gn_g = prm
    lg = jnp.log1p(-jnp.exp2(-decay_exp))
    bsz, n, _ = p_lat.shape
    hd = RET_HEADS * RET_DK
    theta = RET_ROPE_BASE ** (-jnp.linspace(0.0, 1.0, RET_DK // 2, dtype=F32))
    ang = jnp.arange(n, dtype=F32)[:, None] * theta
    cos = jnp.tile(jnp.cos(ang), (1, 2 * RET_HEADS))
    sin = jnp.tile(jnp.concatenate([-jnp.sin(ang), jnp.sin(ang)], axis=1), (1, RET_HEADS))
    zero = jnp.zeros((bsz, hd, hd), F32)
    oc_f, sc_f = _ret_dir(p_ctx, lg[0], zero, rev=False)
    yc, sc_b = _ret_dir(p_ctx, lg[1], zero, rev=True, final=(oc_f, gn_g))
    ol_f, _ = _ret_dir(p_lat, lg[0], sc_f, rev=False, rope=(cos, sin))
    yl, _ = _ret_dir(p_lat, lg[1], sc_b, rev=True, rope=(cos, sin), final=(ol_f, gn_g))
    return yl, (yc if need_ctx else None)


def _mla_prep_kernel(*refs, rotate):
    if rotate:
        (p_ref, gq_ref, gkv_ref, wq_ref, wkv_ref, e_ref, wqr_ref, er_ref, cos_ref, sin_ref,
         q_ref, k_ref, v_ref) = refs
    else:
        p_ref, gq_ref, gkv_ref, wq_ref, wkv_ref, e_ref, q_ref, k_ref, v_ref = refs
    p = p_ref[0]
    cq = p[:, 0:256].astype(F32)
    cqn = cq * lax.rsqrt(jnp.sum(cq * cq, axis=-1, keepdims=True) * (1.0 / MLA_Q_RANK) + EPS) * gq_ref[...]
    cqb = cqn.astype(BF16)
    q = _dot(cqb, wq_ref[...])
    ckv = p[:, 256:384].astype(F32)
    kv = _dot(_rms(ckv, gkv_ref[...]).astype(BF16), wkv_ref[...])
    width = MLA_HEADS * MLA_HEAD_PAD
    k = kv[:, 0:width] + _dot(p[:, 384:512], e_ref[...])
    if rotate:
        cos, sin = cos_ref[...], sin_ref[...]
        q = q * cos + _dot(cqb, wqr_ref[...]) * sin
        k = k * cos + _dot(p[:, 384:512], er_ref[...]) * sin
    q_ref[0] = (q * (MLA_QK ** -0.5 * math.log2(math.e))).astype(BF16)
    k_ref[0] = k.astype(BF16)
    v_ref[0] = kv[:, width:].astype(BF16)


def _mla_prep(p, wts, rope=None):
    bsz, n, _ = p.shape
    tm = min(512, n)
    width = MLA_HEADS * MLA_HEAD_PAD
    full = lambda a: pl.BlockSpec(a.shape, lambda b, i: (0,) * a.ndim)
    row = lambda w: pl.BlockSpec((1, tm, w), lambda b, i: (b, i, 0))
    ins = [p] + list(wts)
    in_specs = [row(512)] + [full(a) for a in wts]
    if rope is not None:
        ins += list(rope)
        in_specs += [full(a) for a in rope[:2]] + [pl.BlockSpec((tm, width), lambda b, i: (i, 0))] * 2
    return _call(
        functools.partial(_mla_prep_kernel, rotate=rope is not None), name="mla_prep", grid=(bsz, n // tm),
        in_specs=in_specs, out_specs=[row(width), row(width), row(MLA_HEADS * MLA_V)],
        out_shape=[jax.ShapeDtypeStruct((bsz, n, width), BF16), jax.ShapeDtypeStruct((bsz, n, width), BF16),
                   jax.ShapeDtypeStruct((bsz, n, MLA_HEADS * MLA_V), BF16)])(*ins)


def _kv_blocks(nk, target=1152):
    assert nk % 128 == 0
    n128 = nk // 128
    nblk = max(1, -(-nk // target))
    sizes = [(n128 // nblk + (1 if j < n128 % nblk else 0)) * 128 for j in range(nblk)]
    starts = [sum(sizes[:j]) for j in range(nblk)]
    return list(zip(starts, sizes))


def _attn_kernel(q_ref, k_ref, v_ref, o_ref, *, blocks):
    q = q_ref[0]
    tq = q.shape[0]
    head = lax.broadcasted_iota(jnp.int32, (tq, MLA_HEADS * MLA_V), 1) // MLA_V
    units = [(j, h) for j in range(len(blocks)) for h in range(MLA_HEADS)]

    def scores(j, h):
        k0, kn = blocks[j]
        c0 = h * MLA_HEAD_PAD
        return _dot_nt(q[:, c0:c0 + MLA_HEAD_PAD], k_ref[0, k0:k0 + kn, c0:c0 + MLA_HEAD_PAD])

    m = [jnp.full((tq, 1), -jnp.inf, F32)] * MLA_HEADS
    l = [jnp.zeros((tq, 1), F32)] * MLA_HEADS
    acc = jnp.zeros((tq, MLA_HEADS * MLA_V), F32)
    s_next = scores(*units[0])
    for idx, (j, h) in enumerate(units):
        s = s_next
        if idx + 1 < len(units):
            s_next = scores(*units[idx + 1])
        k0, kn = blocks[j]
        m_new = jnp.maximum(m[h], jnp.max(s, axis=-1, keepdims=True))
        e = jnp.exp2(s - m_new)
        alpha = jnp.exp2(m[h] - m_new)
        l[h] = alpha * l[h] + jnp.sum(e, axis=-1, keepdims=True)
        m[h] = m_new
        pv = _dot(e.astype(BF16), v_ref[0, k0:k0 + kn, :])
        acc = jnp.where(head == h, alpha * acc + pv, acc)
    inv = jnp.zeros((tq, MLA_HEADS * MLA_V), F32)
    for h in range(MLA_HEADS):
        inv = jnp.where(head == h, 1.0 / l[h], inv)
    o_ref[0] = (acc * inv).astype(o_ref.dtype)


def _attention(q, k, v):
    bsz, n, width = q.shape
    nk = k.shape[1]
    tq = min(512, n)
    return _call(
        functools.partial(_attn_kernel, blocks=_kv_blocks(nk)), name="mla_attn", grid=(bsz, n // tq),
        in_specs=[pl.BlockSpec((1, tq, width), lambda b, i: (b, i, 0)),
                  pl.BlockSpec((1, nk, width), lambda b, i: (b, 0, 0)),
                  pl.BlockSpec((1, nk, MLA_HEADS * MLA_V), lambda b, i: (b, 0, 0))],
        out_specs=pl.BlockSpec((1, tq, MLA_HEADS * MLA_V), lambda b, i: (b, i, 0)),
        out_shape=jax.ShapeDtypeStruct((bsz, n, MLA_HEADS * MLA_V), BF16))(q, k, v)


def _mla_mixer(p_lat, p_ctx, prm, need_ctx):
    q_norm_g, kv_norm_g, w_uq, w_ukv = prm
    n = p_lat.shape[1]
    width = MLA_HEADS * MLA_HEAD_PAD
    wq = jnp.pad(w_uq.reshape(MLA_Q_RANK, MLA_HEADS, MLA_QK),
                 ((0, 256 - MLA_Q_RANK), (0, 0), (0, MLA_HEAD_PAD - MLA_QK))).reshape(256, width).astype(BF16)
    wkv = w_ukv.reshape(MLA_KV_RANK, MLA_HEADS, MLA_NOPE + MLA_V)
    wk = jnp.pad(wkv[:, :, :MLA_NOPE], ((0, 0), (0, 0), (0, MLA_HEAD_PAD - MLA_NOPE))).reshape(MLA_KV_RANK, width)
    wv = wkv[:, :, MLA_NOPE:].reshape(MLA_KV_RANK, MLA_HEADS * MLA_V)
    wkv = jnp.concatenate([wk, wv], axis=1).astype(BF16)
    place = np.zeros((128, width), np.float32)
    for h in range(MLA_HEADS):
        for j in range(MLA_ROPE):
            place[j, h * MLA_HEAD_PAD + MLA_NOPE + j] = 1.0
    place = jnp.asarray(place).astype(BF16)
    gq = jnp.pad(q_norm_g, (0, 256 - MLA_Q_RANK)).reshape(1, 256)
    wts = (gq, kv_norm_g.reshape(1, MLA_KV_RANK), wq, wkv, place)
    pos = jnp.arange(n)
    row, colp = (pos // GRID_W).astype(F32), (pos % GRID_W).astype(F32)
    n_freq = MLA_ROPE // 4
    inv = ROPE_BASE ** (-jnp.arange(n_freq, dtype=F32) / n_freq)
    ang = jnp.concatenate([row[:, None] * inv, colp[:, None] * inv], axis=-1)
    ones, zeros = jnp.ones((n, MLA_NOPE), F32), jnp.zeros((n, MLA_NOPE), F32)
    tail = MLA_HEAD_PAD - MLA_QK
    cos = jnp.concatenate([ones, jnp.cos(ang), jnp.cos(ang), jnp.ones((n, tail), F32)], axis=1)
    sin = jnp.concatenate([zeros, jnp.sin(ang), jnp.sin(ang), jnp.zeros((n, tail), F32)], axis=1)

    def rot_cols(w):
        w3 = w.reshape(w.shape[0], MLA_HEADS, MLA_HEAD_PAD)
        x1 = w3[..., MLA_NOPE:MLA_NOPE + MLA_ROPE // 2]
        x2 = w3[..., MLA_NOPE + MLA_ROPE // 2:MLA_QK]
        z = jnp.zeros_like
        return jnp.concatenate([z(w3[..., :MLA_NOPE]), -x2, x1, z(w3[..., MLA_QK:])], axis=-1).reshape(w.shape)

    rope = (rot_cols(wq), rot_cols(place), jnp.tile(cos, (1, MLA_HEADS)), jnp.tile(sin, (1, MLA_HEADS)))
    q_c, k_c, v_c = _mla_prep(p_ctx, wts)
    q_l, k_l, v_l = _mla_prep(p_lat, wts, rope)
    y_lat = _attention(q_l, jnp.concatenate([k_c, k_l], axis=1), jnp.concatenate([v_c, v_l], axis=1))
    y_ctx = _attention(q_c, k_c, v_c) if need_ctx else None
    return y_lat, y_ctx


def _hy_filter_kernel(z_ref, zt_ref, w1t_ref, b1_ref, w2t_ref, b2_ref, w3_ref, fr_ref, dl_ref, k_ref, hid_scr):
    dotp = lambda a, b: jnp.dot(a, b, preferred_element_type=F32, precision=HIGHEST)

    @pl.when(pl.program_id(0) == 0)
    def _():
        fr = fr_ref[...]
        hid = jnp.sin(fr * (dotp(w1t_ref[...], zt_ref[...]) + b1_ref[...]))
        hid = jnp.sin(fr * (dotp(w2t_ref[...], hid) + b2_ref[...]))
        hid_scr[...] = hid.T

    k = dotp(hid_scr[...], w3_ref[...]) * jnp.exp(-z_ref[:, 0:1] * jnp.abs(dl_ref[...]))
    ss = jnp.sum(k * k, axis=0, keepdims=True)
    r = lax.rsqrt(ss[:, 0:W_GROUP] + ss[:, W_GROUP:] + EPS)
    k_ref[...] = (k * jnp.concatenate([r, r], axis=1)).astype(k_ref.dtype)


def _hy_filter(n, prm):
    w1, b1, w2, b2, w3, freq, deltas = prm
    pos = jnp.arange(n, dtype=F32)
    t01 = pos / (n - 1)
    bands = jnp.linspace(1e-4, HY_BANDS - 1, HY_BANDS, dtype=F32)
    ang = (2.0 * math.pi / n) * pos[:, None] * bands[None, :]
    z = jnp.concatenate([t01[:, None], jnp.cos(ang), -jnp.sin(ang)], axis=-1)
    emb = z.shape[1]
    z = jnp.pad(z, ((0, 0), (0, 128 - emb)))
    w1p = jnp.pad(w1, ((0, 128 - emb), (0, 0)))
    col = lambda a: a.reshape(-1, 1)
    ins = [z, z.T, w1p.T, col(b1), w2.T, col(b2), w3, col(freq), deltas.reshape(1, -1)]
    full = lambda a: pl.BlockSpec(a.shape, lambda o: (0,) * a.ndim)
    ncol = w3.shape[1]
    cw = 2 * W_GROUP
    in_specs = [full(a) for a in ins]
    in_specs[6] = pl.BlockSpec((HY_HIDDEN, cw), lambda o: (0, o))
    in_specs[8] = pl.BlockSpec((1, cw), lambda o: (0, o))
    return _call(_hy_filter_kernel, name="hy_filter", grid=(ncol // cw,), in_specs=in_specs,
                 out_specs=pl.BlockSpec((n, cw), lambda o: (0, o)),
                 out_shape=jax.ShapeDtypeStruct((n, ncol), BF16),
                 scratch_shapes=[pltpu.VMEM((n, HY_HIDDEN), F32)])(*ins)


def _dwconv_rows(x, w_ref, b_ref):
    n = x.shape[0]
    r = lax.broadcasted_iota(jnp.int32, x.shape, 0)
    prev = jnp.where(r == 0, 0.0, pltpu.roll(x, 1, axis=0))
    nxt = jnp.where(r == n - 1, 0.0, pltpu.roll(x, n - 1, axis=0))
    return prev * w_ref[0:1, :] + x * w_ref[1:2, :] + nxt * w_ref[2:3, :] + b_ref[...]


def _fft_consts(n_tok):
    n = 2 * n_tok
    n2n = FFT_N2
    n1n = n // n2n
    k1h = -(-(n1n // 2 + 1) // 8) * 8
    k1 = np.arange(k1h)[None, :, None]
    n1 = np.arange(n1n // 2)[None, None, :]
    n2 = np.arange(n2n)[:, None, None]
    keep = (k1 <= n1n // 2).astype(np.float64)
    th = 2.0 * np.pi * ((k1 * (n2n * n1 + n2)) % n) / n
    fa = np.concatenate([np.cos(th) * keep, -np.sin(th) * keep], axis=1)
    pair = np.where((k1 == 0) | (k1 == n1n // 2), 1.0, 2.0)
    fat = np.transpose(fa * np.concatenate([pair, pair], axis=1), (0, 2, 1)) / n
    a = np.arange(n2n)
    t2 = 2.0 * np.pi * ((a[:, None] * a[None, :]) % n2n) / n2n
    fr, fi = np.cos(t2), -np.sin(t2)
    f2 = np.block([[fr, -fi], [fi, fr]])
    f2i = np.block([[fr, fi], [-fi, fr]])
    as_bf16 = lambda m: jnp.asarray(m, F32).astype(BF16)
    return as_bf16(fa), as_bf16(fat), as_bf16(f2), as_bf16(f2i)


def _fft_a_kernel(x_ref, fa_ref, o_ref, *, nb, cw):
    for j in range(nb):
        o_ref[:, j * cw:(j + 1) * cw] = _dot(fa_ref[j], x_ref[:, j * cw:(j + 1) * cw]).astype(o_ref.dtype)


def _fft_a(x, fa, ctot):
    h1 = x.shape[0]
    r2 = fa.shape[1]
    nb = 4
    return _call(
        functools.partial(_fft_a_kernel, nb=nb, cw=ctot), name="hy_fft_a", grid=(FFT_N2 // nb,),
        in_specs=[pl.BlockSpec((h1, nb * ctot), lambda j: (0, j)), pl.BlockSpec((nb, r2, h1), lambda j: (j, 0, 0))],
        out_specs=pl.BlockSpec((r2, nb * ctot), lambda j: (0, j)),
        out_shape=jax.ShapeDtypeStruct((r2, FFT_N2 * ctot), BF16))(x, fa)


def _fft_spec_kernel(a_ref, k0_ref, f2_ref, h_ref, *, kb):
    c = W_GROUP
    for j in range(kb):
        for o in range(HY_ORDER):
            af = jnp.concatenate([a_ref[0, 0, j, :, 2 * o * c:(2 * o + 1) * c],
                                  a_ref[0, 1, j, :, 2 * o * c:(2 * o + 1) * c]], axis=0)
            ab = jnp.concatenate([a_ref[0, 0, j, :, (2 * o + 1) * c:(2 * o + 2) * c],
                                  a_ref[0, 1, j, :, (2 * o + 1) * c:(2 * o + 2) * c]], axis=0)
            xf, xb = _dot(f2_ref[...], af), _dot(f2_ref[...], ab)
            k0 = k0_ref[:, (2 * o + 1) * c:(2 * o + 2) * c].astype(F32)
            h_ref[o, j, 0:FFT_N2, :] = (xf[0:FFT_N2] + xb[0:FFT_N2] - k0).astype(h_ref.dtype)
            h_ref[o, j, FFT_N2:, :] = (xf[FFT_N2:] - xb[FFT_N2:]).astype(h_ref.dtype)


def _fft_spec(a5, k0, f2):
    n1n = a5.shape[2]
    kb = 8
    ctot = a5.shape[4]
    return _call(
        functools.partial(_fft_spec_kernel, kb=kb), name="hy_fft_spec", grid=(n1n // kb,),
        in_specs=[pl.BlockSpec((1, 2, kb, FFT_N2, ctot), lambda i: (0, 0, i, 0, 0)),
                  pl.BlockSpec((1, ctot), lambda i: (0, 0)), pl.BlockSpec(f2.shape, lambda i: (0, 0))],
        out_specs=pl.BlockSpec((HY_ORDER, kb, 2 * FFT_N2, W_GROUP), lambda i: (0, i, 0, 0)),
        out_shape=jax.ShapeDtypeStruct((HY_ORDER, n1n, 2 * FFT_N2, W_GROUP), BF16))(a5, k0, f2)


HY_SLAB = 128
PITCH_PAD = 8
HY_UNROLL = 8


def _hy_fused_kernel(x1_ref, x2_ref, v_ref, w1_ref, w2_ref, wv_ref, b1_ref, b2_ref, bv_ref, spec_ref,
                     fa_ref, fat_ref, f2_ref, f2i_ref, bias_ref, o_ref, x_scr, a_scr, z_scr, y_scr, g_scr, *, n):
    n2n, h1 = FFT_N2, n // FFT_N2
    n1n = fa_ref.shape[1] // 2
    px, pa, pz = n2n + PITCH_PAD, 2 * n1n + PITCH_PAD, 2 * n2n + PITCH_PAD
    al = lambda i: pl.multiple_of(i, 8)
    unroll_c = max(d for d in range(1, 2 * HY_UNROLL + 1) if n1n % d == 0)

    g_scr[0] = _dwconv_rows(x1_ref[0].astype(F32), w1_ref, b1_ref).astype(BF16)
    g_scr[1] = _dwconv_rows(x2_ref[0].astype(F32), w2_ref, b2_ref).astype(BF16)
    u = _dwconv_rows(v_ref[0].astype(F32), wv_ref, bv_ref)
    for i in range(h1):
        x_scr[i * px:i * px + n2n, :] = u[i * n2n:(i + 1) * n2n]

    for o in range(HY_ORDER):
        def stage_a(t, carry):
            for jj in range(HY_UNROLL):
                n2 = t * HY_UNROLL + jj
                xs = x_scr[pl.ds(n2, h1, stride=px), :].astype(BF16)
                a_scr[pl.ds(al(n2 * pa), 2 * n1n), :] = _dot(fa_ref[n2], xs)
            return carry

        def stage_c(t, carry):
            for jj in range(unroll_c):
                k1 = t * unroll_c + jj
                a = jnp.concatenate([a_scr[pl.ds(k1, n2n, stride=pa), :],
                                     a_scr[pl.ds(n1n + k1, n2n, stride=pa), :]], axis=0).astype(BF16)
                x = _dot(f2_ref[...], a)
                xr, xi = x[0:n2n], x[n2n:]
                hr = spec_ref[o, k1, 0:n2n, :].astype(F32)
                hi = spec_ref[o, k1, n2n:, :].astype(F32)
                y = jnp.concatenate([xr * hr - xi * hi, xr * hi + xi * hr], axis=0).astype(BF16)
                z_scr[pl.ds(al(k1 * pz), 2 * n2n), :] = _dot(f2i_ref[...], y)
            return carry

        def stage_a_inv(t, carry):
            for jj in range(HY_UNROLL):
                n2 = t * HY_UNROLL + jj
                z = jnp.concatenate([z_scr[pl.ds(n2, n1n, stride=pz), :],
                                     z_scr[pl.ds(n2n + n2, n1n, stride=pz), :]], axis=0).astype(BF16)
                y_scr[pl.ds(al(n2 * px), h1), :] = _dot(fat_ref[n2], z)
            return carry

        def gate(t, carry):
            for jj in range(HY_UNROLL):
                n1 = t * HY_UNROLL + jj
                conv = y_scr[pl.ds(n1, n2n, stride=px), :]
                r0 = al(n1 * px)
                g = g_scr[o, pl.ds(pl.multiple_of(n1 * n2n, n2n), n2n), :].astype(F32)
                zn = g * (conv + x_scr[pl.ds(r0, n2n), :] * bias_ref[o:o + 1, :])
                if o + 1 < HY_ORDER:
                    x_scr[pl.ds(r0, n2n), :] = zn
                else:
                    o_ref[0, pl.ds(pl.multiple_of(n1 * n2n, n2n), n2n), :] = zn.astype(o_ref.dtype)
            return carry

        lax.fori_loop(0, n2n // HY_UNROLL, stage_a, 0)
        lax.fori_loop(0, n1n // unroll_c, stage_c, 0)
        lax.fori_loop(0, n2n // HY_UNROLL, stage_a_inv, 0)
        lax.fori_loop(0, h1 // HY_UNROLL, gate, 0)


def _hyena_lat(p, prm):
    conv_w, conv_b, w1, b1, w2, b2, w3, freq, deltas, bias = prm
    bsz, n, ch = p.shape
    c = W_GROUP
    h1 = n // FFT_N2
    fa, fat, f2, f2i = _fft_consts(n)
    n1n = fa.shape[1] // 2
    kf = _hy_filter(n, (w1, b1, w2, b2, w3, freq, deltas))
    ak = _fft_a(kf.reshape(h1, FFT_N2 * 4 * c), fa, 4 * c)
    spec = _fft_spec(ak.reshape(1, 2, n1n, FFT_N2, 4 * c), kf[0:1], f2)
    ns = c // HY_SLAB
    px, pa, pz = FFT_N2 + PITCH_PAD, 2 * n1n + PITCH_PAD, 2 * FFT_N2 + PITCH_PAD
    once = pl.Buffered(1)
    part = lambda a, k, rows: pl.BlockSpec((rows, HY_SLAB), lambda s, b: (0, k * ns + s))
    data = lambda k: pl.BlockSpec((1, n, HY_SLAB), lambda s, b: (b, 0, k * ns + s))
    full = lambda a: pl.BlockSpec(a.shape, lambda s, b: (0,) * a.ndim, pipeline_mode=once)
    cb = conv_b.reshape(1, ch)
    return _call(
        functools.partial(_hy_fused_kernel, n=n), name="hy_fused", grid=(ns, bsz),
        in_specs=[data(0), data(1), data(2), part(conv_w, 0, 3), part(conv_w, 1, 3), part(conv_w, 2, 3),
                  part(cb, 0, 1), part(cb, 1, 1), part(cb, 2, 1),
                  pl.BlockSpec((HY_ORDER, n1n, 2 * FFT_N2, HY_SLAB), lambda s, b: (0, 0, 0, s), pipeline_mode=once),
                  full(fa), full(fat), full(f2), full(f2i),
                  pl.BlockSpec((HY_ORDER, HY_SLAB), lambda s, b: (0, s))],
        out_specs=pl.BlockSpec((1, n, HY_SLAB), lambda s, b: (b, 0, s)),
        out_shape=jax.ShapeDtypeStruct((bsz, n, c), BF16),
        scratch_shapes=[pltpu.VMEM((h1 * px, HY_SLAB), F32), pltpu.VMEM((FFT_N2 * pa, HY_SLAB), F32),
                        pltpu.VMEM((n1n * pz, HY_SLAB), F32), pltpu.VMEM((FFT_N2 * px, HY_SLAB), F32),
                        pltpu.VMEM((HY_ORDER, n, HY_SLAB), BF16)])(
        p, p, p, conv_w, conv_w, conv_w, cb, cb, cb, spec, fa, fat, f2, f2i, bias)


def _hy_ctx_kernel(p_ref, cw_ref, cb_ref, k_ref, fd_ref, fdi_ref, bias_ref, o_ref):
    c = W_GROUP
    u = _dwconv_rows(p_ref[0].astype(F32), cw_ref, cb_ref)
    nf = fd_ref.shape[0] // 2
    xk = _dot(fd_ref[...], k_ref[...])
    z = u[:, 2 * c:]
    for o, gate in enumerate((u[:, 0:c], u[:, c:2 * c])):
        kf, kb = xk[:, 2 * o * c:(2 * o + 1) * c], xk[:, (2 * o + 1) * c:(2 * o + 2) * c]
        k0 = k_ref[0:1, (2 * o + 1) * c:(2 * o + 2) * c].astype(F32)
        hr = kf[0:nf] + kb[0:nf] - k0
        hi = kf[nf:] - kb[nf:]
        x = _dot(fd_ref[...], z.astype(BF16))
        xr, xi = x[0:nf], x[nf:]
        y = jnp.concatenate([xr * hr - xi * hi, xr * hi + xi * hr], axis=0).astype(BF16)
        z = gate * (_dot(fdi_ref[...], y) + z * bias_ref[o:o + 1, :])
    o_ref[0] = z.astype(o_ref.dtype)


def _hyena_ctx(p, prm):
    conv_w, conv_b, w1, b1, w2, b2, w3, freq, deltas, bias = prm
    bsz, n, ch = p.shape
    kf = _hy_filter(n, (w1, b1, w2, b2, w3, freq, deltas))
    nn = 2 * n
    th = 2.0 * np.pi * ((np.arange(nn)[:, None] * np.arange(n)[None, :]) % nn) / nn
    fd = np.concatenate([np.cos(th), -np.sin(th)], axis=0)
    fdj = jnp.asarray(fd, F32).astype(BF16)
    fdi = jnp.asarray(fd.T / nn, F32).astype(BF16)
    ins = [p, conv_w, conv_b.reshape(1, ch), kf, fdj, fdi, bias]
    full = lambda a: pl.BlockSpec(a.shape, lambda b: (0,) * a.ndim)
    return _call(
        _hy_ctx_kernel, name="hy_ctx", grid=(bsz,),
        in_specs=[pl.BlockSpec((1, n, ch), lambda b: (b, 0, 0))] + [full(a) for a in ins[1:]],
        out_specs=pl.BlockSpec((1, n, W_GROUP), lambda b: (b, 0, 0)),
        out_shape=jax.ShapeDtypeStruct((bsz, n, W_GROUP), BF16))(*ins)


def _outproj_kernel(y0_ref, y1_ref, y2_ref, y3_ref, w_ref, x_ref, mod_ref, g_ref, o_ref):
    c = W_GROUP
    y = (_dot(y0_ref[0], w_ref[0:c, :]) + _dot(y1_ref[0], w_ref[c:2 * c, :])
         + _dot(y2_ref[0], w_ref[2 * c:3 * c, :]) + _dot(y3_ref[0], w_ref[3 * c:, :]))
    o_ref[0] = x_ref[0] + mod_ref[0, 2:3, :] * _rms(y, g_ref[...])


def _outproj(ys, w, x, mod, g):
    bsz, n, _ = x.shape
    tm = min(512, n)
    row = lambda width: pl.BlockSpec((1, tm, width), lambda b, i: (b, i, 0))
    return _call(
        _outproj_kernel, name="outproj", grid=(bsz, n // tm),
        in_specs=[row(W_GROUP)] * 4 + [pl.BlockSpec(w.shape, lambda b, i: (0, 0)), row(D_MODEL),
                                       pl.BlockSpec((1, 6, D_MODEL), lambda b, i: (b, 0, 0)),
                                       pl.BlockSpec((1, D_MODEL), lambda b, i: (0, 0))],
        out_specs=row(D_MODEL), out_shape=jax.ShapeDtypeStruct(x.shape, F32))(*ys, w, x, mod, g)


def _ffn_kernel(xp_ref, x_ref, xn_ref, mod_ref, g2_ref, g3_ref, wup_ref, cw_ref, cb_ref, wdn_ref, perm_ref, permt_ref,
                o_ref, h_scr, u_scr, acc_scr, *, tm, nchunk):
    i, nb = pl.program_id(1), pl.num_programs(1)
    shift, scale = mod_ref[0, 3:4, :], mod_ref[0, 4:5, :]
    pre = lambda x: _rms(x, g2_ref[...]) * (1.0 + scale) + shift
    h_scr[0:tm, :] = _dot(perm_ref[...], pre(x_ref[0]).astype(BF16)).astype(BF16)
    before = pre(xp_ref[0])[7:8] * (i > 0).astype(F32)
    after = pre(xn_ref[0])[0:1] * (i < nb - 1).astype(F32)
    hrow = lax.broadcasted_iota(jnp.int32, (16, D_MODEL), 0)
    h_scr[tm:, :] = jnp.where(hrow == 0, before, jnp.where(hrow == 1, after, 0.0)).astype(BF16)
    sub = lax.broadcasted_iota(jnp.int32, (8, 2 * FF_CHUNK), 0)

    def cols(c):
        return (slice(c * FF_CHUNK, (c + 1) * FF_CHUNK), slice(D_FF + c * FF_CHUNK, D_FF + (c + 1) * FF_CHUNK))

    def up(c):
        ga, va = cols(c)
        hb = h_scr[...]
        r = jnp.concatenate([_dot(hb, wup_ref[:, ga]), _dot(hb, wup_ref[:, va])], axis=1)
        ub = u_scr.at[c % nbuf]
        ub[8:8 + tm, :] = r[0:tm]
        ub[0:8, :] = jnp.where(sub == 0, r[tm:tm + 1], pltpu.roll(r[tm - 8:tm], 1, axis=0))
        ub[8 + tm:, :] = jnp.where(sub == 7, r[tm + 1:tm + 2], pltpu.roll(r[0:8], 7, axis=0))

    nbuf = FF_AHEAD + 1
    for c in range(min(FF_AHEAD, nchunk)):
        up(c)
    for c in range(nchunk):
        ub = u_scr.at[c % nbuf]
        if c + FF_AHEAD < nchunk:
            up(c + FF_AHEAD)
        ga, va = cols(c)
        cw = jnp.concatenate([cw_ref[:, ga], cw_ref[:, va]], axis=1)
        cb = jnp.concatenate([cb_ref[:, ga], cb_ref[:, va]], axis=1)
        u = ub[0:tm, :] * cw[0:1, :] + ub[8:8 + tm, :] * cw[1:2, :] + ub[16:16 + tm, :] * cw[2:3, :] + cb
        act = (_silu(u[:, 0:FF_CHUNK]) * u[:, FF_CHUNK:]).astype(BF16)
        if c == 0:
            acc_scr[...] = _dot(act, wdn_ref[c])
        else:
            acc_scr[...] += _dot(act, wdn_ref[c])
    branch = (mod_ref[0, 5:6, :] * _rms(acc_scr[...], g3_ref[...])).astype(BF16)
    o_ref[0] = x_ref[0] + _dot(permt_ref[...], branch)


def _ffn(x, mod, g2, g3, w_up, conv_w, conv_b, w_down):
    bsz, n, _ = x.shape
    tm = min(256, n)
    nb = n // tm
    nchunk = D_FF // FF_CHUNK
    t8 = tm // 8
    pm = np.zeros((tm, tm), np.float32)
    for j in range(t8):
        for s in range(8):
            pm[8 * j + s, s * t8 + j] = 1.0
    perm, permt = jnp.asarray(pm).astype(BF16), jnp.asarray(pm.T).astype(BF16)
    wup, cw, cb = w_up.astype(BF16), conv_w, conv_b.reshape(1, 2 * D_FF)
    wdn = w_down.reshape(nchunk, FF_CHUNK, D_MODEL).astype(BF16)
    r8 = tm // 8
    last8 = n // 8 - 1
    full = lambda a: pl.BlockSpec(a.shape, lambda b, i: (0,) * a.ndim)
    vec = pl.BlockSpec((1, D_MODEL), lambda b, i: (0, 0))
    return _call(
        functools.partial(_ffn_kernel, tm=tm, nchunk=nchunk), name="convffn", grid=(bsz, nb),
        in_specs=[pl.BlockSpec((1, 8, D_MODEL), lambda b, i: (b, jnp.maximum(i * r8 - 1, 0), 0)),
                  pl.BlockSpec((1, tm, D_MODEL), lambda b, i: (b, i, 0)),
                  pl.BlockSpec((1, 8, D_MODEL), lambda b, i: (b, jnp.minimum((i + 1) * r8, last8), 0)),
                  pl.BlockSpec((1, 6, D_MODEL), lambda b, i: (b, 0, 0)), vec, vec,
                  full(wup), full(cw), full(cb), full(wdn), full(perm), full(permt)],
        out_specs=pl.BlockSpec((1, tm, D_MODEL), lambda b, i: (b, i, 0)),
        out_shape=jax.ShapeDtypeStruct(x.shape, F32),
        scratch_shapes=[pltpu.VMEM((tm + 16, D_MODEL), BF16), pltpu.VMEM((FF_AHEAD + 1, tm + 16, 2 * FF_CHUNK), F32),
                        pltpu.VMEM((tm, D_MODEL), F32)])(x, x, x, mod, g2, g3, wup, cw, cb, wdn, perm, permt)


def kernel(x, c, ctx, c_ctx, ada_w, ada_b, norm_g, w_in, w_out, s5_lam_re, s5_lam_im, s5_log_dt, s5_b_re, s5_b_im, s5_c_re, s5_c_im, s5_d, s5_glu_w, s5_glu_b, hy_conv_w, hy_conv_b, hy_w1, hy_b1, hy_w2, hy_b2, hy_w3, hy_freq, hy_deltas, hy_bias, ret_decay_exp, ret_gn_g, mla_q_norm_g, mla_kv_norm_g, mla_w_uq, mla_w_ukv, ffn_w_up, ffn_conv_w, ffn_conv_b, ffn_w_down):
    depth = ada_w.shape[0]
    mod_lat, mod_ctx = _adaln(c, c_ctx, ada_w, ada_b)
    for l in range(depth):
        need_ctx = l < depth - 1
        ml, mc = mod_lat[l], mod_ctx[l]
        g = lambda j: norm_g[l, j].reshape(1, D_MODEL)
        w_in_l = _perm_w_in(w_in[l])
        pl_s5, pl_hy, pl_ret, pl_mla = _inproj(x, ml, g(0), w_in_l)
        pc_s5, pc_hy, pc_ret, pc_mla = _inproj(ctx, mc, g(0), w_in_l)
        s5_p = (s5_lam_re[l], s5_lam_im[l], s5_log_dt[l], s5_b_re[l], s5_b_im[l], s5_c_re[l], s5_c_im[l],
                s5_d[l], s5_glu_w[l], s5_glu_b[l])
        hy_p = (hy_conv_w[l], hy_conv_b[l], hy_w1[l], hy_b1[l], hy_w2[l], hy_b2[l], hy_w3[l], hy_freq[l],
                hy_deltas[l], hy_bias[l])
        y_s5, yc_s5 = _s5_mixer(pl_s5, pc_s5, s5_p, need_ctx)
        y_hy = _hyena_lat(pl_hy, hy_p)
        y_ret, yc_ret = _ret_mixer(pl_ret, pc_ret, (ret_decay_exp[l], ret_gn_g[l]), need_ctx)
        y_mla, yc_mla = _mla_mixer(pl_mla, pc_mla, (mla_q_norm_g[l], mla_kv_norm_g[l], mla_w_uq[l], mla_w_ukv[l]),
                                   need_ctx)
        w_out_l = w_out[l].astype(BF16)
        ffn_p = (ffn_w_up[l], ffn_conv_w[l], ffn_conv_b[l], ffn_w_down[l])
        x = _outproj((y_s5, y_hy, y_ret, y_mla), w_out_l, x, ml, g(1))
        x = _ffn(x, ml, g(2), g(3), *ffn_p)
        if need_ctx:
            yc_hy = _hyena_ctx(pc_hy, hy_p)
            ctx = _outproj((yc_s5, yc_hy, yc_ret, yc_mla), w_out_l, ctx, mc, g(1))
            ctx = _ffn(ctx, mc, g(2), g(3), *ffn_p)
    return x
```

```python
import functools
import math

import numpy as np
import jax
import jax.numpy as jnp
from jax import lax
from jax.experimental import pallas as pl
from jax.experimental.pallas import tpu as pltpu

F32 = jnp.float32
BF16 = jnp.bfloat16
HIGHEST = lax.Precision.HIGHEST

EPS = 1e-6
D_MODEL = 1024
W_GROUP = 256
GRID_W = 64
S5_CH, S5_GROUPS, S5_STATE = 16, 16, 64
S5_COLS = S5_GROUPS * S5_STATE
S5_SUB = 16
S5_GRP = 128
HY_ORDER, HY_BANDS, HY_HIDDEN = 2, 16, 64
RET_HEADS, RET_DK = 4, 64
RET_ROPE_BASE = 10000.0
MLA_HEADS, MLA_NOPE, MLA_ROPE, MLA_V = 4, 64, 32, 64
MLA_Q_RANK, MLA_KV_RANK = 192, 128
MLA_QK = MLA_NOPE + MLA_ROPE
MLA_HEAD_PAD = 128
MLA_VROWS = MLA_V
ATTN_TQ = 512
ROPE_BASE = 10000.0
D_FF = 2816
FF_CHUNK = 256
FF_AHEAD = 2
N_IN = 2560
FFT_N2 = 64

VMEM_LIMIT_BYTES = 56 * 1024 * 1024


def _call(kernel, *, name, grid, in_specs, out_specs, out_shape, scratch_shapes=()):
    return pl.pallas_call(
        kernel, name=name, grid=grid, in_specs=in_specs, out_specs=out_specs, out_shape=out_shape,
        scratch_shapes=scratch_shapes,
        compiler_params=pltpu.CompilerParams(dimension_semantics=("arbitrary",) * len(grid),
                                             vmem_limit_bytes=VMEM_LIMIT_BYTES))


def _dot(a, b):
    return jnp.dot(a, b, preferred_element_type=F32)


def _dot_nt(a, b):
    return lax.dot_general(a, b, (((1,), (1,)), ((), ())), preferred_element_type=F32)


def _dot_tn(a, b):
    return lax.dot_general(a, b, (((0,), (0,)), ((), ())), preferred_element_type=F32)


def _rms(x, g):
    return x * lax.rsqrt(jnp.mean(x * x, axis=-1, keepdims=True) + EPS) * g


def _silu(x):
    return x * jax.nn.sigmoid(x)


def _mod_kernel(c_ref, w_ref, b_ref, o_ref):
    s = _silu(c_ref[...])
    o_ref[0] = jnp.dot(s, w_ref[0], preferred_element_type=F32, precision=HIGHEST) + b_ref[0]


def _adaln(c, c_ctx, ada_w, ada_b):
    bsz, depth, n6 = c.shape[0], ada_w.shape[0], ada_w.shape[2]
    rows = 8
    assert bsz + 1 <= rows
    cc = jnp.concatenate([c, c_ctx[None], jnp.zeros((rows - bsz - 1, D_MODEL), F32)], axis=0)
    tn = 512
    out = _call(
        _mod_kernel, name="adaln", grid=(depth, n6 // tn),
        in_specs=[pl.BlockSpec((rows, D_MODEL), lambda l, j: (0, 0)),
                  pl.BlockSpec((1, D_MODEL, tn), lambda l, j: (l, 0, j)),
                  pl.BlockSpec((1, 1, tn), lambda l, j: (l, 0, j))],
        out_specs=pl.BlockSpec((1, rows, tn), lambda l, j: (l, 0, j)),
        out_shape=jax.ShapeDtypeStruct((depth, rows, n6), F32))(cc, ada_w, ada_b.reshape(depth, 1, n6))
    mod = out.reshape(depth, rows, 6, D_MODEL)
    return mod[:, :bsz], jnp.broadcast_to(mod[:, bsz:bsz + 1], (depth, bsz, 6, D_MODEL))


def _inproj_kernel(x_ref, mod_ref, g_ref, w_ref, s5_ref, hy_ref, ret_ref, mla_ref):
    h = _rms(x_ref[0], g_ref[...]) * (1.0 + mod_ref[0, 1:2, :]) + mod_ref[0, 0:1, :]
    hb = h.astype(BF16)
    s5_ref[0] = _dot(hb, w_ref[:, 0:256]).astype(BF16)
    hy_ref[0] = _dot(hb, w_ref[:, 256:1024]).astype(BF16)
    ret_ref[0] = _dot(hb, w_ref[:, 1024:2048]).astype(BF16)
    mla_ref[0] = _dot(hb, w_ref[:, 2048:2560]).astype(BF16)


def _inproj(x, mod, g, w):
    bsz, n, _ = x.shape
    tm = min(512, n)
    row = lambda width: pl.BlockSpec((1, tm, width), lambda b, i: (b, i, 0))
    shp = lambda width: jax.ShapeDtypeStruct((bsz, n, width), BF16)
    return _call(
        _inproj_kernel, name="inproj", grid=(bsz, n // tm),
        in_specs=[row(D_MODEL), pl.BlockSpec((1, 6, D_MODEL), lambda b, i: (b, 0, 0)),
                  pl.BlockSpec((1, D_MODEL), lambda b, i: (0, 0)),
                  pl.BlockSpec((D_MODEL, N_IN), lambda b, i: (0, 0))],
        out_specs=[row(256), row(768), row(1024), row(512)],
        out_shape=[shp(256), shp(768), shp(1024), shp(512)])(x, mod, g, w)


def _perm_w_in(w):
    z = lambda k: jnp.zeros((D_MODEL, k), w.dtype)
    return jnp.concatenate([w[:, :2048], w[:, 2048:2240], z(64), w[:, 2240:2368], w[:, 2368:2400], z(96)],
                           axis=1).astype(BF16)


def _gelu_tanh(x):
    return 0.5 * x * (1.0 + jnp.tanh(math.sqrt(2.0 / math.pi) * (x + 0.044715 * (x * x * x))))


def _s5_kernel(*refs, rev, tm, final):
    if final:
        (u_ref, bblk_ref, cblk_ref, tab_ref, tri_ref, h0_ref, yprev_ref, d_ref, gw_ref, gb_ref,
         y_ref, hfin_ref, hs_scr, carry_scr) = refs
    else:
        u_ref, bblk_ref, cblk_ref, tab_ref, tri_ref, h0_ref, y_ref, hfin_ref, hs_scr, carry_scr = refs
    nc = S5_COLS

    @pl.when(pl.program_id(1) == 0)
    def _():
        carry_scr[...] = h0_ref[0]

    car_r, car_i = tab_ref[4], tab_ref[5]
    nsub = S5_GRP // S5_SUB

    def cmul(t, xr, xi):
        tr, ti = tab_ref[t][None], tab_ref[t + 1][None]
        xr3, xi3 = xr.reshape(nsub, S5_SUB, nc), xi.reshape(nsub, S5_SUB, nc)
        return ((tr * xr3 - ti * xi3).reshape(S5_GRP, nc), (tr * xi3 + ti * xr3).reshape(S5_GRP, nc))

    order = list(range(tm // S5_GRP))
    subs = list(range(S5_GRP // S5_SUB))
    if rev:
        order, subs = order[::-1], subs[::-1]
    rows = lambda g: slice(g * S5_GRP, (g + 1) * S5_GRP)

    def project(g):
        return _dot(u_ref[0, rows(g), :], bblk_ref[...])

    def cumsum(bu):
        br, bi = bu[:, 0:nc], bu[:, nc:]
        gr, gi = cmul(0, br, bi)
        return _dot(tri_ref[...], jnp.concatenate([gr, gi], axis=1).astype(BF16))

    def scan(g, cs, cr, ci):
        csr, csi = cs[:, 0:nc], cs[:, nc:]
        wr, wi = cmul(2, csr, csi)
        r0 = g * S5_GRP
        for s in subs:
            a0 = s * S5_SUB
            hr = wr[a0:a0 + S5_SUB] + car_r * cr - car_i * ci
            hi = wi[a0:a0 + S5_SUB] + car_r * ci + car_i * cr
            hs_scr[r0 + a0:r0 + a0 + S5_SUB, 0:nc] = hr.astype(BF16)
            hs_scr[r0 + a0:r0 + a0 + S5_SUB, nc:] = hi.astype(BF16)
            edge = 0 if rev else S5_SUB - 1
            cr, ci = hr[edge:edge + 1], hi[edge:edge + 1]
        return cr, ci

    def readout(g):
        y = _dot(hs_scr[rows(g), :], cblk_ref[...])
        if final:
            yt = _gelu_tanh(yprev_ref[0, rows(g), :] + y + d_ref[...] * u_ref[0, rows(g), :].astype(F32))
            z = _dot(yt.astype(BF16), gw_ref[...]) + gb_ref[...]
            y_ref[0, rows(g), :] = (yt * jax.nn.sigmoid(z)).astype(y_ref.dtype)
        else:
            y_ref[0, rows(g), :] = y

    cr, ci = carry_scr[:, 0:nc], carry_scr[:, nc:]
    ng = len(order)
    bus = {0: project(order[0])}
    if ng > 1:
        bus[1] = project(order[1])
    css = {0: cumsum(bus.pop(0))}
    for i in range(ng):
        if i + 2 < ng:
            bus[i + 2] = project(order[i + 2])
        if i + 1 < ng:
            css[i + 1] = cumsum(bus.pop(i + 1))
        cr, ci = scan(order[i], css.pop(i), cr, ci)
        readout(order[i])
    carry = jnp.concatenate([cr, ci], axis=1)
    carry_scr[...] = carry
    hfin_ref[0] = carry


def _s5_dir(u, bblk, cblk, tab, tri, h0, rev, final=None):
    bsz, n, _ = u.shape
    tm = min(1024, n)
    nb = n // tm
    blk = (lambda i: nb - 1 - i) if rev else (lambda i: i)
    row = pl.BlockSpec((1, tm, W_GROUP), lambda b, i: (b, blk(i), 0))
    full = lambda a: pl.BlockSpec(a.shape, lambda b, i: (0,) * a.ndim)
    state = pl.BlockSpec((1, 1, 2 * S5_COLS), lambda b, i: (b, 0, 0))
    ins = [u, bblk, cblk, tab, tri, h0]
    in_specs = [row, full(bblk), full(cblk), full(tab), full(tri), state]
    if final is not None:
        ins += list(final)
        in_specs += [row] + [full(a) for a in final[1:]]
    return _call(
        functools.partial(_s5_kernel, rev=rev, tm=tm, final=final is not None),
        name="s5_bwd" if rev else "s5_fwd", grid=(bsz, nb), in_specs=in_specs,
        out_specs=[row, state],
        out_shape=[jax.ShapeDtypeStruct((bsz, n, W_GROUP), BF16 if final is not None else F32),
                   jax.ShapeDtypeStruct((bsz, 1, 2 * S5_COLS), F32)],
        scratch_shapes=[pltpu.VMEM((tm, 2 * S5_COLS), BF16), pltpu.VMEM((1, 2 * S5_COLS), F32)])(*ins)


def _s5_weights(lam_re, lam_im, log_dt, b_re, b_im, c_re, c_im):
    g, p, ch = S5_GROUPS, S5_STATE, S5_CH
    dt = jnp.exp(log_dt)[..., None]
    re_dt = (lam_re * dt).reshape(2, 1, g * p)
    im_dt = (lam_im * dt).reshape(2, 1, g * p)
    lam = lax.complex(lam_re, lam_im)
    abar = jnp.exp(lax.complex(lam_re * dt, lam_im * dt))
    bbar = ((abar - 1.0) / lam)[..., None] * lax.complex(b_re, b_im)
    eye = jnp.eye(g, dtype=F32)
    place_b = lambda a: jnp.einsum('dgpc,gh->dgchp', a, eye).reshape(2, g * ch, g * p)
    bblk = jnp.concatenate([place_b(jnp.real(bbar)), place_b(jnp.imag(bbar))], axis=2).astype(BF16)
    place_c = lambda a: jnp.einsum('dgcp,gh->dgphc', a, eye).reshape(2, g * p, g * ch)
    cblk = jnp.concatenate([place_c(c_re), place_c(-c_im)], axis=1).astype(BF16)

    def powers(k, d):
        kk = k[:, None]
        mag = jnp.exp(kk * re_dt[d])
        return [mag * jnp.cos(kk * im_dt[d]), mag * jnp.sin(kk * im_dt[d])]

    i_sub = jnp.arange(S5_SUB, dtype=F32)
    tab_f = jnp.stack(powers(-i_sub, 0) + powers(i_sub, 0) + powers(i_sub + 1.0, 0))
    tab_b = jnp.stack(powers(i_sub, 1) + powers(-i_sub, 1) + powers(S5_SUB - i_sub, 1))
    ii = np.arange(S5_GRP)
    same = (ii[:, None] // S5_SUB) == (ii[None, :] // S5_SUB)
    tri_f = jnp.asarray(same & (ii[None, :] <= ii[:, None]), F32).astype(BF16)
    tri_b = jnp.asarray(same & (ii[None, :] >= ii[:, None]), F32).astype(BF16)
    return (bblk[0], cblk[0], tab_f, tri_f), (bblk[1], cblk[1], tab_b, tri_b)


def _s5_mixer(u_lat, u_ctx, prm, need_ctx):
    (lam_re, lam_im, log_dt, b_re, b_im, c_re, c_im, d_skip, glu_w, glu_b) = prm
    wf, wb = _s5_weights(lam_re, lam_im, log_dt, b_re, b_im, c_re, c_im)
    bsz = u_lat.shape[0]
    zero = jnp.zeros((bsz, 1, 2 * S5_COLS), F32)
    fin = lambda yprev: (yprev, d_skip.reshape(1, W_GROUP), glu_w.astype(BF16), glu_b.reshape(1, W_GROUP))
    yc_f, hc_f = _s5_dir(u_ctx, *wf, zero, rev=False)
    yc, hc_b = _s5_dir(u_ctx, *wb, zero, rev=True, final=fin(yc_f))
    yl_f, _ = _s5_dir(u_lat, *wf, hc_f, rev=False)
    yl, _ = _s5_dir(u_lat, *wb, hc_b, rev=True, final=fin(yl_f))
    return yl, (yc if need_ctx else None)


def _rot_half(x, half, period):
    lane = lax.broadcasted_iota(jnp.int32, x.shape, 1)
    width = x.shape[1]
    return jnp.where((lane % period) < half, pltpu.roll(x, width - half, axis=1), pltpu.roll(x, half, axis=1))


def _ret_kernel(*refs, rev, tm, nchunks, rotate, final):
    refs = list(refs)
    q_ref, k_ref, v_ref = refs[:3]
    del refs[:3]
    if rotate:
        cos_ref, sin_ref = refs[:2]
        del refs[:2]
    lgt_ref, lgc_ref, s0_ref = refs[:3]
    del refs[:3]
    if final:
        of_ref, g_ref, gn_ref, avg_ref = refs[:4]
        del refs[:4]
    o_ref, sfin_ref, d_scr, xz_scr, s_scr = refs
    hd = RET_HEADS * RET_DK

    @pl.when((pl.program_id(0) == 0) & (pl.program_id(1) == 0))
    def _():
        ri = lax.broadcasted_iota(jnp.int32, (tm, tm), 0)
        ci = lax.broadcasted_iota(jnp.int32, (tm, tm), 1)
        diff = (ci - ri) if rev else (ri - ci)
        dpos = jnp.maximum(diff, 0).astype(F32)
        for h in range(RET_HEADS):
            d_scr[h] = jnp.where(diff >= 0, jnp.exp(dpos * lgt_ref[h]), 0.0)
        pos = lax.broadcasted_iota(jnp.int32, (tm, hd), 0).astype(F32)
        lgc = lgc_ref[...]
        xz_scr[0] = jnp.exp(((tm - pos) if rev else (pos + 1.0)) * lgc)
        xz_scr[1] = jnp.exp((pos if rev else (tm - 1.0 - pos)) * lgc)

    @pl.when(pl.program_id(1) == 0)
    def _():
        s_scr[...] = s0_ref[0]

    head = lax.broadcasted_iota(jnp.int32, (tm, hd), 1) // RET_DK
    rh = lax.broadcasted_iota(jnp.int32, (hd, hd), 0) // RET_DK
    ch = lax.broadcasted_iota(jnp.int32, (hd, hd), 1) // RET_DK
    chunks = list(range(nchunks))
    state = s_scr[...]
    for c in (chunks[::-1] if rev else chunks):
        rows = slice(c * tm, (c + 1) * tm)
        q = q_ref[0, rows, :].astype(F32)
        k = k_ref[0, rows, :].astype(F32) * (RET_DK ** -0.5)
        if rotate:
            cos, sin = cos_ref[rows, :], sin_ref[rows, :]
            q = q * cos + _rot_half(q, RET_DK // 2, RET_DK) * sin
            k = k * cos + _rot_half(k, RET_DK // 2, RET_DK) * sin
        qb, kb, v = q.astype(BF16), k.astype(BF16), v_ref[0, rows, :]
        o = _dot((q * xz_scr[0]).astype(BF16), state.astype(BF16))
        for h in range(RET_HEADS):
            m = head == h
            s = _dot_nt(jnp.where(m, qb, jnp.zeros_like(qb)), kb)
            oh = _dot((s * d_scr[h]).astype(BF16), v)
            o = o + jnp.where(m, oh, 0.0)
        ds = _dot_tn(kb, (v.astype(F32) * xz_scr[1]).astype(BF16))
        state = state * jnp.exp(tm * lgc_ref[...]) + jnp.where(rh == ch, ds, 0.0)
        if final:
            o = o + of_ref[0, rows, :]
            o2 = o * o
            hi = o2.astype(BF16)
            lo = (o2 - hi.astype(F32)).astype(BF16)
            ms = _dot(hi, avg_ref[...]) + _dot(lo, avg_ref[...])
            g = g_ref[0, rows, :].astype(F32)
            o_ref[0, rows, :] = (_silu(g) * (o * lax.rsqrt(ms + EPS) * gn_ref[...])).astype(o_ref.dtype)
        else:
            o_ref[0, rows, :] = o
    s_scr[...] = state
    sfin_ref[0] = state


def _ret_dir(p, lg, s0, rev, rope=None, final=None):
    bsz, n, _ = p.shape
    tm = min(256, n)
    nchunks = 2 if n % (2 * tm) == 0 else 1
    tb = tm * nchunks
    nb = n // tb
    hd = RET_HEADS * RET_DK
    blk = (lambda i: nb - 1 - i) if rev else (lambda i: i)
    col = lambda j: pl.BlockSpec((1, tb, hd), lambda b, i: (b, blk(i), j))
    full = lambda a: pl.BlockSpec(a.shape, lambda b, i: (0,) * a.ndim)
    state = pl.BlockSpec((1, hd, hd), lambda b, i: (b, 0, 0))
    lgt = jnp.broadcast_to(lg[:, None, None], (RET_HEADS, 1, tm))
    lgc = jnp.repeat(lg, RET_DK).reshape(1, hd)
    ins, in_specs = [p, p, p], [col(0), col(1), col(2)]
    if rope is not None:
        ins += list(rope)
        in_specs += [pl.BlockSpec((tb, hd), lambda b, i: (blk(i), 0))] * 2
    ins += [lgt, lgc, s0]
    in_specs += [full(lgt), full(lgc), state]
    if final is not None:
        of, gn = final
        avg = jnp.asarray(np.kron(np.eye(RET_HEADS), np.full((RET_DK, RET_DK), 1.0 / RET_DK)), F32).astype(BF16)
        ins += [of, p, gn.reshape(1, hd), avg]
        in_specs += [pl.BlockSpec((1, tb, hd), lambda b, i: (b, blk(i), 0)), col(3), full(gn.reshape(1, hd)), full(avg)]
    return _call(
        functools.partial(_ret_kernel, rev=rev, tm=tm, nchunks=nchunks, rotate=rope is not None,
                          final=final is not None),
        name="ret_bwd" if rev else "ret_fwd", grid=(bsz, nb), in_specs=in_specs,
        out_specs=[pl.BlockSpec((1, tb, hd), lambda b, i: (b, blk(i), 0)), state],
        out_shape=[jax.ShapeDtypeStruct((bsz, n, hd), BF16 if final is not None else F32),
                   jax.ShapeDtypeStruct((bsz, hd, hd), F32)],
        scratch_shapes=[pltpu.VMEM((RET_HEADS, tm, tm), F32), pltpu.VMEM((2, tm, hd), F32),
                        pltpu.VMEM((hd, hd), F32)])(*ins)


def _ret_mixer(p_lat, p_ctx, prm, need_ctx):
    decay_exp, gn_g = prm
    lg = jnp.log1p(-jnp.exp2(-decay_exp))
    bsz, n, _ = p_lat.shape
    hd = RET_HEADS * RET_DK
    theta = RET_ROPE_BASE ** (-jnp.linspace(0.0, 1.0, RET_DK // 2, dtype=F32))
    ang = jnp.arange(n, dtype=F32)[:, None] * theta
    cos = jnp.tile(jnp.cos(ang), (1, 2 * RET_HEADS))
    sin = jnp.tile(jnp.concatenate([-jnp.sin(ang), jnp.sin(ang)], axis=1), (1, RET_HEADS))
    zero = jnp.zeros((bsz, hd, hd), F32)
    oc_f, sc_f = _ret_dir(p_ctx, lg[0], zero, rev=False)
    yc, sc_b = _ret_dir(p_ctx, lg[1], zero, rev=True, final=(oc_f, gn_g))
    ol_f, _ = _ret_dir(p_lat, lg[0], sc_f, rev=False, rope=(cos, sin))
    yl, _ = _ret_dir(p_lat, lg[1], sc_b, rev=True, rope=(cos, sin), final=(ol_f, gn_g))
    return yl, (yc if need_ctx else None)


def _mla_prep_kernel(*refs, rotate):
    if rotate:
        (p_ref, gq_ref, gkv_ref, wq_ref, wk_ref, wvt_ref, e_ref, wqr_ref, er_ref, cos_ref, sin_ref,
         q_ref, k_ref, vt_ref) = refs
    else:
        p_ref, gq_ref, gkv_ref, wq_ref, wk_ref, wvt_ref, e_ref, q_ref, k_ref, vt_ref = refs
    p = p_ref[0]
    cq = p[:, 0:256].astype(F32)
    cqn = cq * lax.rsqrt(jnp.sum(cq * cq, axis=-1, keepdims=True) * (1.0 / MLA_Q_RANK) + EPS) * gq_ref[...]
    cqb = cqn.astype(BF16)
    q = _dot(cqb, wq_ref[...])
    ckv = p[:, 256:384].astype(F32)
    ckvb = _rms(ckv, gkv_ref[...]).astype(BF16)
    k = _dot(ckvb, wk_ref[...]) + _dot(p[:, 384:512], e_ref[...])
    if rotate:
        cos, sin = cos_ref[...], sin_ref[...]
        q = q * cos + _dot(cqb, wqr_ref[...]) * sin
        k = k * cos + _dot(p[:, 384:512], er_ref[...]) * sin
    q_ref[0] = (q * (MLA_QK ** -0.5 * math.log2(math.e))).astype(BF16)
    k_ref[0] = k.astype(BF16)
    vt_ref[0] = _dot_nt(wvt_ref[...], ckvb).astype(BF16)


def _mla_prep(p, wts, rope=None):
    bsz, n, _ = p.shape
    tm = min(512, n)
    width = MLA_HEADS * MLA_HEAD_PAD
    full = lambda a: pl.BlockSpec(a.shape, lambda b, i: (0,) * a.ndim)
    row = lambda w: pl.BlockSpec((1, tm, w), lambda b, i: (b, i, 0))
    ins = [p] + list(wts)
    in_specs = [row(512)] + [full(a) for a in wts]
    if rope is not None:
        ins += list(rope)
        in_specs += [full(a) for a in rope[:2]] + [pl.BlockSpec((tm, width), lambda b, i: (i, 0))] * 2
    return _call(
        functools.partial(_mla_prep_kernel, rotate=rope is not None), name="mla_prep", grid=(bsz, n // tm),
        in_specs=in_specs,
        out_specs=[row(width), row(width), pl.BlockSpec((1, MLA_HEADS * MLA_VROWS, tm), lambda b, i: (b, 0, i))],
        out_shape=[jax.ShapeDtypeStruct((bsz, n, width), BF16), jax.ShapeDtypeStruct((bsz, n, width), BF16),
                   jax.ShapeDtypeStruct((bsz, MLA_HEADS * MLA_VROWS, n), BF16)])(*ins)


def _kv_blocks(nk, target=1152):
    assert nk % 128 == 0
    n128 = nk // 128
    nblk = max(1, -(-nk // target))
    sizes = [(n128 // nblk + (1 if j < n128 % nblk else 0)) * 128 for j in range(nblk)]
    starts = [sum(sizes[:j]) for j in range(nblk)]
    return list(zip(starts, sizes))


def _attn_kernel(q_ref, k_ref, vt_ref, o_ref, *, blocks):
    q = q_ref[0]
    tq = q.shape[0]
    units = [(j, h) for j in range(len(blocks)) for h in range(MLA_HEADS)]

    def scores(j, h):
        k0, kn = blocks[j]
        c0 = h * MLA_HEAD_PAD
        return _dot_nt(k_ref[0, k0:k0 + kn, c0:c0 + MLA_HEAD_PAD], q[:, c0:c0 + MLA_HEAD_PAD])

    m = [jnp.full((1, tq), -jnp.inf, F32)] * MLA_HEADS
    l = [jnp.zeros((1, tq), F32)] * MLA_HEADS
    acc = [jnp.zeros((MLA_V, tq), F32)] * MLA_HEADS
    s_next = scores(*units[0])
    for idx, (j, h) in enumerate(units):
        s = s_next
        if idx + 1 < len(units):
            s_next = scores(*units[idx + 1])
        k0, kn = blocks[j]
        m_new = jnp.maximum(m[h], jnp.max(s, axis=0, keepdims=True))
        e = jnp.exp2(s - m_new)
        alpha = jnp.exp2(m[h] - m_new)
        l[h] = alpha * l[h] + jnp.sum(e, axis=0, keepdims=True)
        m[h] = m_new
        pv = _dot(vt_ref[0, h * MLA_V:(h + 1) * MLA_V, k0:k0 + kn], e.astype(BF16))
        acc[h] = alpha * acc[h] + pv
    out_t = jnp.concatenate([acc[h] * (1.0 / l[h]) for h in range(MLA_HEADS)], axis=0)
    o_ref[0] = out_t.T.astype(o_ref.dtype)


def _attention(q, k, vt):
    bsz, n, width = q.shape
    nk = k.shape[1]
    tq = min(ATTN_TQ, n)
    return _call(
        functools.partial(_attn_kernel, blocks=_kv_blocks(nk)), name="mla_attn", grid=(bsz, n // tq),
        in_specs=[pl.BlockSpec((1, tq, width), lambda b, i: (b, i, 0)),
                  pl.BlockSpec((1, nk, width), lambda b, i: (b, 0, 0)),
                  pl.BlockSpec((1, MLA_HEADS * MLA_VROWS, nk), lambda b, i: (b, 0, 0))],
        out_specs=pl.BlockSpec((1, tq, MLA_HEADS * MLA_V), lambda b, i: (b, i, 0)),
        out_shape=jax.ShapeDtypeStruct((bsz, n, MLA_HEADS * MLA_V), BF16))(q, k, vt)


def _mla_mixer(p_lat, p_ctx, prm, need_ctx):
    q_norm_g, kv_norm_g, w_uq, w_ukv = prm
    n = p_lat.shape[1]
    width = MLA_HEADS * MLA_HEAD_PAD
    wq = jnp.pad(w_uq.reshape(MLA_Q_RANK, MLA_HEADS, MLA_QK),
                 ((0, 256 - MLA_Q_RANK), (0, 0), (0, MLA_HEAD_PAD - MLA_QK))).reshape(256, width).astype(BF16)
    wkv = w_ukv.reshape(MLA_KV_RANK, MLA_HEADS, MLA_NOPE + MLA_V)
    wk = jnp.pad(wkv[:, :, :MLA_NOPE], ((0, 0), (0, 0), (0, MLA_HEAD_PAD - MLA_NOPE))).reshape(MLA_KV_RANK, width)
    wv = wkv[:, :, MLA_NOPE:].reshape(MLA_KV_RANK, MLA_HEADS * MLA_V)
    place = np.zeros((128, width), np.float32)
    for h in range(MLA_HEADS):
        for j in range(MLA_ROPE):
            place[j, h * MLA_HEAD_PAD + MLA_NOPE + j] = 1.0
    place = jnp.asarray(place).astype(BF16)
    gq = jnp.pad(q_norm_g, (0, 256 - MLA_Q_RANK)).reshape(1, 256)
    wts = (gq, kv_norm_g.reshape(1, MLA_KV_RANK), wq, wk.astype(BF16), wv.T.astype(BF16), place)
    pos = jnp.arange(n)
    row, colp = (pos // GRID_W).astype(F32), (pos % GRID_W).astype(F32)
    n_freq = MLA_ROPE // 4
    inv = ROPE_BASE ** (-jnp.arange(n_freq, dtype=F32) / n_freq)
    ang = jnp.concatenate([row[:, None] * inv, colp[:, None] * inv], axis=-1)
    ones, zeros = jnp.ones((n, MLA_NOPE), F32), jnp.zeros((n, MLA_NOPE), F32)
    tail = MLA_HEAD_PAD - MLA_QK
    cos = jnp.concatenate([ones, jnp.cos(ang), jnp.cos(ang), jnp.ones((n, tail), F32)], axis=1)
    sin = jnp.concatenate([zeros, jnp.sin(ang), jnp.sin(ang), jnp.zeros((n, tail), F32)], axis=1)

    def rot_cols(w):
        w3 = w.reshape(w.shape[0], MLA_HEADS, MLA_HEAD_PAD)
        x1 = w3[..., MLA_NOPE:MLA_NOPE + MLA_ROPE // 2]
        x2 = w3[..., MLA_NOPE + MLA_ROPE // 2:MLA_QK]
        z = jnp.zeros_like
        return jnp.concatenate([z(w3[..., :MLA_NOPE]), -x2, x1, z(w3[..., MLA_QK:])], axis=-1).reshape(w.shape)

    rope = (rot_cols(wq), rot_cols(place), jnp.tile(cos, (1, MLA_HEADS)), jnp.tile(sin, (1, MLA_HEADS)))
    q_c, k_c, v_c = _mla_prep(p_ctx, wts)
    q_l, k_l, v_l = _mla_prep(p_lat, wts, rope)
    y_lat = _attention(q_l, jnp.concatenate([k_c, k_l], axis=1), jnp.concatenate([v_c, v_l], axis=2))
    y_ctx = _attention(q_c, k_c, v_c) if need_ctx else None
    return y_lat, y_ctx


def _hy_filter_kernel(z_ref, zt_ref, w1t_ref, b1_ref, w2t_ref, b2_ref, w3_ref, fr_ref, dl_ref, k_ref, hid_scr):
    dotp = lambda a, b: jnp.dot(a, b, preferred_element_type=F32, precision=HIGHEST)

    @pl.when(pl.program_id(0) == 0)
    def _():
        fr = fr_ref[...]
        hid = jnp.sin(fr * (dotp(w1t_ref[...], zt_ref[...]) + b1_ref[...]))
        hid = jnp.sin(fr * (dotp(w2t_ref[...], hid) + b2_ref[...]))
        hid_scr[...] = hid.T

    k = dotp(hid_scr[...], w3_ref[...]) * jnp.exp(-z_ref[:, 0:1] * jnp.abs(dl_ref[...]))
    ss = jnp.sum(k * k, axis=0, keepdims=True)
    r = lax.rsqrt(ss[:, 0:W_GROUP] + ss[:, W_GROUP:] + EPS)
    k_ref[...] = (k * jnp.concatenate([r, r], axis=1)).astype(k_ref.dtype)


def _hy_filter(n, prm):
    w1, b1, w2, b2, w3, freq, deltas = prm
    pos = jnp.arange(n, dtype=F32)
    t01 = pos / (n - 1)
    bands = jnp.linspace(1e-4, HY_BANDS - 1, HY_BANDS, dtype=F32)
    ang = (2.0 * math.pi / n) * pos[:, None] * bands[None, :]
    z = jnp.concatenate([t01[:, None], jnp.cos(ang), -jnp.sin(ang)], axis=-1)
    emb = z.shape[1]
    z = jnp.pad(z, ((0, 0), (0, 128 - emb)))
    w1p = jnp.pad(w1, ((0, 128 - emb), (0, 0)))
    col = lambda a: a.reshape(-1, 1)
    ins = [z, z.T, w1p.T, col(b1), w2.T, col(b2), w3, col(freq), deltas.reshape(1, -1)]
    full = lambda a: pl.BlockSpec(a.shape, lambda o: (0,) * a.ndim)
    ncol = w3.shape[1]
    cw = 2 * W_GROUP
    in_specs = [full(a) for a in ins]
    in_specs[6] = pl.BlockSpec((HY_HIDDEN, cw), lambda o: (0, o))
    in_specs[8] = pl.BlockSpec((1, cw), lambda o: (0, o))
    return _call(_hy_filter_kernel, name="hy_filter", grid=(ncol // cw,), in_specs=in_specs,
                 out_specs=pl.BlockSpec((n, cw), lambda o: (0, o)),
                 out_shape=jax.ShapeDtypeStruct((n, ncol), BF16),
                 scratch_shapes=[pltpu.VMEM((n, HY_HIDDEN), F32)])(*ins)


def _dwconv_rows(x, w_ref, b_ref):
    n = x.shape[0]
    r = lax.broadcasted_iota(jnp.int32, x.shape, 0)
    prev = jnp.where(r == 0, 0.0, pltpu.roll(x, 1, axis=0))
    nxt = jnp.where(r == n - 1, 0.0, pltpu.roll(x, n - 1, axis=0))
    return prev * w_ref[0:1, :] + x * w_ref[1:2, :] + nxt * w_ref[2:3, :] + b_ref[...]


def _fft_consts(n_tok):
    n = 2 * n_tok
    n2n = FFT_N2
    n1n = n // n2n
    k1h = -(-(n1n // 2 + 1) // 8) * 8
    k1 = np.arange(k1h)[None, :, None]
    n1 = np.arange(n1n // 2)[None, None, :]
    n2 = np.arange(n2n)[:, None, None]
    keep = (k1 <= n1n // 2).astype(np.float64)
    th = 2.0 * np.pi * ((k1 * (n2n * n1 + n2)) % n) / n
    fa = np.concatenate([np.cos(th) * keep, -np.sin(th) * keep], axis=1)
    pair = np.where((k1 == 0) | (k1 == n1n // 2), 1.0, 2.0)
    fat = np.transpose(fa * np.concatenate([pair, pair], axis=1), (0, 2, 1)) / n
    a = np.arange(n2n)
    t2 = 2.0 * np.pi * ((a[:, None] * a[None, :]) % n2n) / n2n
    fr, fi = np.cos(t2), -np.sin(t2)
    f2 = np.block([[fr, -fi], [fi, fr]])
    f2i = np.block([[fr, fi], [-fi, fr]])
    as_bf16 = lambda m: jnp.asarray(m, F32).astype(BF16)
    return as_bf16(fa), as_bf16(fat), as_bf16(f2), as_bf16(f2i)


def _fft_a_kernel(x_ref, fa_ref, o_ref, *, nb, cw):
    for j in range(nb):
        o_ref[:, j * cw:(j + 1) * cw] = _dot(fa_ref[j], x_ref[:, j * cw:(j + 1) * cw]).astype(o_ref.dtype)


def _fft_a(x, fa, ctot):
    h1 = x.shape[0]
    r2 = fa.shape[1]
    nb = 4
    return _call(
        functools.partial(_fft_a_kernel, nb=nb, cw=ctot), name="hy_fft_a", grid=(FFT_N2 // nb,),
        in_specs=[pl.BlockSpec((h1, nb * ctot), lambda j: (0, j)), pl.BlockSpec((nb, r2, h1), lambda j: (j, 0, 0))],
        out_specs=pl.BlockSpec((r2, nb * ctot), lambda j: (0, j)),
        out_shape=jax.ShapeDtypeStruct((r2, FFT_N2 * ctot), BF16))(x, fa)


def _fft_spec_kernel(a_ref, k0_ref, f2_ref, h_ref, *, kb):
    c = W_GROUP
    for j in range(kb):
        for o in range(HY_ORDER):
            af = jnp.concatenate([a_ref[0, 0, j, :, 2 * o * c:(2 * o + 1) * c],
                                  a_ref[0, 1, j, :, 2 * o * c:(2 * o + 1) * c]], axis=0)
            ab = jnp.concatenate([a_ref[0, 0, j, :, (2 * o + 1) * c:(2 * o + 2) * c],
                                  a_ref[0, 1, j, :, (2 * o + 1) * c:(2 * o + 2) * c]], axis=0)
            xf, xb = _dot(f2_ref[...], af), _dot(f2_ref[...], ab)
            k0 = k0_ref[:, (2 * o + 1) * c:(2 * o + 2) * c].astype(F32)
            h_ref[o, j, 0:FFT_N2, :] = (xf[0:FFT_N2] + xb[0:FFT_N2] - k0).astype(h_ref.dtype)
            h_ref[o, j, FFT_N2:, :] = (xf[FFT_N2:] - xb[FFT_N2:]).astype(h_ref.dtype)


def _fft_spec(a5, k0, f2):
    n1n = a5.shape[2]
    kb = 8
    ctot = a5.shape[4]
    return _call(
        functools.partial(_fft_spec_kernel, kb=kb), name="hy_fft_spec", grid=(n1n // kb,),
        in_specs=[pl.BlockSpec((1, 2, kb, FFT_N2, ctot), lambda i: (0, 0, i, 0, 0)),
                  pl.BlockSpec((1, ctot), lambda i: (0, 0)), pl.BlockSpec(f2.shape, lambda i: (0, 0))],
        out_specs=pl.BlockSpec((HY_ORDER, kb, 2 * FFT_N2, W_GROUP), lambda i: (0, i, 0, 0)),
        out_shape=jax.ShapeDtypeStruct((HY_ORDER, n1n, 2 * FFT_N2, W_GROUP), BF16))(a5, k0, f2)


HY_SLAB = 128
PITCH_PAD = 8
HY_UNROLL = 8


def _hy_fused_kernel(x1_ref, x2_ref, v_ref, w1_ref, w2_ref, wv_ref, b1_ref, b2_ref, bv_ref, spec_ref,
                     fa_ref, fat_ref, f2_ref, f2i_ref, bias_ref, o_ref, x_scr, a_scr, z_scr, y_scr, g_scr, *, n):
    n2n, h1 = FFT_N2, n // FFT_N2
    n1n = fa_ref.shape[1] // 2
    px, pa, pz = n2n + PITCH_PAD, 2 * n1n + PITCH_PAD, 2 * n2n + PITCH_PAD
    al = lambda i: pl.multiple_of(i, 8)
    unroll_c = max(d for d in range(1, 2 * HY_UNROLL + 1) if n1n % d == 0)

    g_scr[0] = _dwconv_rows(x1_ref[0].astype(F32), w1_ref, b1_ref).astype(BF16)
    g_scr[1] = _dwconv_rows(x2_ref[0].astype(F32), w2_ref, b2_ref).astype(BF16)
    u = _dwconv_rows(v_ref[0].astype(F32), wv_ref, bv_ref)
    for i in range(h1):
        x_scr[i * px:i * px + n2n, :] = u[i * n2n:(i + 1) * n2n]

    for o in range(HY_ORDER):
        def stage_a(t, carry):
            for jj in range(HY_UNROLL):
                n2 = t * HY_UNROLL + jj
                xs = x_scr[pl.ds(n2, h1, stride=px), :].astype(BF16)
                a_scr[pl.ds(al(n2 * pa), 2 * n1n), :] = _dot(fa_ref[n2], xs)
            return carry

        def stage_c(t, carry):
            for jj in range(unroll_c):
                k1 = t * unroll_c + jj
                a = jnp.concatenate([a_scr[pl.ds(k1, n2n, stride=pa), :],
                                     a_scr[pl.ds(n1n + k1, n2n, stride=pa), :]], axis=0).astype(BF16)
                x = _dot(f2_ref[...], a)
                xr, xi = x[0:n2n], x[n2n:]
                hr = spec_ref[o, k1, 0:n2n, :].astype(F32)
                hi = spec_ref[o, k1, n2n:, :].astype(F32)
                y = jnp.concatenate([xr * hr - xi * hi, xr * hi + xi * hr], axis=0).astype(BF16)
                z_scr[pl.ds(al(k1 * pz), 2 * n2n), :] = _dot(f2i_ref[...], y)
            return carry

        def stage_a_inv(t, carry):
            for jj in range(HY_UNROLL):
                n2 = t * HY_UNROLL + jj
                z = jnp.concatenate([z_scr[pl.ds(n2, n1n, stride=pz), :],
                                     z_scr[pl.ds(n2n + n2, n1n, stride=pz), :]], axis=0).astype(BF16)
                y_scr[pl.ds(al(n2 * px), h1), :] = _dot(fat_ref[n2], z)
            return carry

        def gate(t, carry):
            for jj in range(HY_UNROLL):
                n1 = t * HY_UNROLL + jj
                conv = y_scr[pl.ds(n1, n2n, stride=px), :]
                r0 = al(n1 * px)
                g = g_scr[o, pl.ds(pl.multiple_of(n1 * n2n, n2n), n2n), :].astype(F32)
                zn = g * (conv + x_scr[pl.ds(r0, n2n), :] * bias_ref[o:o + 1, :])
                if o + 1 < HY_ORDER:
                    x_scr[pl.ds(r0, n2n), :] = zn
                else:
                    o_ref[0, pl.ds(pl.multiple_of(n1 * n2n, n2n), n2n), :] = zn.astype(o_ref.dtype)
            return carry

        lax.fori_loop(0, n2n // HY_UNROLL, stage_a, 0)
        lax.fori_loop(0, n1n // unroll_c, stage_c, 0)
        lax.fori_loop(0, n2n // HY_UNROLL, stage_a_inv, 0)
        lax.fori_loop(0, h1 // HY_UNROLL, gate, 0)


def _hyena_lat(p, prm):
    conv_w, conv_b, w1, b1, w2, b2, w3, freq, deltas, bias = prm
    bsz, n, ch = p.shape
    c = W_GROUP
    h1 = n // FFT_N2
    fa, fat, f2, f2i = _fft_consts(n)
    n1n = fa.shape[1] // 2
    kf = _hy_filter(n, (w1, b1, w2, b2, w3, freq, deltas))
    ak = _fft_a(kf.reshape(h1, FFT_N2 * 4 * c), fa, 4 * c)
    spec = _fft_spec(ak.reshape(1, 2, n1n, FFT_N2, 4 * c), kf[0:1], f2)
    ns = c // HY_SLAB
    px, pa, pz = FFT_N2 + PITCH_PAD, 2 * n1n + PITCH_PAD, 2 * FFT_N2 + PITCH_PAD
    once = pl.Buffered(1)
    part = lambda a, k, rows: pl.BlockSpec((rows, HY_SLAB), lambda s, b: (0, k * ns + s))
    data = lambda k: pl.BlockSpec((1, n, HY_SLAB), lambda s, b: (b, 0, k * ns + s))
    full = lambda a: pl.BlockSpec(a.shape, lambda s, b: (0,) * a.ndim, pipeline_mode=once)
    cb = conv_b.reshape(1, ch)
    return _call(
        functools.partial(_hy_fused_kernel, n=n), name="hy_fused", grid=(ns, bsz),
        in_specs=[data(0), data(1), data(2), part(conv_w, 0, 3), part(conv_w, 1, 3), part(conv_w, 2, 3),
                  part(cb, 0, 1), part(cb, 1, 1), part(cb, 2, 1),
                  pl.BlockSpec((HY_ORDER, n1n, 2 * FFT_N2, HY_SLAB), lambda s, b: (0, 0, 0, s), pipeline_mode=once),
                  full(fa), full(fat), full(f2), full(f2i),
                  pl.BlockSpec((HY_ORDER, HY_SLAB), lambda s, b: (0, s))],
        out_specs=pl.BlockSpec((1, n, HY_SLAB), lambda s, b: (b, 0, s)),
        out_shape=jax.ShapeDtypeStruct((bsz, n, c), BF16),
        scratch_shapes=[pltpu.VMEM((h1 * px, HY_SLAB), F32), pltpu.VMEM((FFT_N2 * pa, HY_SLAB), F32),
                        pltpu.VMEM((n1n * pz, HY_SLAB), F32), pltpu.VMEM((FFT_N2 * px, HY_SLAB), F32),
                        pltpu.VMEM((HY_ORDER, n, HY_SLAB), BF16)])(
        p, p, p, conv_w, conv_w, conv_w, cb, cb, cb, spec, fa, fat, f2, f2i, bias)


def _hy_ctx_kernel(p_ref, cw_ref, cb_ref, k_ref, fd_ref, fdi_ref, bias_ref, o_ref):
    c = W_GROUP
    u = _dwconv_rows(p_ref[0].astype(F32), cw_ref, cb_ref)
    nf = fd_ref.shape[0] // 2
    xk = _dot(fd_ref[...], k_ref[...])
    z = u[:, 2 * c:]
    for o, gate in enumerate((u[:, 0:c], u[:, c:2 * c])):
        kf, kb = xk[:, 2 * o * c:(2 * o + 1) * c], xk[:, (2 * o + 1) * c:(2 * o + 2) * c]
        k0 = k_ref[0:1, (2 * o + 1) * c:(2 * o + 2) * c].astype(F32)
        hr = kf[0:nf] + kb[0:nf] - k0
        hi = kf[nf:] - kb[nf:]
        x = _dot(fd_ref[...], z.astype(BF16))
        xr, xi = x[0:nf], x[nf:]
        y = jnp.concatenate([xr * hr - xi * hi, xr * hi + xi * hr], axis=0).astype(BF16)
        z = gate * (_dot(fdi_ref[...], y) + z * bias_ref[o:o + 1, :])
    o_ref[0] = z.astype(o_ref.dtype)


def _hyena_ctx(p, prm):
    conv_w, conv_b, w1, b1, w2, b2, w3, freq, deltas, bias = prm
    bsz, n, ch = p.shape
    kf = _hy_filter(n, (w1, b1, w2, b2, w3, freq, deltas))
    nn = 2 * n
    th = 2.0 * np.pi * ((np.arange(nn)[:, None] * np.arange(n)[None, :]) % nn) / nn
    fd = np.concatenate([np.cos(th), -np.sin(th)], axis=0)
    fdj = jnp.asarray(fd, F32).astype(BF16)
    fdi = jnp.asarray(fd.T / nn, F32).astype(BF16)
    ins = [p, conv_w, conv_b.reshape(1, ch), kf, fdj, fdi, bias]
    full = lambda a: pl.BlockSpec(a.shape, lambda b: (0,) * a.ndim)
    return _call(
        _hy_ctx_kernel, name="hy_ctx", grid=(bsz,),
        in_specs=[pl.BlockSpec((1, n, ch), lambda b: (b, 0, 0))] + [full(a) for a in ins[1:]],
        out_specs=pl.BlockSpec((1, n, W_GROUP), lambda b: (b, 0, 0)),
        out_shape=jax.ShapeDtypeStruct((bsz, n, W_GROUP), BF16))(*ins)


def _outproj_kernel(y0_ref, y1_ref, y2_ref, y3_ref, w_ref, x_ref, mod_ref, g_ref, o_ref):
    c = W_GROUP
    y = (_dot(y0_ref[0], w_ref[0:c, :]) + _dot(y1_ref[0], w_ref[c:2 * c, :])
         + _dot(y2_ref[0], w_ref[2 * c:3 * c, :]) + _dot(y3_ref[0], w_ref[3 * c:, :]))
    o_ref[0] = x_ref[0] + mod_ref[0, 2:3, :] * _rms(y, g_ref[...])


def _outproj(ys, w, x, mod, g):
    bsz, n, _ = x.shape
    tm = min(512, n)
    row = lambda width: pl.BlockSpec((1, tm, width), lambda b, i: (b, i, 0))
    return _call(
        _outproj_kernel, name="outproj", grid=(bsz, n // tm),
        in_specs=[row(W_GROUP)] * 4 + [pl.BlockSpec(w.shape, lambda b, i: (0, 0)), row(D_MODEL),
                                       pl.BlockSpec((1, 6, D_MODEL), lambda b, i: (b, 0, 0)),
                                       pl.BlockSpec((1, D_MODEL), lambda b, i: (0, 0))],
        out_specs=row(D_MODEL), out_shape=jax.ShapeDtypeStruct(x.shape, F32))(*ys, w, x, mod, g)


def _ffn_kernel(xp_ref, x_ref, xn_ref, mod_ref, g2_ref, g3_ref, wup_ref, cw_ref, cb_ref, wdn_ref, perm_ref, permt_ref,
                o_ref, h_scr, u_scr, acc_scr, *, tm, nchunk):
    i, nb = pl.program_id(1), pl.num_programs(1)
    shift, scale = mod_ref[0, 3:4, :], mod_ref[0, 4:5, :]
    pre = lambda x: _rms(x, g2_ref[...]) * (1.0 + scale) + shift
    h_scr[0:tm, :] = _dot(perm_ref[...], pre(x_ref[0]).astype(BF16)).astype(BF16)
    before = pre(xp_ref[0])[7:8] * (i > 0).astype(F32)
    after = pre(xn_ref[0])[0:1] * (i < nb - 1).astype(F32)
    hrow = lax.broadcasted_iota(jnp.int32, (16, D_MODEL), 0)
    h_scr[tm:, :] = jnp.where(hrow == 0, before, jnp.where(hrow == 1, after, 0.0)).astype(BF16)
    sub = lax.broadcasted_iota(jnp.int32, (8, 2 * FF_CHUNK), 0)

    def cols(c):
        return (slice(c * FF_CHUNK, (c + 1) * FF_CHUNK), slice(D_FF + c * FF_CHUNK, D_FF + (c + 1) * FF_CHUNK))

    def up(c):
        ga, va = cols(c)
        hb = h_scr[...]
        r = jnp.concatenate([_dot(hb, wup_ref[:, ga]), _dot(hb, wup_ref[:, va])], axis=1)
        ub = u_scr.at[c % nbuf]
        ub[8:8 + tm, :] = r[0:tm]
        ub[0:8, :] = jnp.where(sub == 0, r[tm:tm + 1], pltpu.roll(r[tm - 8:tm], 1, axis=0))
        ub[8 + tm:, :] = jnp.where(sub == 7, r[tm + 1:tm + 2], pltpu.roll(r[0:8], 7, axis=0))

    nbuf = FF_AHEAD + 1
    for c in range(min(FF_AHEAD, nchunk)):
        up(c)
    for c in range(nchunk):
        ub = u_scr.at[c % nbuf]
        if c + FF_AHEAD < nchunk:
            up(c + FF_AHEAD)
        ga, va = cols(c)
        cw = jnp.concatenate([cw_ref[:, ga], cw_ref[:, va]], axis=1)
        cb = jnp.concatenate([cb_ref[:, ga], cb_ref[:, va]], axis=1)
        u = ub[0:tm, :] * cw[0:1, :] + ub[8:8 + tm, :] * cw[1:2, :] + ub[16:16 + tm, :] * cw[2:3, :] + cb
        act = (_silu(u[:, 0:FF_CHUNK]) * u[:, FF_CHUNK:]).astype(BF16)
        if c == 0:
            acc_scr[...] = _dot(act, wdn_ref[c])
        else:
            acc_scr[...] += _dot(act, wdn_ref[c])
    branch = (mod_ref[0, 5:6, :] * _rms(acc_scr[...], g3_ref[...])).astype(BF16)
    o_ref[0] = x_ref[0] + _dot(permt_ref[...], branch)


def _ffn(x, mod, g2, g3, w_up, conv_w, conv_b, w_down):
    bsz, n, _ = x.shape
    tm = min(256, n)
    nb = n // tm
    nchunk = D_FF // FF_CHUNK
    t8 = tm // 8
    pm = np.zeros((tm, tm), np.float32)
    for j in range(t8):
        for s in range(8):
            pm[8 * j + s, s * t8 + j] = 1.0
    perm, permt = jnp.asarray(pm).astype(BF16), jnp.asarray(pm.T).astype(BF16)
    wup, cw, cb = w_up.astype(BF16), conv_w, conv_b.reshape(1, 2 * D_FF)
    wdn = w_down.reshape(nchunk, FF_CHUNK, D_MODEL).astype(BF16)
    r8 = tm // 8
    last8 = n // 8 - 1
    full = lambda a: pl.BlockSpec(a.shape, lambda b, i: (0,) * a.ndim)
    vec = pl.BlockSpec((1, D_MODEL), lambda b, i: (0, 0))
    return _call(
        functools.partial(_ffn_kernel, tm=tm, nchunk=nchunk), name="convffn", grid=(bsz, nb),
        in_specs=[pl.BlockSpec((1, 8, D_MODEL), lambda b, i: (b, jnp.maximum(i * r8 - 1, 0), 0)),
                  pl.BlockSpec((1, tm, D_MODEL), lambda b, i: (b, i, 0)),
                  pl.BlockSpec((1, 8, D_MODEL), lambda b, i: (b, jnp.minimum((i + 1) * r8, last8), 0)),
                  pl.BlockSpec((1, 6, D_MODEL), lambda b, i: (b, 0, 0)), vec, vec,
                  full(wup), full(cw), full(cb), full(wdn), full(perm), full(permt)],
        out_specs=pl.BlockSpec((1, tm, D_MODEL), lambda b, i: (b, i, 0)),
        out_shape=jax.ShapeDtypeStruct(x.shape, F32),
        scratch_shapes=[pltpu.VMEM((tm + 16, D_MODEL), BF16), pltpu.VMEM((FF_AHEAD + 1, tm + 16, 2 * FF_CHUNK), F32),
                        pltpu.VMEM((tm, D_MODEL), F32)])(x, x, x, mod, g2, g3, wup, cw, cb, wdn, perm, permt)


def kernel(x, c, ctx, c_ctx, ada_w, ada_b, norm_g, w_in, w_out, s5_lam_re, s5_lam_im, s5_log_dt, s5_b_re, s5_b_im, s5_c_re, s5_c_im, s5_d, s5_glu_w, s5_glu_b, hy_conv_w, hy_conv_b, hy_w1, hy_b1, hy_w2, hy_b2, hy_w3, hy_freq, hy_deltas, hy_bias, ret_decay_exp, ret_gn_g, mla_q_norm_g, mla_kv_norm_g, mla_w_uq, mla_w_ukv, ffn_w_up, ffn_conv_w, ffn_conv_b, ffn_w_down):
    depth = ada_w.shape[0]
    mod_lat, mod_ctx = _adaln(c, c_ctx, ada_w, ada_b)
    for l in range(depth):
        need_ctx = l < depth - 1
        ml, mc = mod_lat[l], mod_ctx[l]
        g = lambda j: norm_g[l, j].reshape(1, D_MODEL)
        w_in_l = _perm_w_in(w_in[l])
        pl_s5, pl_hy, pl_ret, pl_mla = _inproj(x, ml, g(0), w_in_l)
        pc_s5, pc_hy, pc_ret, pc_mla = _inproj(ctx, mc, g(0), w_in_l)
        s5_p = (s5_lam_re[l], s5_lam_im[l], s5_log_dt[l], s5_b_re[l], s5_b_im[l], s5_c_re[l], s5_c_im[l],
                s5_d[l], s5_glu_w[l], s5_glu_b[l])
        hy_p = (hy_conv_w[l], hy_conv_b[l], hy_w1[l], hy_b1[l], hy_w2[l], hy_b2[l], hy_w3[l], hy_freq[l],
                hy_deltas[l], hy_bias[l])
        y_s5, yc_s5 = _s5_mixer(pl_s5, pc_s5, s5_p, need_ctx)
        y_hy = _hyena_lat(pl_hy, hy_p)
        y_ret, yc_ret = _ret_mixer(pl_ret, pc_ret, (ret_decay_exp[l], ret_gn_g[l]), need_ctx)
        y_mla, yc_mla = _mla_mixer(pl_mla, pc_mla, (mla_q_norm_g[l], mla_kv_norm_g[l], mla_w_uq[l], mla_w_ukv[l]),
                                   need_ctx)
        w_out_l = w_out[l].astype(BF16)
        ffn_p = (ffn_w_up[l], ffn_conv_w[l], ffn_conv_b[l], ffn_w_down[l])
        x = _outproj((y_s5, y_hy, y_ret, y_mla), w_out_l, x, ml, g(1))
        x = _ffn(x, ml, g(2), g(3), *ffn_p)
        if need_ctx:
            yc_hy = _hyena_ctx(pc_hy, hy_p)
            ctx = _outproj((yc_s5, yc_hy, yc_ret, yc_mla), w_out_l, ctx, mc, g(1))
            ctx = _ffn(ctx, mc, g(2), g(3), *ffn_p)
    return x
```

```python
import functools
import math

import numpy as np
import jax
import jax.numpy as jnp
from jax import lax
from jax.experimental import pallas as pl
from jax.experimental.pallas import tpu as pltpu

F32 = jnp.float32
BF16 = jnp.bfloat16
HIGHEST = lax.Precision.HIGHEST

EPS = 1e-6
D_MODEL = 1024
W_GROUP = 256
GRID_W = 64
S5_CH, S5_GROUPS, S5_STATE = 16, 16, 64
S5_COLS = S5_GROUPS * S5_STATE
S5_SUB = 16
S5_GRP = 128
HY_ORDER, HY_BANDS, HY_HIDDEN = 2, 16, 64
RET_HEADS, RET_DK = 4, 64
RET_ROPE_BASE = 10000.0
MLA_HEADS, MLA_NOPE, MLA_ROPE, MLA_V = 4, 64, 32, 64
MLA_Q_RANK, MLA_KV_RANK = 192, 128
MLA_QK = MLA_NOPE + MLA_ROPE
MLA_HEAD_PAD = 128
MLA_VROWS = MLA_V
ATTN_TQ = 512
ROPE_BASE = 10000.0
D_FF = 2816
FF_CHUNK = 256
FF_AHEAD = 2
N_IN = 2560
FFT_N2 = 64

VMEM_LIMIT_BYTES = 56 * 1024 * 1024


def _call(kernel, *, name, grid, in_specs, out_specs, out_shape, scratch_shapes=()):
    return pl.pallas_call(
        kernel, name=name, grid=grid, in_specs=in_specs, out_specs=out_specs, out_shape=out_shape,
        scratch_shapes=scratch_shapes,
        compiler_params=pltpu.CompilerParams(dimension_semantics=("arbitrary",) * len(grid),
                                             vmem_limit_bytes=VMEM_LIMIT_BYTES))


def _dot(a, b):
    return jnp.dot(a, b, preferred_element_type=F32)


def _dot_nt(a, b):
    return lax.dot_general(a, b, (((1,), (1,)), ((), ())), preferred_element_type=F32)


def _dot_tn(a, b):
    return lax.dot_general(a, b, (((0,), (0,)), ((), ())), preferred_element_type=F32)


def _rms(x, g):
    return x * lax.rsqrt(jnp.mean(x * x, axis=-1, keepdims=True) + EPS) * g


def _silu(x):
    return x * jax.nn.sigmoid(x)


def _mod_kernel(c_ref, w_ref, b_ref, o_ref):
    s = _silu(c_ref[...])
    o_ref[0] = jnp.dot(s, w_ref[0], preferred_element_type=F32, precision=HIGHEST) + b_ref[0]


def _adaln(c, c_ctx, ada_w, ada_b):
    bsz, depth, n6 = c.shape[0], ada_w.shape[0], ada_w.shape[2]
    rows = 8
    assert bsz + 1 <= rows
    cc = jnp.concatenate([c, c_ctx[None], jnp.zeros((rows - bsz - 1, D_MODEL), F32)], axis=0)
    tn = 512
    out = _call(
        _mod_kernel, name="adaln", grid=(depth, n6 // tn),
        in_specs=[pl.BlockSpec((rows, D_MODEL), lambda l, j: (0, 0)),
                  pl.BlockSpec((1, D_MODEL, tn), lambda l, j: (l, 0, j)),
                  pl.BlockSpec((1, 1, tn), lambda l, j: (l, 0, j))],
        out_specs=pl.BlockSpec((1, rows, tn), lambda l, j: (l, 0, j)),
        out_shape=jax.ShapeDtypeStruct((depth, rows, n6), F32))(cc, ada_w, ada_b.reshape(depth, 1, n6))
    mod = out.reshape(depth, rows, 6, D_MODEL)
    return mod[:, :bsz], jnp.broadcast_to(mod[:, bsz:bsz + 1], (depth, bsz, 6, D_MODEL))


def _inproj_kernel(x_ref, mod_ref, g_ref, w_ref, s5_ref, hy_ref, ret_ref, mla_ref):
    h = _rms(x_ref[0], g_ref[...]) * (1.0 + mod_ref[0, 1:2, :]) + mod_ref[0, 0:1, :]
    hb = h.astype(BF16)
    s5_ref[0] = _dot(hb, w_ref[:, 0:256]).astype(BF16)
    hy_ref[0] = _dot(hb, w_ref[:, 256:1024]).astype(BF16)
    ret_ref[0] = _dot(hb, w_ref[:, 1024:2048]).astype(BF16)
    mla_ref[0] = _dot(hb, w_ref[:, 2048:2560]).astype(BF16)


def _inproj(x, mod, g, w):
    bsz, n, _ = x.shape
    tm = min(512, n)
    row = lambda width: pl.BlockSpec((1, tm, width), lambda b, i: (b, i, 0))
    shp = lambda width: jax.ShapeDtypeStruct((bsz, n, width), BF16)
    return _call(
        _inproj_kernel, name="inproj", grid=(bsz, n // tm),
        in_specs=[row(D_MODEL), pl.BlockSpec((1, 6, D_MODEL), lambda b, i: (b, 0, 0)),
                  pl.BlockSpec((1, D_MODEL), lambda b, i: (0, 0)),
                  pl.BlockSpec((D_MODEL, N_IN), lambda b, i: (0, 0))],
        out_specs=[row(256), row(768), row(1024), row(512)],
        out_shape=[shp(256), shp(768), shp(1024), shp(512)])(x, mod, g, w)


def _perm_w_in(w):
    z = lambda k: jnp.zeros((D_MODEL, k), w.dtype)
    return jnp.concatenate([w[:, :2048], w[:, 2048:2240], z(64), w[:, 2240:2368], w[:, 2368:2400], z(96)],
                           axis=1).astype(BF16)


def _gelu_tanh(x):
    return 0.5 * x * (1.0 + jnp.tanh(math.sqrt(2.0 / math.pi) * (x + 0.044715 * (x * x * x))))


def _s5_kernel(*refs, rev, tm, final):
    if final:
        (u_ref, bblk_ref, cblk_ref, tab_ref, tri_ref, h0_ref, yprev_ref, d_ref, gw_ref, gb_ref,
         y_ref, hfin_ref, hs_scr, carry_scr) = refs
    else:
        u_ref, bblk_ref, cblk_ref, tab_ref, tri_ref, h0_ref, y_ref, hfin_ref, hs_scr, carry_scr = refs
    nc = S5_COLS

    @pl.when(pl.program_id(1) == 0)
    def _():
        carry_scr[...] = h0_ref[0]

    car_r, car_i = tab_ref[4], tab_ref[5]
    nsub = S5_GRP // S5_SUB

    def cmul(t, xr, xi):
        tr, ti = tab_ref[t][None], tab_ref[t + 1][None]
        xr3, xi3 = xr.reshape(nsub, S5_SUB, nc), xi.reshape(nsub, S5_SUB, nc)
        return ((tr * xr3 - ti * xi3).reshape(S5_GRP, nc), (tr * xi3 + ti * xr3).reshape(S5_GRP, nc))

    order = list(range(tm // S5_GRP))
    subs = list(range(S5_GRP // S5_SUB))
    if rev:
        order, subs = order[::-1], subs[::-1]
    rows = lambda g: slice(g * S5_GRP, (g + 1) * S5_GRP)

    def project(g):
        return _dot(u_ref[0, rows(g), :], bblk_ref[...])

    def cumsum(bu):
        br, bi = bu[:, 0:nc], bu[:, nc:]
        gr, gi = cmul(0, br, bi)
        return _dot(tri_ref[...], jnp.concatenate([gr, gi], axis=1).astype(BF16))

    def scan(g, cs, hr_prev, hi_prev):
        csr, csi = cs[:, 0:nc], cs[:, nc:]
        edge = 0 if rev else S5_SUB - 1
        first = slice(0, 1)
        step = first if rev else slice(S5_SUB - 1, S5_SUB)
        rmul = lambda sl, xr, xi: (car_r[sl] * xr - car_i[sl] * xi, car_r[sl] * xi + car_i[sl] * xr)
        c_r, c_i = rmul(step if rev else first, hr_prev, hi_prev)
        post_r, post_i = tab_ref[2], tab_ref[3]
        r0 = g * S5_GRP
        hr = hi = None
        for s in subs:
            a0 = s * S5_SUB
            tr = csr[a0:a0 + S5_SUB] + c_r
            ti = csi[a0:a0 + S5_SUB] + c_i
            hr = post_r * tr - post_i * ti
            hi = post_r * ti + post_i * tr
            hs_scr[r0 + a0:r0 + a0 + S5_SUB, 0:nc] = hr.astype(BF16)
            hs_scr[r0 + a0:r0 + a0 + S5_SUB, nc:] = hi.astype(BF16)
            c_r, c_i = rmul(step, tr[edge:edge + 1], ti[edge:edge + 1])
        return hr[edge:edge + 1], hi[edge:edge + 1]

    def readout(g):
        y = _dot(hs_scr[rows(g), :], cblk_ref[...])
        if final:
            yt = _gelu_tanh(yprev_ref[0, rows(g), :] + y + d_ref[...] * u_ref[0, rows(g), :].astype(F32))
            z = _dot(yt.astype(BF16), gw_ref[...]) + gb_ref[...]
            y_ref[0, rows(g), :] = (yt * jax.nn.sigmoid(z)).astype(y_ref.dtype)
        else:
            y_ref[0, rows(g), :] = y

    cr, ci = carry_scr[:, 0:nc], carry_scr[:, nc:]
    ng = len(order)
    bus = {0: project(order[0])}
    if ng > 1:
        bus[1] = project(order[1])
    css = {0: cumsum(bus.pop(0))}
    for i in range(ng):
        if i + 2 < ng:
            bus[i + 2] = project(order[i + 2])
        if i + 1 < ng:
            css[i + 1] = cumsum(bus.pop(i + 1))
        cr, ci = scan(order[i], css.pop(i), cr, ci)
        readout(order[i])
    carry = jnp.concatenate([cr, ci], axis=1)
    carry_scr[...] = carry
    hfin_ref[0] = carry


def _s5_dir(u, bblk, cblk, tab, tri, h0, rev, final=None):
    bsz, n, _ = u.shape
    tm = min(1024, n)
    nb = n // tm
    blk = (lambda i: nb - 1 - i) if rev else (lambda i: i)
    row = pl.BlockSpec((1, tm, W_GROUP), lambda b, i: (b, blk(i), 0))
    full = lambda a: pl.BlockSpec(a.shape, lambda b, i: (0,) * a.ndim)
    state = pl.BlockSpec((1, 1, 2 * S5_COLS), lambda b, i: (b, 0, 0))
    ins = [u, bblk, cblk, tab, tri, h0]
    in_specs = [row, full(bblk), full(cblk), full(tab), full(tri), state]
    if final is not None:
        ins += list(final)
        in_specs += [row] + [full(a) for a in final[1:]]
    return _call(
        functools.partial(_s5_kernel, rev=rev, tm=tm, final=final is not None),
        name="s5_bwd" if rev else "s5_fwd", grid=(bsz, nb), in_specs=in_specs,
        out_specs=[row, state],
        out_shape=[jax.ShapeDtypeStruct((bsz, n, W_GROUP), BF16 if final is not None else F32),
                   jax.ShapeDtypeStruct((bsz, 1, 2 * S5_COLS), F32)],
        scratch_shapes=[pltpu.VMEM((tm, 2 * S5_COLS), BF16), pltpu.VMEM((1, 2 * S5_COLS), F32)])(*ins)


def _s5_weights(lam_re, lam_im, log_dt, b_re, b_im, c_re, c_im):
    g, p, ch = S5_GROUPS, S5_STATE, S5_CH
    dt = jnp.exp(log_dt)[..., None]
    re_dt = (lam_re * dt).reshape(2, 1, g * p)
    im_dt = (lam_im * dt).reshape(2, 1, g * p)
    lam = lax.complex(lam_re, lam_im)
    abar = jnp.exp(lax.complex(lam_re * dt, lam_im * dt))
    bbar = ((abar - 1.0) / lam)[..., None] * lax.complex(b_re, b_im)
    eye = jnp.eye(g, dtype=F32)
    place_b = lambda a: jnp.einsum('dgpc,gh->dgchp', a, eye).reshape(2, g * ch, g * p)
    bblk = jnp.concatenate([place_b(jnp.real(bbar)), place_b(jnp.imag(bbar))], axis=2).astype(BF16)
    place_c = lambda a: jnp.einsum('dgcp,gh->dgphc', a, eye).reshape(2, g * p, g * ch)
    cblk = jnp.concatenate([place_c(c_re), place_c(-c_im)], axis=1).astype(BF16)

    def powers(k, d):
        kk = k[:, None]
        mag = jnp.exp(kk * re_dt[d])
        return [mag * jnp.cos(kk * im_dt[d]), mag * jnp.sin(kk * im_dt[d])]

    i_sub = jnp.arange(S5_SUB, dtype=F32)
    tab_f = jnp.stack(powers(-i_sub, 0) + powers(i_sub, 0) + powers(i_sub + 1.0, 0))
    tab_b = jnp.stack(powers(i_sub, 1) + powers(-i_sub, 1) + powers(S5_SUB - i_sub, 1))
    ii = np.arange(S5_GRP)
    same = (ii[:, None] // S5_SUB) == (ii[None, :] // S5_SUB)
    tri_f = jnp.asarray(same & (ii[None, :] <= ii[:, None]), F32).astype(BF16)
    tri_b = jnp.asarray(same & (ii[None, :] >= ii[:, None]), F32).astype(BF16)
    return (bblk[0], cblk[0], tab_f, tri_f), (bblk[1], cblk[1], tab_b, tri_b)


def _s5_mixer(u_lat, u_ctx, prm, need_ctx):
    (lam_re, lam_im, log_dt, b_re, b_im, c_re, c_im, d_skip, glu_w, glu_b) = prm
    wf, wb = _s5_weights(lam_re, lam_im, log_dt, b_re, b_im, c_re, c_im)
    bsz = u_lat.shape[0]
    zero = jnp.zeros((bsz, 1, 2 * S5_COLS), F32)
    fin = lambda yprev: (yprev, d_skip.reshape(1, W_GROUP), glu_w.astype(BF16), glu_b.reshape(1, W_GROUP))
    yc_f, hc_f = _s5_dir(u_ctx, *wf, zero, rev=False)
    yc, hc_b = _s5_dir(u_ctx, *wb, zero, rev=True, final=fin(yc_f))
    yl_f, _ = _s5_dir(u_lat, *wf, hc_f, rev=False)
    yl, _ = _s5_dir(u_lat, *wb, hc_b, rev=True, final=fin(yl_f))
    return yl, (yc if need_ctx else None)


def _rot_half(x, half, period):
    lane = lax.broadcasted_iota(jnp.int32, x.shape, 1)
    width = x.shape[1]
    return jnp.where((lane % period) < half, pltpu.roll(x, width - half, axis=1), pltpu.roll(x, half, axis=1))


def _ret_kernel(*refs, rev, tm, nchunks, rotate, final):
    refs = list(refs)
    q_ref, k_ref, v_ref = refs[:3]
    del refs[:3]
    if rotate:
        cos_ref, sin_ref = refs[:2]
        del refs[:2]
    lgt_ref, lgc_ref, s0_ref = refs[:3]
    del refs[:3]
    if final:
        of_ref, g_ref, gn_ref, avg_ref = refs[:4]
        del refs[:4]
    o_ref, sfin_ref, d_scr, xz_scr, s_scr = refs
    hd = RET_HEADS * RET_DK

    @pl.when((pl.program_id(0) == 0) & (pl.program_id(1) == 0))
    def _():
        ri = lax.broadcasted_iota(jnp.int32, (tm, tm), 0)
        ci = lax.broadcasted_iota(jnp.int32, (tm, tm), 1)
        diff = (ci - ri) if rev else (ri - ci)
        dpos = jnp.maximum(diff, 0).astype(F32)
        for h in range(RET_HEADS):
            d_scr[h] = jnp.where(diff >= 0, jnp.exp(dpos * lgt_ref[h]), 0.0)
        pos = lax.broadcasted_iota(jnp.int32, (tm, hd), 0).astype(F32)
        lgc = lgc_ref[...]
        xz_scr[0] = jnp.exp(((tm - pos) if rev else (pos + 1.0)) * lgc)
        xz_scr[1] = jnp.exp((pos if rev else (tm - 1.0 - pos)) * lgc)

    @pl.when(pl.program_id(1) == 0)
    def _():
        s_scr[...] = s0_ref[0]

    head = lax.broadcasted_iota(jnp.int32, (tm, hd), 1) // RET_DK
    rh = lax.broadcasted_iota(jnp.int32, (hd, hd), 0) // RET_DK
    ch = lax.broadcasted_iota(jnp.int32, (hd, hd), 1) // RET_DK
    chunks = list(range(nchunks))
    state = s_scr[...]
    for c in (chunks[::-1] if rev else chunks):
        rows = slice(c * tm, (c + 1) * tm)
        q = q_ref[0, rows, :].astype(F32)
        k = k_ref[0, rows, :].astype(F32) * (RET_DK ** -0.5)
        if rotate:
            cos, sin = cos_ref[rows, :], sin_ref[rows, :]
            q = q * cos + _rot_half(q, RET_DK // 2, RET_DK) * sin
            k = k * cos + _rot_half(k, RET_DK // 2, RET_DK) * sin
        qb, kb, v = q.astype(BF16), k.astype(BF16), v_ref[0, rows, :]
        o = _dot((q * xz_scr[0]).astype(BF16), state.astype(BF16))
        for h in range(RET_HEADS):
            m = head == h
            s = _dot_nt(jnp.where(m, qb, jnp.zeros_like(qb)), kb)
            oh = _dot((s * d_scr[h]).astype(BF16), v)
            o = o + jnp.where(m, oh, 0.0)
        ds = _dot_tn(kb, (v.astype(F32) * xz_scr[1]).astype(BF16))
        state = state * jnp.exp(tm * lgc_ref[...]) + jnp.where(rh == ch, ds, 0.0)
        if final:
            o = o + of_ref[0, rows, :]
            o2 = o * o
            hi = o2.astype(BF16)
            lo = (o2 - hi.astype(F32)).astype(BF16)
            ms = _dot(hi, avg_ref[...]) + _dot(lo, avg_ref[...])
            g = g_ref[0, rows, :].astype(F32)
            o_ref[0, rows, :] = (_silu(g) * (o * lax.rsqrt(ms + EPS) * gn_ref[...])).astype(o_ref.dtype)
        else:
            o_ref[0, rows, :] = o
    s_scr[...] = state
    sfin_ref[0] = state


def _ret_dir(p, lg, s0, rev, rope=None, final=None):
    bsz, n, _ = p.shape
    tm = min(256, n)
    nchunks = 2 if n % (2 * tm) == 0 else 1
    tb = tm * nchunks
    nb = n // tb
    hd = RET_HEADS * RET_DK
    blk = (lambda i: nb - 1 - i) if rev else (lambda i: i)
    col = lambda j: pl.BlockSpec((1, tb, hd), lambda b, i: (b, blk(i), j))
    full = lambda a: pl.BlockSpec(a.shape, lambda b, i: (0,) * a.ndim)
    state = pl.BlockSpec((1, hd, hd), lambda b, i: (b, 0, 0))
    lgt = jnp.broadcast_to(lg[:, None, None], (RET_HEADS, 1, tm))
    lgc = jnp.repeat(lg, RET_DK).reshape(1, hd)
    ins, in_specs = [p, p, p], [col(0), col(1), col(2)]
    if rope is not None:
        ins += list(rope)
        in_specs += [pl.BlockSpec((tb, hd), lambda b, i: (blk(i), 0))] * 2
    ins += [lgt, lgc, s0]
    in_specs += [full(lgt), full(lgc), state]
    if final is not None:
        of, gn = final
        avg = jnp.asarray(np.kron(np.eye(RET_HEADS), np.full((RET_DK, RET_DK), 1.0 / RET_DK)), F32).astype(BF16)
        ins += [of, p, gn.reshape(1, hd), avg]
        in_specs += [pl.BlockSpec((1, tb, hd), lambda b, i: (b, blk(i), 0)), col(3), full(gn.reshape(1, hd)), full(avg)]
    return _call(
        functools.partial(_ret_kernel, rev=rev, tm=tm, nchunks=nchunks, rotate=rope is not None,
                          final=final is not None),
        name="ret_bwd" if rev else "ret_fwd", grid=(bsz, nb), in_specs=in_specs,
        out_specs=[pl.BlockSpec((1, tb, hd), lambda b, i: (b, blk(i), 0)), state],
        out_shape=[jax.ShapeDtypeStruct((bsz, n, hd), BF16 if final is not None else F32),
                   jax.ShapeDtypeStruct((bsz, hd, hd), F32)],
        scratch_shapes=[pltpu.VMEM((RET_HEADS, tm, tm), F32), pltpu.VMEM((2, tm, hd), F32),
                        pltpu.VMEM((hd, hd), F32)])(*ins)


def _ret_mixer(p_lat, p_ctx, prm, need_ctx):
    decay_exp, gn_g = prm
    lg = jnp.log1p(-jnp.exp2(-decay_exp))
    bsz, n, _ = p_lat.shape
    hd = RET_HEADS * RET_DK
    theta = RET_ROPE_BASE ** (-jnp.linspace(0.0, 1.0, RET_DK // 2, dtype=F32))
    ang = jnp.arange(n, dtype=F32)[:, None] * theta
    cos = jnp.tile(jnp.cos(ang), (1, 2 * RET_HEADS))
    sin = jnp.tile(jnp.concatenate([-jnp.sin(ang), jnp.sin(ang)], axis=1), (1, RET_HEADS))
    zero = jnp.zeros((bsz, hd, hd), F32)
    oc_f, sc_f = _ret_dir(p_ctx, lg[0], zero, rev=False)
    yc, sc_b = _ret_dir(p_ctx, lg[1], zero, rev=True, final=(oc_f, gn_g))
    ol_f, _ = _ret_dir(p_lat, lg[0], sc_f, rev=False, rope=(cos, sin))
    yl, _ = _ret_dir(p_lat, lg[1], sc_b, rev=True, rope=(cos, sin), final=(ol_f, gn_g))
    return yl, (yc if need_ctx else None)


def _mla_prep_kernel(*refs, rotate):
    if rotate:
        (p_ref, gq_ref, gkv_ref, wq_ref, wk_ref, wvt_ref, e_ref, wqr_ref, er_ref, cos_ref, sin_ref,
         q_ref, k_ref, vt_ref) = refs
    else:
        p_ref, gq_ref, gkv_ref, wq_ref, wk_ref, wvt_ref, e_ref, q_ref, k_ref, vt_ref = refs
    p = p_ref[0]
    cq = p[:, 0:256].astype(F32)
    cqn = cq * lax.rsqrt(jnp.sum(cq * cq, axis=-1, keepdims=True) * (1.0 / MLA_Q_RANK) + EPS) * gq_ref[...]
    cqb = cqn.astype(BF16)
    q = _dot(cqb, wq_ref[...])
    ckv = p[:, 256:384].astype(F32)
    ckvb = _rms(ckv, gkv_ref[...]).astype(BF16)
    k = _dot(ckvb, wk_ref[...]) + _dot(p[:, 384:512], e_ref[...])
    if rotate:
        cos, sin = cos_ref[...], sin_ref[...]
        q = q * cos + _dot(cqb, wqr_ref[...]) * sin
        k = k * cos + _dot(p[:, 384:512], er_ref[...]) * sin
    q_ref[0] = (q * (MLA_QK ** -0.5 * math.log2(math.e))).astype(BF16)
    k_ref[0] = k.astype(BF16)
    vt_ref[0] = _dot_nt(wvt_ref[...], ckvb).astype(BF16)


def _mla_prep(p, wts, rope=None):
    bsz, n, _ = p.shape
    tm = min(512, n)
    width = MLA_HEADS * MLA_HEAD_PAD
    full = lambda a: pl.BlockSpec(a.shape, lambda b, i: (0,) * a.ndim)
    row = lambda w: pl.BlockSpec((1, tm, w), lambda b, i: (b, i, 0))
    ins = [p] + list(wts)
    in_specs = [row(512)] + [full(a) for a in wts]
    if rope is not None:
        ins += list(rope)
        in_specs += [full(a) for a in rope[:2]] + [pl.BlockSpec((tm, width), lambda b, i: (i, 0))] * 2
    return _call(
        functools.partial(_mla_prep_kernel, rotate=rope is not None), name="mla_prep", grid=(bsz, n // tm),
        in_specs=in_specs,
        out_specs=[row(width), row(width), pl.BlockSpec((1, MLA_HEADS * MLA_VROWS, tm), lambda b, i: (b, 0, i))],
        out_shape=[jax.ShapeDtypeStruct((bsz, n, width), BF16), jax.ShapeDtypeStruct((bsz, n, width), BF16),
                   jax.ShapeDtypeStruct((bsz, MLA_HEADS * MLA_VROWS, n), BF16)])(*ins)


def _kv_blocks(nk, target=1152):
    assert nk % 128 == 0
    n128 = nk // 128
    nblk = max(1, -(-nk // target))
    sizes = [(n128 // nblk + (1 if j < n128 % nblk else 0)) * 128 for j in range(nblk)]
    starts = [sum(sizes[:j]) for j in range(nblk)]
    return list(zip(starts, sizes))


def _attn_kernel(q_ref, k_ref, vt_ref, o_ref, *, blocks):
    q = q_ref[0]
    tq = q.shape[0]
    units = [(j, h) for j in range(len(blocks)) for h in range(MLA_HEADS)]

    def scores(j, h):
        k0, kn = blocks[j]
        c0 = h * MLA_HEAD_PAD
        return _dot_nt(k_ref[0, k0:k0 + kn, c0:c0 + MLA_HEAD_PAD], q[:, c0:c0 + MLA_HEAD_PAD])

    m = [jnp.full((1, tq), -jnp.inf, F32)] * MLA_HEADS
    l = [jnp.zeros((1, tq), F32)] * MLA_HEADS
    acc = [jnp.zeros((MLA_V, tq), F32)] * MLA_HEADS
    ahead = 2
    pending = [scores(*u) for u in units[:ahead]]
    for idx, (j, h) in enumerate(units):
        s = pending.pop(0)
        if idx + ahead < len(units):
            pending.append(scores(*units[idx + ahead]))
        k0, kn = blocks[j]
        m_new = jnp.maximum(m[h], jnp.max(s, axis=0, keepdims=True))
        e = jnp.exp2(s - m_new)
        alpha = jnp.exp2(m[h] - m_new)
        l[h] = alpha * l[h] + jnp.sum(e, axis=0, keepdims=True)
        m[h] = m_new
        pv = _dot(vt_ref[0, h * MLA_V:(h + 1) * MLA_V, k0:k0 + kn], e.astype(BF16))
        acc[h] = alpha * acc[h] + pv
    out_t = jnp.concatenate([acc[h] * (1.0 / l[h]) for h in range(MLA_HEADS)], axis=0)
    o_ref[0] = out_t.T.astype(o_ref.dtype)


def _attention(q, k, vt):
    bsz, n, width = q.shape
    nk = k.shape[1]
    tq = min(ATTN_TQ, n)
    return _call(
        functools.partial(_attn_kernel, blocks=_kv_blocks(nk)), name="mla_attn", grid=(bsz, n // tq),
        in_specs=[pl.BlockSpec((1, tq, width), lambda b, i: (b, i, 0)),
                  pl.BlockSpec((1, nk, width), lambda b, i: (b, 0, 0)),
                  pl.BlockSpec((1, MLA_HEADS * MLA_VROWS, nk), lambda b, i: (b, 0, 0))],
        out_specs=pl.BlockSpec((1, tq, MLA_HEADS * MLA_V), lambda b, i: (b, i, 0)),
        out_shape=jax.ShapeDtypeStruct((bsz, n, MLA_HEADS * MLA_V), BF16))(q, k, vt)


def _mla_mixer(p_lat, p_ctx, prm, need_ctx):
    q_norm_g, kv_norm_g, w_uq, w_ukv = prm
    n = p_lat.shape[1]
    width = MLA_HEADS * MLA_HEAD_PAD
    wq = jnp.pad(w_uq.reshape(MLA_Q_RANK, MLA_HEADS, MLA_QK),
                 ((0, 256 - MLA_Q_RANK), (0, 0), (0, MLA_HEAD_PAD - MLA_QK))).reshape(256, width).astype(BF16)
    wkv = w_ukv.reshape(MLA_KV_RANK, MLA_HEADS, MLA_NOPE + MLA_V)
    wk = jnp.pad(wkv[:, :, :MLA_NOPE], ((0, 0), (0, 0), (0, MLA_HEAD_PAD - MLA_NOPE))).reshape(MLA_KV_RANK, width)
    wv = wkv[:, :, MLA_NOPE:].reshape(MLA_KV_RANK, MLA_HEADS * MLA_V)
    place = np.zeros((128, width), np.float32)
    for h in range(MLA_HEADS):
        for j in range(MLA_ROPE):
            place[j, h * MLA_HEAD_PAD + MLA_NOPE + j] = 1.0
    place = jnp.asarray(place).astype(BF16)
    gq = jnp.pad(q_norm_g, (0, 256 - MLA_Q_RANK)).reshape(1, 256)
    wts = (gq, kv_norm_g.reshape(1, MLA_KV_RANK), wq, wk.astype(BF16), wv.T.astype(BF16), place)
    pos = jnp.arange(n)
    row, colp = (pos // GRID_W).astype(F32), (pos % GRID_W).astype(F32)
    n_freq = MLA_ROPE // 4
    inv = ROPE_BASE ** (-jnp.arange(n_freq, dtype=F32) / n_freq)
    ang = jnp.concatenate([row[:, None] * inv, colp[:, None] * inv], axis=-1)
    ones, zeros = jnp.ones((n, MLA_NOPE), F32), jnp.zeros((n, MLA_NOPE), F32)
    tail = MLA_HEAD_PAD - MLA_QK
    cos = jnp.concatenate([ones, jnp.cos(ang), jnp.cos(ang), jnp.ones((n, tail), F32)], axis=1)
    sin = jnp.concatenate([zeros, jnp.sin(ang), jnp.sin(ang), jnp.zeros((n, tail), F32)], axis=1)

    def rot_cols(w):
        w3 = w.reshape(w.shape[0], MLA_HEADS, MLA_HEAD_PAD)
        x1 = w3[..., MLA_NOPE:MLA_NOPE + MLA_ROPE // 2]
        x2 = w3[..., MLA_NOPE + MLA_ROPE // 2:MLA_QK]
        z = jnp.zeros_like
        return jnp.concatenate([z(w3[..., :MLA_NOPE]), -x2, x1, z(w3[..., MLA_QK:])], axis=-1).reshape(w.shape)

    rope = (rot_cols(wq), rot_cols(place), jnp.tile(cos, (1, MLA_HEADS)), jnp.tile(sin, (1, MLA_HEADS)))
    q_c, k_c, v_c = _mla_prep(p_ctx, wts)
    q_l, k_l, v_l = _mla_prep(p_lat, wts, rope)
    y_lat = _attention(q_l, jnp.concatenate([k_c, k_l], axis=1), jnp.concatenate([v_c, v_l], axis=2))
    y_ctx = _attention(q_c, k_c, v_c) if need_ctx else None
    return y_lat, y_ctx


def _hy_filter_kernel(z_ref, zt_ref, w1t_ref, b1_ref, w2t_ref, b2_ref, w3_ref, fr_ref, dl_ref, k_ref, hid_scr):
    dotp = lambda a, b: jnp.dot(a, b, preferred_element_type=F32, precision=HIGHEST)

    @pl.when(pl.program_id(0) == 0)
    def _():
        fr = fr_ref[...]
        hid = jnp.sin(fr * (dotp(w1t_ref[...], zt_ref[...]) + b1_ref[...]))
        hid = jnp.sin(fr * (dotp(w2t_ref[...], hid) + b2_ref[...]))
        hid_scr[...] = hid.T

    k = dotp(hid_scr[...], w3_ref[...]) * jnp.exp(-z_ref[:, 0:1] * jnp.abs(dl_ref[...]))
    ss = jnp.sum(k * k, axis=0, keepdims=True)
    r = lax.rsqrt(ss[:, 0:W_GROUP] + ss[:, W_GROUP:] + EPS)
    k_ref[...] = (k * jnp.concatenate([r, r], axis=1)).astype(k_ref.dtype)


def _hy_filter(n, prm):
    w1, b1, w2, b2, w3, freq, deltas = prm
    pos = jnp.arange(n, dtype=F32)
    t01 = pos / (n - 1)
    bands = jnp.linspace(1e-4, HY_BANDS - 1, HY_BANDS, dtype=F32)
    ang = (2.0 * math.pi / n) * pos[:, None] * bands[None, :]
    z = jnp.concatenate([t01[:, None], jnp.cos(ang), -jnp.sin(ang)], axis=-1)
    emb = z.shape[1]
    z = jnp.pad(z, ((0, 0), (0, 128 - emb)))
    w1p = jnp.pad(w1, ((0, 128 - emb), (0, 0)))
    col = lambda a: a.reshape(-1, 1)
    ins = [z, z.T, w1p.T, col(b1), w2.T, col(b2), w3, col(freq), deltas.reshape(1, -1)]
    full = lambda a: pl.BlockSpec(a.shape, lambda o: (0,) * a.ndim)
    ncol = w3.shape[1]
    cw = 2 * W_GROUP
    in_specs = [full(a) for a in ins]
    in_specs[6] = pl.BlockSpec((HY_HIDDEN, cw), lambda o: (0, o))
    in_specs[8] = pl.BlockSpec((1, cw), lambda o: (0, o))
    return _call(_hy_filter_kernel, name="hy_filter", grid=(ncol // cw,), in_specs=in_specs,
                 out_specs=pl.BlockSpec((n, cw), lambda o: (0, o)),
                 out_shape=jax.ShapeDtypeStruct((n, ncol), BF16),
                 scratch_shapes=[pltpu.VMEM((n, HY_HIDDEN), F32)])(*ins)


def _dwconv_rows(x, w_ref, b_ref):
    n = x.shape[0]
    r = lax.broadcasted_iota(jnp.int32, x.shape, 0)
    prev = jnp.where(r == 0, 0.0, pltpu.roll(x, 1, axis=0))
    nxt = jnp.where(r == n - 1, 0.0, pltpu.roll(x, n - 1, axis=0))
    return prev * w_ref[0:1, :] + x * w_ref[1:2, :] + nxt * w_ref[2:3, :] + b_ref[...]


def _fft_consts(n_tok):
    n = 2 * n_tok
    n2n = FFT_N2
    n1n = n // n2n
    k1h = -(-(n1n // 2 + 1) // 8) * 8
    k1 = np.arange(k1h)[None, :, None]
    n1 = np.arange(n1n // 2)[None, None, :]
    n2 = np.arange(n2n)[:, None, None]
    keep = (k1 <= n1n // 2).astype(np.float64)
    th = 2.0 * np.pi * ((k1 * (n2n * n1 + n2)) % n) / n
    fa = np.concatenate([np.cos(th) * keep, -np.sin(th) * keep], axis=1)
    pair = np.where((k1 == 0) | (k1 == n1n // 2), 1.0, 2.0)
    fat = np.transpose(fa * np.concatenate([pair, pair], axis=1), (0, 2, 1)) / n
    a = np.arange(n2n)
    t2 = 2.0 * np.pi * ((a[:, None] * a[None, :]) % n2n) / n2n
    fr, fi = np.cos(t2), -np.sin(t2)
    f2 = np.block([[fr, -fi], [fi, fr]])
    f2i = np.block([[fr, fi], [-fi, fr]])
    as_bf16 = lambda m: jnp.asarray(m, F32).astype(BF16)
    return as_bf16(fa), as_bf16(fat), as_bf16(f2), as_bf16(f2i)


def _fft_a_kernel(x_ref, fa_ref, o_ref, *, nb, cw):
    for j in range(nb):
        o_ref[:, j * cw:(j + 1) * cw] = _dot(fa_ref[j], x_ref[:, j * cw:(j + 1) * cw]).astype(o_ref.dtype)


def _fft_a(x, fa, ctot):
    h1 = x.shape[0]
    r2 = fa.shape[1]
    nb = 4
    return _call(
        functools.partial(_fft_a_kernel, nb=nb, cw=ctot), name="hy_fft_a", grid=(FFT_N2 // nb,),
        in_specs=[pl.BlockSpec((h1, nb * ctot), lambda j: (0, j)), pl.BlockSpec((nb, r2, h1), lambda j: (j, 0, 0))],
        out_specs=pl.BlockSpec((r2, nb * ctot), lambda j: (0, j)),
        out_shape=jax.ShapeDtypeStruct((r2, FFT_N2 * ctot), BF16))(x, fa)


def _fft_spec_kernel(a_ref, k0_ref, f2_ref, h_ref, *, kb):
    c = W_GROUP
    for j in range(kb):
        for o in range(HY_ORDER):
            af = jnp.concatenate([a_ref[0, 0, j, :, 2 * o * c:(2 * o + 1) * c],
                                  a_ref[0, 1, j, :, 2 * o * c:(2 * o + 1) * c]], axis=0)
            ab = jnp.concatenate([a_ref[0, 0, j, :, (2 * o + 1) * c:(2 * o + 2) * c],
                                  a_ref[0, 1, j, :, (2 * o + 1) * c:(2 * o + 2) * c]], axis=0)
            xf, xb = _dot(f2_ref[...], af), _dot(f2_ref[...], ab)
            k0 = k0_ref[:, (2 * o + 1) * c:(2 * o + 2) * c].astype(F32)
            h_ref[o, j, 0:FFT_N2, :] = (xf[0:FFT_N2] + xb[0:FFT_N2] - k0).astype(h_ref.dtype)
            h_ref[o, j, FFT_N2:, :] = (xf[FFT_N2:] - xb[FFT_N2:]).astype(h_ref.dtype)


def _fft_spec(a5, k0, f2):
    n1n = a5.shape[2]
    kb = 8
    ctot = a5.shape[4]
    return _call(
        functools.partial(_fft_spec_kernel, kb=kb), name="hy_fft_spec", grid=(n1n // kb,),
        in_specs=[pl.BlockSpec((1, 2, kb, FFT_N2, ctot), lambda i: (0, 0, i, 0, 0)),
                  pl.BlockSpec((1, ctot), lambda i: (0, 0)), pl.BlockSpec(f2.shape, lambda i: (0, 0))],
        out_specs=pl.BlockSpec((HY_ORDER, kb, 2 * FFT_N2, W_GROUP), lambda i: (0, i, 0, 0)),
        out_shape=jax.ShapeDtypeStruct((HY_ORDER, n1n, 2 * FFT_N2, W_GROUP), BF16))(a5, k0, f2)


HY_SLAB = 128
PITCH_PAD = 8
HY_UNROLL = 16


def _hy_fused_kernel(x1_ref, x2_ref, v_ref, w1_ref, w2_ref, wv_ref, b1_ref, b2_ref, bv_ref, spec_ref,
                     fa_ref, fat_ref, f2_ref, f2i_ref, bias_ref, o_ref, x_scr, a_scr, z_scr, y_scr, g_scr, *, n):
    n2n, h1 = FFT_N2, n // FFT_N2
    n1n = fa_ref.shape[1] // 2
    px, pa, pz = n2n + PITCH_PAD, 2 * n1n + PITCH_PAD, 2 * n2n + PITCH_PAD
    al = lambda i: pl.multiple_of(i, 8)
    unroll_c = max(d for d in range(1, HY_UNROLL + HY_UNROLL // 2 + 1) if n1n % d == 0)
    unroll_g = min(HY_UNROLL, h1)

    g_scr[0] = _dwconv_rows(x1_ref[0].astype(F32), w1_ref, b1_ref).astype(BF16)
    g_scr[1] = _dwconv_rows(x2_ref[0].astype(F32), w2_ref, b2_ref).astype(BF16)
    u = _dwconv_rows(v_ref[0].astype(F32), wv_ref, bv_ref)
    for i in range(h1):
        x_scr[i * px:i * px + n2n, :] = u[i * n2n:(i + 1) * n2n]

    for o in range(HY_ORDER):
        def stage_a(t, carry):
            for jj in range(HY_UNROLL):
                n2 = t * HY_UNROLL + jj
                xs = x_scr[pl.ds(n2, h1, stride=px), :].astype(BF16)
                a_scr[pl.ds(al(n2 * pa), 2 * n1n), :] = _dot(fa_ref[n2], xs)
            return carry

        def stage_c(t, carry):
            for jj in range(unroll_c):
                k1 = t * unroll_c + jj
                a = jnp.concatenate([a_scr[pl.ds(k1, n2n, stride=pa), :],
                                     a_scr[pl.ds(n1n + k1, n2n, stride=pa), :]], axis=0).astype(BF16)
                x = _dot(f2_ref[...], a)
                xr, xi = x[0:n2n], x[n2n:]
                hr = spec_ref[o, k1, 0:n2n, :].astype(F32)
                hi = spec_ref[o, k1, n2n:, :].astype(F32)
                y = jnp.concatenate([xr * hr - xi * hi, xr * hi + xi * hr], axis=0).astype(BF16)
                z_scr[pl.ds(al(k1 * pz), 2 * n2n), :] = _dot(f2i_ref[...], y)
            return carry

        def stage_a_inv(t, carry):
            for jj in range(HY_UNROLL):
                n2 = t * HY_UNROLL + jj
                z = jnp.concatenate([z_scr[pl.ds(n2, n1n, stride=pz), :],
                                     z_scr[pl.ds(n2n + n2, n1n, stride=pz), :]], axis=0).astype(BF16)
                y_scr[pl.ds(al(n2 * px), h1), :] = _dot(fat_ref[n2], z)
            return carry

        def gate(t, carry):
            for jj in range(unroll_g):
                n1 = t * unroll_g + jj
                conv = y_scr[pl.ds(n1, n2n, stride=px), :]
                r0 = al(n1 * px)
                g = g_scr[o, pl.ds(pl.multiple_of(n1 * n2n, n2n), n2n), :].astype(F32)
                zn = g * (conv + x_scr[pl.ds(r0, n2n), :] * bias_ref[o:o + 1, :])
                if o + 1 < HY_ORDER:
                    x_scr[pl.ds(r0, n2n), :] = zn
                else:
                    o_ref[0, pl.ds(pl.multiple_of(n1 * n2n, n2n), n2n), :] = zn.astype(o_ref.dtype)
            return carry

        lax.fori_loop(0, n2n // HY_UNROLL, stage_a, 0)
        lax.fori_loop(0, n1n // unroll_c, stage_c, 0)
        lax.fori_loop(0, n2n // HY_UNROLL, stage_a_inv, 0)
        lax.fori_loop(0, h1 // unroll_g, gate, 0)


def _hyena_lat(p, prm):
    conv_w, conv_b, w1, b1, w2, b2, w3, freq, deltas, bias = prm
    bsz, n, ch = p.shape
    c = W_GROUP
    h1 = n // FFT_N2
    fa, fat, f2, f2i = _fft_consts(n)
    n1n = fa.shape[1] // 2
    kf = _hy_filter(n, (w1, b1, w2, b2, w3, freq, deltas))
    ak = _fft_a(kf.reshape(h1, FFT_N2 * 4 * c), fa, 4 * c)
    spec = _fft_spec(ak.reshape(1, 2, n1n, FFT_N2, 4 * c), kf[0:1], f2)
    ns = c // HY_SLAB
    px, pa, pz = FFT_N2 + PITCH_PAD, 2 * n1n + PITCH_PAD, 2 * FFT_N2 + PITCH_PAD
    once = pl.Buffered(1)
    part = lambda a, k, rows: pl.BlockSpec((rows, HY_SLAB), lambda s, b: (0, k * ns + s))
    data = lambda k: pl.BlockSpec((1, n, HY_SLAB), lambda s, b: (b, 0, k * ns + s))
    full = lambda a: pl.BlockSpec(a.shape, lambda s, b: (0,) * a.ndim, pipeline_mode=once)
    cb = conv_b.reshape(1, ch)
    return _call(
        functools.partial(_hy_fused_kernel, n=n), name="hy_fused", grid=(ns, bsz),
        in_specs=[data(0), data(1), data(2), part(conv_w, 0, 3), part(conv_w, 1, 3), part(conv_w, 2, 3),
                  part(cb, 0, 1), part(cb, 1, 1), part(cb, 2, 1),
                  pl.BlockSpec((HY_ORDER, n1n, 2 * FFT_N2, HY_SLAB), lambda s, b: (0, 0, 0, s), pipeline_mode=once),
                  full(fa), full(fat), full(f2), full(f2i),
                  pl.BlockSpec((HY_ORDER, HY_SLAB), lambda s, b: (0, s))],
        out_specs=pl.BlockSpec((1, n, HY_SLAB), lambda s, b: (b, 0, s)),
        out_shape=jax.ShapeDtypeStruct((bsz, n, c), BF16),
        scratch_shapes=[pltpu.VMEM((h1 * px, HY_SLAB), F32), pltpu.VMEM((FFT_N2 * pa, HY_SLAB), F32),
                        pltpu.VMEM((n1n * pz, HY_SLAB), F32), pltpu.VMEM((FFT_N2 * px, HY_SLAB), F32),
                        pltpu.VMEM((HY_ORDER, n, HY_SLAB), BF16)])(
        p, p, p, conv_w, conv_w, conv_w, cb, cb, cb, spec, fa, fat, f2, f2i, bias)


def _hy_ctx_kernel(p_ref, cw_ref, cb_ref, k_ref, fd_ref, fdi_ref, bias_ref, o_ref):
    c = W_GROUP
    u = _dwconv_rows(p_ref[0].astype(F32), cw_ref, cb_ref)
    nf = fd_ref.shape[0] // 2
    xk = _dot(fd_ref[...], k_ref[...])
    z = u[:, 2 * c:]
    for o, gate in enumerate((u[:, 0:c], u[:, c:2 * c])):
        kf, kb = xk[:, 2 * o * c:(2 * o + 1) * c], xk[:, (2 * o + 1) * c:(2 * o + 2) * c]
        k0 = k_ref[0:1, (2 * o + 1) * c:(2 * o + 2) * c].astype(F32)
        hr = kf[0:nf] + kb[0:nf] - k0
        hi = kf[nf:] - kb[nf:]
        x = _dot(fd_ref[...], z.astype(BF16))
        xr, xi = x[0:nf], x[nf:]
        y = jnp.concatenate([xr * hr - xi * hi, xr * hi + xi * hr], axis=0).astype(BF16)
        z = gate * (_dot(fdi_ref[...], y) + z * bias_ref[o:o + 1, :])
    o_ref[0] = z.astype(o_ref.dtype)


def _hyena_ctx(p, prm):
    conv_w, conv_b, w1, b1, w2, b2, w3, freq, deltas, bias = prm
    bsz, n, ch = p.shape
    kf = _hy_filter(n, (w1, b1, w2, b2, w3, freq, deltas))
    nn = 2 * n
    th = 2.0 * np.pi * ((np.arange(nn)[:, None] * np.arange(n)[None, :]) % nn) / nn
    fd = np.concatenate([np.cos(th), -np.sin(th)], axis=0)
    fdj = jnp.asarray(fd, F32).astype(BF16)
    fdi = jnp.asarray(fd.T / nn, F32).astype(BF16)
    ins = [p, conv_w, conv_b.reshape(1, ch), kf, fdj, fdi, bias]
    full = lambda a: pl.BlockSpec(a.shape, lambda b: (0,) * a.ndim)
    return _call(
        _hy_ctx_kernel, name="hy_ctx", grid=(bsz,),
        in_specs=[pl.BlockSpec((1, n, ch), lambda b: (b, 0, 0))] + [full(a) for a in ins[1:]],
        out_specs=pl.BlockSpec((1, n, W_GROUP), lambda b: (b, 0, 0)),
        out_shape=jax.ShapeDtypeStruct((bsz, n, W_GROUP), BF16))(*ins)


def _outproj_kernel(y0_ref, y1_ref, y2_ref, y3_ref, w_ref, x_ref, mod_ref, g_ref, o_ref):
    c = W_GROUP
    y = (_dot(y0_ref[0], w_ref[0:c, :]) + _dot(y1_ref[0], w_ref[c:2 * c, :])
         + _dot(y2_ref[0], w_ref[2 * c:3 * c, :]) + _dot(y3_ref[0], w_ref[3 * c:, :]))
    o_ref[0] = x_ref[0] + mod_ref[0, 2:3, :] * _rms(y, g_ref[...])


def _outproj(ys, w, x, mod, g):
    bsz, n, _ = x.shape
    tm = min(512, n)
    row = lambda width: pl.BlockSpec((1, tm, width), lambda b, i: (b, i, 0))
    return _call(
        _outproj_kernel, name="outproj", grid=(bsz, n // tm),
        in_specs=[row(W_GROUP)] * 4 + [pl.BlockSpec(w.shape, lambda b, i: (0, 0)), row(D_MODEL),
                                       pl.BlockSpec((1, 6, D_MODEL), lambda b, i: (b, 0, 0)),
                                       pl.BlockSpec((1, D_MODEL), lambda b, i: (0, 0))],
        out_specs=row(D_MODEL), out_shape=jax.ShapeDtypeStruct(x.shape, F32))(*ys, w, x, mod, g)


def _ffn_kernel(xp_ref, x_ref, xn_ref, mod_ref, g2_ref, g3_ref, wup_ref, cw_ref, cb_ref, wdn_ref, perm_ref, permt_ref,
                o_ref, h_scr, u_scr, acc_scr, *, tm, nchunk):
    i, nb = pl.program_id(1), pl.num_programs(1)
    shift, scale = mod_ref[0, 3:4, :], mod_ref[0, 4:5, :]
    pre = lambda x: _rms(x, g2_ref[...]) * (1.0 + scale) + shift
    h_scr[0:tm, :] = _dot(perm_ref[...], pre(x_ref[0]).astype(BF16)).astype(BF16)
    before = pre(xp_ref[0])[7:8] * (i > 0).astype(F32)
    after = pre(xn_ref[0])[0:1] * (i < nb - 1).astype(F32)
    hrow = lax.broadcasted_iota(jnp.int32, (16, D_MODEL), 0)
    h_scr[tm:, :] = jnp.where(hrow == 0, before, jnp.where(hrow == 1, after, 0.0)).astype(BF16)
    sub = lax.broadcasted_iota(jnp.int32, (8, 2 * FF_CHUNK), 0)

    def cols(c):
        return (slice(c * FF_CHUNK, (c + 1) * FF_CHUNK), slice(D_FF + c * FF_CHUNK, D_FF + (c + 1) * FF_CHUNK))

    def up(c):
        ga, va = cols(c)
        hb = h_scr[...]
        r = jnp.concatenate([_dot(hb, wup_ref[:, ga]), _dot(hb, wup_ref[:, va])], axis=1)
        ub = u_scr.at[c % nbuf]
        ub[8:8 + tm, :] = r[0:tm]
        ub[0:8, :] = jnp.where(sub == 0, r[tm:tm + 1], pltpu.roll(r[tm - 8:tm], 1, axis=0))
        ub[8 + tm:, :] = jnp.where(sub == 7, r[tm + 1:tm + 2], pltpu.roll(r[0:8], 7, axis=0))

    nbuf = FF_AHEAD + 1
    for c in range(min(FF_AHEAD, nchunk)):
        up(c)
    for c in range(nchunk):
        ub = u_scr.at[c % nbuf]
        if c + FF_AHEAD < nchunk:
            up(c + FF_AHEAD)
        ga, va = cols(c)
        cw = jnp.concatenate([cw_ref[:, ga], cw_ref[:, va]], axis=1)
        cb = jnp.concatenate([cb_ref[:, ga], cb_ref[:, va]], axis=1)
        u = ub[0:tm, :] * cw[0:1, :] + ub[8:8 + tm, :] * cw[1:2, :] + ub[16:16 + tm, :] * cw[2:3, :] + cb
        act = (_silu(u[:, 0:FF_CHUNK]) * u[:, FF_CHUNK:]).astype(BF16)
        if c == 0:
            acc_scr[...] = _dot(act, wdn_ref[c])
        else:
            acc_scr[...] += _dot(act, wdn_ref[c])
    branch = (mod_ref[0, 5:6, :] * _rms(acc_scr[...], g3_ref[...])).astype(BF16)
    o_ref[0] = x_ref[0] + _dot(permt_ref[...], branch)


def _ffn(x, mod, g2, g3, w_up, conv_w, conv_b, w_down):
    bsz, n, _ = x.shape
    tm = min(256, n)
    nb = n // tm
    nchunk = D_FF // FF_CHUNK
    t8 = tm // 8
    pm = np.zeros((tm, tm), np.float32)
    for j in range(t8):
        for s in range(8):
            pm[8 * j + s, s * t8 + j] = 1.0
    perm, permt = jnp.asarray(pm).astype(BF16), jnp.asarray(pm.T).astype(BF16)
    wup, cw, cb = w_up.astype(BF16), conv_w, conv_b.reshape(1, 2 * D_FF)
    wdn = w_down.reshape(nchunk, FF_CHUNK, D_MODEL).astype(BF16)
    r8 = tm // 8
    last8 = n // 8 - 1
    full = lambda a: pl.BlockSpec(a.shape, lambda b, i: (0,) * a.ndim)
    vec = pl.BlockSpec((1, D_MODEL), lambda b, i: (0, 0))
    return _call(
        functools.partial(_ffn_kernel, tm=tm, nchunk=nchunk), name="convffn", grid=(bsz, nb),
        in_specs=[pl.BlockSpec((1, 8, D_MODEL), lambda b, i: (b, jnp.maximum(i * r8 - 1, 0), 0)),
                  pl.BlockSpec((1, tm, D_MODEL), lambda b, i: (b, i, 0)),
                  pl.BlockSpec((1, 8, D_MODEL), lambda b, i: (b, jnp.minimum((i + 1) * r8, last8), 0)),
                  pl.BlockSpec((1, 6, D_MODEL), lambda b, i: (b, 0, 0)), vec, vec,
                  full(wup), full(cw), full(cb), full(wdn), full(perm), full(permt)],
        out_specs=pl.BlockSpec((1, tm, D_MODEL), lambda b, i: (b, i, 0)),
        out_shape=jax.ShapeDtypeStruct(x.shape, F32),
        scratch_shapes=[pltpu.VMEM((tm + 16, D_MODEL), BF16), pltpu.VMEM((FF_AHEAD + 1, tm + 16, 2 * FF_CHUNK), F32),
                        pltpu.VMEM((tm, D_MODEL), F32)])(x, x, x, mod, g2, g3, wup, cw, cb, wdn, perm, permt)


def kernel(x, c, ctx, c_ctx, ada_w, ada_b, norm_g, w_in, w_out, s5_lam_re, s5_lam_im, s5_log_dt, s5_b_re, s5_b_im, s5_c_re, s5_c_im, s5_d, s5_glu_w, s5_glu_b, hy_conv_w, hy_conv_b, hy_w1, hy_b1, hy_w2, hy_b2, hy_w3, hy_freq, hy_deltas, hy_bias, ret_decay_exp, ret_gn_g, mla_q_norm_g, mla_kv_norm_g, mla_w_uq, mla_w_ukv, ffn_w_up, ffn_conv_w, ffn_conv_b, ffn_w_down):
    depth = ada_w.shape[0]
    mod_lat, mod_ctx = _adaln(c, c_ctx, ada_w, ada_b)
    for l in range(depth):
        need_ctx = l < depth - 1
        ml, mc = mod_lat[l], mod_ctx[l]
        g = lambda j: norm_g[l, j].reshape(1, D_MODEL)
        w_in_l = _perm_w_in(w_in[l])
        pl_s5, pl_hy, pl_ret, pl_mla = _inproj(x, ml, g(0), w_in_l)
        pc_s5, pc_hy, pc_ret, pc_mla = _inproj(ctx, mc, g(0), w_in_l)
        s5_p = (s5_lam_re[l], s5_lam_im[l], s5_log_dt[l], s5_b_re[l], s5_b_im[l], s5_c_re[l], s5_c_im[l],
                s5_d[l], s5_glu_w[l], s5_glu_b[l])
        hy_p = (hy_conv_w[l], hy_conv_b[l], hy_w1[l], hy_b1[l], hy_w2[l], hy_b2[l], hy_w3[l], hy_freq[l],
                hy_deltas[l], hy_bias[l])
        y_s5, yc_s5 = _s5_mixer(pl_s5, pc_s5, s5_p, need_ctx)
        y_hy = _hyena_lat(pl_hy, hy_p)
        y_ret, yc_ret = _ret_mixer(pl_ret, pc_ret, (ret_decay_exp[l], ret_gn_g[l]), need_ctx)
        y_mla, yc_mla = _mla_mixer(pl_mla, pc_mla, (mla_q_norm_g[l], mla_kv_norm_g[l], mla_w_uq[l], mla_w_ukv[l]),
                                   need_ctx)
        w_out_l = w_out[l].astype(BF16)
        ffn_p = (ffn_w_up[l], ffn_conv_w[l], ffn_conv_b[l], ffn_w_down[l])
        x = _outproj((y_s5, y_hy, y_ret, y_mla), w_out_l, x, ml, g(1))
        x = _ffn(x, ml, g(2), g(3), *ffn_p)
        if need_ctx:
            yc_hy = _hyena_ctx(pc_hy, hy_p)
            ctx = _outproj((yc_s5, yc_hy, yc_ret, yc_mla), w_out_l, ctx, mc, g(1))
            ctx = _ffn(ctx, mc, g(2), g(3), *ffn_p)
    return x
```

```python
import functools
import math

import numpy as np
import jax
import jax.numpy as jnp
from jax import lax
from jax.experimental import pallas as pl
from jax.experimental.pallas import tpu as pltpu

F32 = jnp.float32
BF16 = jnp.bfloat16
HIGHEST = lax.Precision.HIGHEST

EPS = 1e-6
D_MODEL = 1024
W_GROUP = 256
GRID_W = 64
S5_CH, S5_GROUPS, S5_STATE = 16, 16, 64
S5_COLS = S5_GROUPS * S5_STATE
S5_SUB = 16
S5_GRP = 128
HY_ORDER, HY_BANDS, HY_HIDDEN = 2, 16, 64
RET_HEADS, RET_DK = 4, 64
RET_ROPE_BASE = 10000.0
MLA_HEADS, MLA_NOPE, MLA_ROPE, MLA_V = 4, 64, 32, 64
MLA_Q_RANK, MLA_KV_RANK = 192, 128
MLA_QK = MLA_NOPE + MLA_ROPE
MLA_HEAD_PAD = 128
MLA_VROWS = MLA_V
ATTN_TQ = 512
ROPE_BASE = 10000.0
D_FF = 2816
FF_CHUNK = 256
FF_AHEAD = 2
N_IN = 2560
FFT_N2 = 64

VMEM_LIMIT_BYTES = 56 * 1024 * 1024


def _call(kernel, *, name, grid, in_specs, out_specs, out_shape, scratch_shapes=()):
    return pl.pallas_call(
        kernel, name=name, grid=grid, in_specs=in_specs, out_specs=out_specs, out_shape=out_shape,
        scratch_shapes=scratch_shapes,
        compiler_params=pltpu.CompilerParams(dimension_semantics=("arbitrary",) * len(grid),
                                             vmem_limit_bytes=VMEM_LIMIT_BYTES))


def _dot(a, b):
    return jnp.dot(a, b, preferred_element_type=F32)


def _dot_nt(a, b):
    return lax.dot_general(a, b, (((1,), (1,)), ((), ())), preferred_element_type=F32)


def _dot_tn(a, b):
    return lax.dot_general(a, b, (((0,), (0,)), ((), ())), preferred_element_type=F32)


def _rms(x, g):
    return x * lax.rsqrt(jnp.mean(x * x, axis=-1, keepdims=True) + EPS) * g


def _silu(x):
    return x * jax.nn.sigmoid(x)


def _mod_kernel(c_ref, w_ref, b_ref, o_ref):
    split = lambda a: (a.astype(BF16), (a - a.astype(BF16).astype(F32)).astype(BF16))
    s_hi, s_lo = split(_silu(c_ref[...]))
    w_hi, w_lo = split(w_ref[0])
    o_ref[0] = _dot(s_hi, w_hi) + (_dot(s_lo, w_hi) + _dot(s_hi, w_lo)) + b_ref[0]


def _adaln(c, c_ctx, ada_w, ada_b):
    bsz, depth, n6 = c.shape[0], ada_w.shape[0], ada_w.shape[2]
    rows = 8
    assert bsz + 1 <= rows
    cc = jnp.concatenate([c, c_ctx[None], jnp.zeros((rows - bsz - 1, D_MODEL), F32)], axis=0)
    tn = 512
    out = _call(
        _mod_kernel, name="adaln", grid=(depth, n6 // tn),
        in_specs=[pl.BlockSpec((rows, D_MODEL), lambda l, j: (0, 0)),
                  pl.BlockSpec((1, D_MODEL, tn), lambda l, j: (l, 0, j)),
                  pl.BlockSpec((1, 1, tn), lambda l, j: (l, 0, j))],
        out_specs=pl.BlockSpec((1, rows, tn), lambda l, j: (l, 0, j)),
        out_shape=jax.ShapeDtypeStruct((depth, rows, n6), F32))(cc, ada_w, ada_b.reshape(depth, 1, n6))
    mod = out.reshape(depth, rows, 6, D_MODEL)
    return mod[:, :bsz], jnp.broadcast_to(mod[:, bsz:bsz + 1], (depth, bsz, 6, D_MODEL))


def _inproj_kernel(x_ref, mod_ref, g_ref, w_ref, s5_ref, hy_ref, ret_ref, mla_ref):
    h = _rms(x_ref[0], g_ref[...]) * (1.0 + mod_ref[0, 1:2, :]) + mod_ref[0, 0:1, :]
    hb = h.astype(BF16)
    s5_ref[0] = _dot(hb, w_ref[:, 0:256]).astype(BF16)
    hy_ref[0] = _dot(hb, w_ref[:, 256:1024]).astype(BF16)
    ret_ref[0] = _dot(hb, w_ref[:, 1024:2048]).astype(BF16)
    mla_ref[0] = _dot(hb, w_ref[:, 2048:2560]).astype(BF16)


def _inproj(x, mod, g, w):
    bsz, n, _ = x.shape
    tm = min(512, n)
    row = lambda width: pl.BlockSpec((1, tm, width), lambda b, i: (b, i, 0))
    shp = lambda width: jax.ShapeDtypeStruct((bsz, n, width), BF16)
    return _call(
        _inproj_kernel, name="inproj", grid=(bsz, n // tm),
        in_specs=[row(D_MODEL), pl.BlockSpec((1, 6, D_MODEL), lambda b, i: (b, 0, 0)),
                  pl.BlockSpec((1, D_MODEL), lambda b, i: (0, 0)),
                  pl.BlockSpec((D_MODEL, N_IN), lambda b, i: (0, 0))],
        out_specs=[row(256), row(768), row(1024), row(512)],
        out_shape=[shp(256), shp(768), shp(1024), shp(512)])(x, mod, g, w)


def _perm_w_in(w):
    z = lambda k: jnp.zeros((D_MODEL, k), w.dtype)
    return jnp.concatenate([w[:, :2048], w[:, 2048:2240], z(64), w[:, 2240:2368], w[:, 2368:2400], z(96)],
                           axis=1).astype(BF16)


def _gelu_tanh(x):
    return 0.5 * x * (1.0 + jnp.tanh(math.sqrt(2.0 / math.pi) * (x + 0.044715 * (x * x * x))))


def _s5_kernel(*refs, rev, tm, final):
    if final:
        (u_ref, bblk_ref, cblk_ref, tab_ref, tri_ref, h0_ref, yprev_ref, d_ref, gw_ref, gb_ref,
         y_ref, hfin_ref, hs_scr, carry_scr) = refs
    else:
        u_ref, bblk_ref, cblk_ref, tab_ref, tri_ref, h0_ref, y_ref, hfin_ref, hs_scr, carry_scr = refs
    nc = S5_COLS

    @pl.when(pl.program_id(1) == 0)
    def _():
        carry_scr[...] = h0_ref[0]

    car_r, car_i = tab_ref[4], tab_ref[5]
    nsub = S5_GRP // S5_SUB

    def cmul(t, xr, xi):
        tr, ti = tab_ref[t][None], tab_ref[t + 1][None]
        xr3, xi3 = xr.reshape(nsub, S5_SUB, nc), xi.reshape(nsub, S5_SUB, nc)
        return ((tr * xr3 - ti * xi3).reshape(S5_GRP, nc), (tr * xi3 + ti * xr3).reshape(S5_GRP, nc))

    order = list(range(tm // S5_GRP))
    subs = list(range(S5_GRP // S5_SUB))
    if rev:
        order, subs = order[::-1], subs[::-1]
    rows = lambda g: slice(g * S5_GRP, (g + 1) * S5_GRP)

    def project(g):
        return _dot(u_ref[0, rows(g), :], bblk_ref[...])

    def cumsum(bu):
        br, bi = bu[:, 0:nc], bu[:, nc:]
        gr, gi = cmul(0, br, bi)
        return _dot(tri_ref[...], jnp.concatenate([gr, gi], axis=1).astype(BF16))

    def scan(g, cs, hr_prev, hi_prev):
        csr, csi = cs[:, 0:nc], cs[:, nc:]
        edge = 0 if rev else S5_SUB - 1
        first = slice(0, 1)
        step = first if rev else slice(S5_SUB - 1, S5_SUB)
        rmul = lambda sl, xr, xi: (car_r[sl] * xr - car_i[sl] * xi, car_r[sl] * xi + car_i[sl] * xr)
        c_r, c_i = rmul(step if rev else first, hr_prev, hi_prev)
        post_r, post_i = tab_ref[2], tab_ref[3]
        r0 = g * S5_GRP
        hr = hi = None
        for s in subs:
            a0 = s * S5_SUB
            tr = csr[a0:a0 + S5_SUB] + c_r
            ti = csi[a0:a0 + S5_SUB] + c_i
            hr = post_r * tr - post_i * ti
            hi = post_r * ti + post_i * tr
            hs_scr[r0 + a0:r0 + a0 + S5_SUB, 0:nc] = hr.astype(BF16)
            hs_scr[r0 + a0:r0 + a0 + S5_SUB, nc:] = hi.astype(BF16)
            c_r, c_i = rmul(step, tr[edge:edge + 1], ti[edge:edge + 1])
        return hr[edge:edge + 1], hi[edge:edge + 1]

    def readout(g):
        y = _dot(hs_scr[rows(g), :], cblk_ref[...])
        if final:
            yt = _gelu_tanh(yprev_ref[0, rows(g), :] + y + d_ref[...] * u_ref[0, rows(g), :].astype(F32))
            z = _dot(yt.astype(BF16), gw_ref[...]) + gb_ref[...]
            y_ref[0, rows(g), :] = (yt * jax.nn.sigmoid(z)).astype(y_ref.dtype)
        else:
            y_ref[0, rows(g), :] = y

    cr, ci = carry_scr[:, 0:nc], carry_scr[:, nc:]
    ng = len(order)
    bus = {0: project(order[0])}
    if ng > 1:
        bus[1] = project(order[1])
    css = {0: cumsum(bus.pop(0))}
    for i in range(ng):
        if i + 2 < ng:
            bus[i + 2] = project(order[i + 2])
        if i + 1 < ng:
            css[i + 1] = cumsum(bus.pop(i + 1))
        cr, ci = scan(order[i], css.pop(i), cr, ci)
        readout(order[i])
    carry = jnp.concatenate([cr, ci], axis=1)
    carry_scr[...] = carry
    hfin_ref[0] = carry


def _s5_dir(u, bblk, cblk, tab, tri, h0, rev, final=None):
    bsz, n, _ = u.shape
    tm = min(1024, n)
    nb = n // tm
    blk = (lambda i: nb - 1 - i) if rev else (lambda i: i)
    row = pl.BlockSpec((1, tm, W_GROUP), lambda b, i: (b, blk(i), 0))
    full = lambda a: pl.BlockSpec(a.shape, lambda b, i: (0,) * a.ndim)
    state = pl.BlockSpec((1, 1, 2 * S5_COLS), lambda b, i: (b, 0, 0))
    ins = [u, bblk, cblk, tab, tri, h0]
    in_specs = [row, full(bblk), full(cblk), full(tab), full(tri), state]
    if final is not None:
        ins += list(final)
        in_specs += [row] + [full(a) for a in final[1:]]
    return _call(
        functools.partial(_s5_kernel, rev=rev, tm=tm, final=final is not None),
        name="s5_bwd" if rev else "s5_fwd", grid=(bsz, nb), in_specs=in_specs,
        out_specs=[row, state],
        out_shape=[jax.ShapeDtypeStruct((bsz, n, W_GROUP), BF16 if final is not None else F32),
                   jax.ShapeDtypeStruct((bsz, 1, 2 * S5_COLS), F32)],
        scratch_shapes=[pltpu.VMEM((tm, 2 * S5_COLS), BF16), pltpu.VMEM((1, 2 * S5_COLS), F32)])(*ins)


def _s5_weights(lam_re, lam_im, log_dt, b_re, b_im, c_re, c_im):
    g, p, ch = S5_GROUPS, S5_STATE, S5_CH
    dt = jnp.exp(log_dt)[..., None]
    re_dt = (lam_re * dt).reshape(2, 1, g * p)
    im_dt = (lam_im * dt).reshape(2, 1, g * p)
    lam = lax.complex(lam_re, lam_im)
    abar = jnp.exp(lax.complex(lam_re * dt, lam_im * dt))
    bbar = ((abar - 1.0) / lam)[..., None] * lax.complex(b_re, b_im)
    eye = jnp.eye(g, dtype=F32)
    place_b = lambda a: jnp.einsum('dgpc,gh->dgchp', a, eye).reshape(2, g * ch, g * p)
    bblk = jnp.concatenate([place_b(jnp.real(bbar)), place_b(jnp.imag(bbar))], axis=2).astype(BF16)
    place_c = lambda a: jnp.einsum('dgcp,gh->dgphc', a, eye).reshape(2, g * p, g * ch)
    cblk = jnp.concatenate([place_c(c_re), place_c(-c_im)], axis=1).astype(BF16)

    def powers(k, d):
        kk = k[:, None]
        mag = jnp.exp(kk * re_dt[d])
        return [mag * jnp.cos(kk * im_dt[d]), mag * jnp.sin(kk * im_dt[d])]

    i_sub = jnp.arange(S5_SUB, dtype=F32)
    tab_f = jnp.stack(powers(-i_sub, 0) + powers(i_sub, 0) + powers(i_sub + 1.0, 0))
    tab_b = jnp.stack(powers(i_sub, 1) + powers(-i_sub, 1) + powers(S5_SUB - i_sub, 1))
    ii = np.arange(S5_GRP)
    same = (ii[:, None] // S5_SUB) == (ii[None, :] // S5_SUB)
    tri_f = jnp.asarray(same & (ii[None, :] <= ii[:, None]), F32).astype(BF16)
    tri_b = jnp.asarray(same & (ii[None, :] >= ii[:, None]), F32).astype(BF16)
    return (bblk[0], cblk[0], tab_f, tri_f), (bblk[1], cblk[1], tab_b, tri_b)


def _s5_mixer(u_lat, u_ctx, prm, need_ctx):
    (lam_re, lam_im, log_dt, b_re, b_im, c_re, c_im, d_skip, glu_w, glu_b) = prm
    wf, wb = _s5_weights(lam_re, lam_im, log_dt, b_re, b_im, c_re, c_im)
    bsz = u_lat.shape[0]
    zero = jnp.zeros((bsz, 1, 2 * S5_COLS), F32)
    fin = lambda yprev: (yprev, d_skip.reshape(1, W_GROUP), glu_w.astype(BF16), glu_b.reshape(1, W_GROUP))
    yc_f, hc_f = _s5_dir(u_ctx, *wf, zero, rev=False)
    yc, hc_b = _s5_dir(u_ctx, *wb, zero, rev=True, final=fin(yc_f))
    yl_f, _ = _s5_dir(u_lat, *wf, hc_f, rev=False)
    yl, _ = _s5_dir(u_lat, *wb, hc_b, rev=True, final=fin(yl_f))
    return yl, (yc if need_ctx else None)


def _rot_half(x, half, period):
    lane = lax.broadcasted_iota(jnp.int32, x.shape, 1)
    width = x.shape[1]
    return jnp.where((lane % period) < half, pltpu.roll(x, width - half, axis=1), pltpu.roll(x, half, axis=1))


def _ret_kernel(*refs, rev, tm, nchunks, rotate, final):
    refs = list(refs)
    q_ref, k_ref, v_ref = refs[:3]
    del refs[:3]
    if rotate:
        cos_ref, sin_ref = refs[:2]
        del refs[:2]
    lgt_ref, lgc_ref, s0_ref = refs[:3]
    del refs[:3]
    if final:
        of_ref, g_ref, gn_ref, avg_ref = refs[:4]
        del refs[:4]
    o_ref, sfin_ref, d_scr, xz_scr, s_scr = refs
    hd = RET_HEADS * RET_DK

    @pl.when((pl.program_id(0) == 0) & (pl.program_id(1) == 0))
    def _():
        ri = lax.broadcasted_iota(jnp.int32, (tm, tm), 0)
        ci = lax.broadcasted_iota(jnp.int32, (tm, tm), 1)
        diff = (ci - ri) if rev else (ri - ci)
        dpos = jnp.maximum(diff, 0).astype(F32)
        for h in range(RET_HEADS):
            d_scr[h] = jnp.where(diff >= 0, jnp.exp(dpos * lgt_ref[h]), 0.0)
        pos = lax.broadcasted_iota(jnp.int32, (tm, hd), 0).astype(F32)
        lgc = lgc_ref[...]
        xz_scr[0] = jnp.exp(((tm - pos) if rev else (pos + 1.0)) * lgc)
        xz_scr[1] = jnp.exp((pos if rev else (tm - 1.0 - pos)) * lgc)

    @pl.when(pl.program_id(1) == 0)
    def _():
        s_scr[...] = s0_ref[0]

    head = lax.broadcasted_iota(jnp.int32, (tm, hd), 1) // RET_DK
    rh = lax.broadcasted_iota(jnp.int32, (hd, hd), 0) // RET_DK
    ch = lax.broadcasted_iota(jnp.int32, (hd, hd), 1) // RET_DK
    chunks = list(range(nchunks))
    state = s_scr[...]
    for c in (chunks[::-1] if rev else chunks):
        rows = slice(c * tm, (c + 1) * tm)
        q = q_ref[0, rows, :].astype(F32)
        k = k_ref[0, rows, :].astype(F32) * (RET_DK ** -0.5)
        if rotate:
            reps = hd // cos_ref.shape[1]
            cos = jnp.concatenate([cos_ref[rows, :]] * reps, axis=1)
            sin = jnp.concatenate([sin_ref[rows, :]] * reps, axis=1)
            q = q * cos + _rot_half(q, RET_DK // 2, RET_DK) * sin
            k = k * cos + _rot_half(k, RET_DK // 2, RET_DK) * sin
        qb, kb, v = q.astype(BF16), k.astype(BF16), v_ref[0, rows, :]
        o = _dot((q * xz_scr[0]).astype(BF16), state.astype(BF16))
        for h in range(RET_HEADS):
            m = head == h
            s = _dot_nt(jnp.where(m, qb, jnp.zeros_like(qb)), kb)
            oh = _dot((s * d_scr[h]).astype(BF16), v)
            o = o + jnp.where(m, oh, 0.0)
        ds = _dot_tn(kb, (v.astype(F32) * xz_scr[1]).astype(BF16))
        state = state * jnp.exp(tm * lgc_ref[...]) + jnp.where(rh == ch, ds, 0.0)
        if final:
            o = o + of_ref[0, rows, :]
            o2 = o * o
            hi = o2.astype(BF16)
            lo = (o2 - hi.astype(F32)).astype(BF16)
            ms = _dot(hi, avg_ref[...]) + _dot(lo, avg_ref[...])
            g = g_ref[0, rows, :].astype(F32)
            o_ref[0, rows, :] = (_silu(g) * (o * lax.rsqrt(ms + EPS) * gn_ref[...])).astype(o_ref.dtype)
        else:
            o_ref[0, rows, :] = o
    s_scr[...] = state
    sfin_ref[0] = state


def _ret_dir(p, lg, s0, rev, rope=None, final=None):
    bsz, n, _ = p.shape
    tm = min(256, n)
    nchunks = 2 if n % (2 * tm) == 0 else 1
    tb = tm * nchunks
    nb = n // tb
    hd = RET_HEADS * RET_DK
    blk = (lambda i: nb - 1 - i) if rev else (lambda i: i)
    col = lambda j: pl.BlockSpec((1, tb, hd), lambda b, i: (b, blk(i), j))
    full = lambda a: pl.BlockSpec(a.shape, lambda b, i: (0,) * a.ndim)
    state = pl.BlockSpec((1, hd, hd), lambda b, i: (b, 0, 0))
    lgt = jnp.broadcast_to(lg[:, None, None], (RET_HEADS, 1, tm))
    lgc = jnp.repeat(lg, RET_DK).reshape(1, hd)
    ins, in_specs = [p, p, p], [col(0), col(1), col(2)]
    if rope is not None:
        ins += list(rope)
        in_specs += [pl.BlockSpec((tb, rope[0].shape[1]), lambda b, i: (blk(i), 0))] * 2
    ins += [lgt, lgc, s0]
    in_specs += [full(lgt), full(lgc), state]
    if final is not None:
        of, gn = final
        avg = jnp.asarray(np.kron(np.eye(RET_HEADS), np.full((RET_DK, RET_DK), 1.0 / RET_DK)), F32).astype(BF16)
        ins += [of, p, gn.reshape(1, hd), avg]
        in_specs += [pl.BlockSpec((1, tb, hd), lambda b, i: (b, blk(i), 0)), col(3), full(gn.reshape(1, hd)), full(avg)]
    return _call(
        functools.partial(_ret_kernel, rev=rev, tm=tm, nchunks=nchunks, rotate=rope is not None,
                          final=final is not None),
        name="ret_bwd" if rev else "ret_fwd", grid=(bsz, nb), in_specs=in_specs,
        out_specs=[pl.BlockSpec((1, tb, hd), lambda b, i: (b, blk(i), 0)), state],
        out_shape=[jax.ShapeDtypeStruct((bsz, n, hd), BF16 if final is not None else F32),
                   jax.ShapeDtypeStruct((bsz, hd, hd), F32)],
        scratch_shapes=[pltpu.VMEM((RET_HEADS, tm, tm), F32), pltpu.VMEM((2, tm, hd), F32),
                        pltpu.VMEM((hd, hd), F32)])(*ins)


def _ret_rope_tables(n):
    theta = RET_ROPE_BASE ** (-jnp.linspace(0.0, 1.0, RET_DK // 2, dtype=F32))
    ang = jnp.arange(n, dtype=F32)[:, None] * theta
    cos = jnp.tile(jnp.cos(ang), (1, 4))
    sin = jnp.tile(jnp.concatenate([-jnp.sin(ang), jnp.sin(ang)], axis=1), (1, 2))
    return cos, sin


def _ret_mixer(p_lat, p_ctx, prm, need_ctx, rope):
    decay_exp, gn_g = prm
    lg = jnp.log1p(-jnp.exp2(-decay_exp))
    bsz, n, _ = p_lat.shape
    hd = RET_HEADS * RET_DK
    cos, sin = rope
    zero = jnp.zeros((bsz, hd, hd), F32)
    oc_f, sc_f = _ret_dir(p_ctx, lg[0], zero, rev=False)
    yc, sc_b = _ret_dir(p_ctx, lg[1], zero, rev=True, final=(oc_f, gn_g))
    ol_f, _ = _ret_dir(p_lat, lg[0], sc_f, rev=False, rope=(cos, sin))
    yl, _ = _ret_dir(p_lat, lg[1], sc_b, rev=True, rope=(cos, sin), final=(ol_f, gn_g))
    return yl, (yc if need_ctx else None)


def _mla_prep_kernel(*refs, rotate):
    if rotate:
        (p_ref, gq_ref, gkv_ref, wq_ref, wk_ref, wvt_ref, e_ref, wqr_ref, er_ref, cos_ref, sin_ref,
         q_ref, k_ref, vt_ref) = refs
    else:
        p_ref, gq_ref, gkv_ref, wq_ref, wk_ref, wvt_ref, e_ref, q_ref, k_ref, vt_ref = refs
    p = p_ref[0]
    cq = p[:, 0:256].astype(F32)
    cqn = cq * lax.rsqrt(jnp.sum(cq * cq, axis=-1, keepdims=True) * (1.0 / MLA_Q_RANK) + EPS) * gq_ref[...]
    cqb = cqn.astype(BF16)
    q = _dot(cqb, wq_ref[...])
    ckv = p[:, 256:384].astype(F32)
    ckvb = _rms(ckv, gkv_ref[...]).astype(BF16)
    k = _dot(ckvb, wk_ref[...]) + _dot(p[:, 384:512], e_ref[...])
    if rotate:
        cos = jnp.concatenate([cos_ref[...]] * MLA_HEADS, axis=1)
        sin = jnp.concatenate([sin_ref[...]] * MLA_HEADS, axis=1)
        q = q * cos + _dot(cqb, wqr_ref[...]) * sin
        k = k * cos + _dot(p[:, 384:512], er_ref[...]) * sin
    q_ref[0] = (q * (MLA_QK ** -0.5 * math.log2(math.e))).astype(BF16)
    k_ref[0] = k.astype(BF16)
    vt_ref[0] = _dot_nt(wvt_ref[...], ckvb).astype(BF16)


def _mla_prep(p, wts, rope=None):
    bsz, n, _ = p.shape
    tm = min(512, n)
    width = MLA_HEADS * MLA_HEAD_PAD
    full = lambda a: pl.BlockSpec(a.shape, lambda b, i: (0,) * a.ndim)
    row = lambda w: pl.BlockSpec((1, tm, w), lambda b, i: (b, i, 0))
    ins = [p] + list(wts)
    in_specs = [row(512)] + [full(a) for a in wts]
    if rope is not None:
        ins += list(rope)
        in_specs += [full(a) for a in rope[:2]] + [pl.BlockSpec((tm, MLA_HEAD_PAD), lambda b, i: (i, 0))] * 2
    return _call(
        functools.partial(_mla_prep_kernel, rotate=rope is not None), name="mla_prep", grid=(bsz, n // tm),
        in_specs=in_specs,
        out_specs=[row(width), row(width), pl.BlockSpec((1, MLA_HEADS * MLA_VROWS, tm), lambda b, i: (b, 0, i))],
        out_shape=[jax.ShapeDtypeStruct((bsz, n, width), BF16), jax.ShapeDtypeStruct((bsz, n, width), BF16),
                   jax.ShapeDtypeStruct((bsz, MLA_HEADS * MLA_VROWS, n), BF16)])(*ins)


def _kv_blocks(nk, target=1152):
    assert nk % 128 == 0
    n128 = nk // 128
    nblk = max(1, -(-nk // target))
    sizes = [(n128 // nblk + (1 if j < n128 % nblk else 0)) * 128 for j in range(nblk)]
    starts = [sum(sizes[:j]) for j in range(nblk)]
    return list(zip(starts, sizes))


def _attn_kernel(q_ref, *refs, blocks):
    o_ref = refs[-1]
    k_refs, vt_refs = refs[0:-1:2], refs[1:-1:2]
    q = q_ref[0]
    tq = q.shape[0]
    units = [(j, h) for j in range(len(blocks)) for h in range(MLA_HEADS)]

    def scores(j, h):
        src, k0, kn = blocks[j]
        c0 = h * MLA_HEAD_PAD
        return _dot_nt(k_refs[src][0, k0:k0 + kn, c0:c0 + MLA_HEAD_PAD], q[:, c0:c0 + MLA_HEAD_PAD])

    m = [jnp.full((1, tq), -jnp.inf, F32)] * MLA_HEADS
    l = [jnp.zeros((1, tq), F32)] * MLA_HEADS
    acc = [jnp.zeros((MLA_V, tq), F32)] * MLA_HEADS
    ahead = 2
    pending = [scores(*u) for u in units[:ahead]]
    for idx, (j, h) in enumerate(units):
        s = pending.pop(0)
        if idx + ahead < len(units):
            pending.append(scores(*units[idx + ahead]))
        src, k0, kn = blocks[j]
        m_new = jnp.maximum(m[h], jnp.max(s, axis=0, keepdims=True))
        e = jnp.exp2(s - m_new)
        alpha = jnp.exp2(m[h] - m_new)
        l[h] = alpha * l[h] + jnp.sum(e, axis=0, keepdims=True)
        m[h] = m_new
        pv = _dot(vt_refs[src][0, h * MLA_V:(h + 1) * MLA_V, k0:k0 + kn], e.astype(BF16))
        acc[h] = alpha * acc[h] + pv
    out_t = jnp.concatenate([acc[h] * (1.0 / l[h]) for h in range(MLA_HEADS)], axis=0)
    o_ref[0] = out_t.T.astype(o_ref.dtype)


def _attention(q, sources):
    bsz, n, width = q.shape
    tq = min(ATTN_TQ, n)
    blocks, ins = [], [q]
    in_specs = [pl.BlockSpec((1, tq, width), lambda b, i: (b, i, 0))]
    for src, (k, vt) in enumerate(sources):
        nk = k.shape[1]
        blocks += [(src, k0, kn) for k0, kn in _kv_blocks(nk)]
        ins += [k, vt]
        in_specs += [pl.BlockSpec((1, nk, width), lambda b, i: (b, 0, 0)),
                     pl.BlockSpec((1, MLA_HEADS * MLA_VROWS, nk), lambda b, i: (b, 0, 0))]
    return _call(
        functools.partial(_attn_kernel, blocks=blocks), name="mla_attn", grid=(bsz, n // tq),
        in_specs=in_specs, out_specs=pl.BlockSpec((1, tq, MLA_HEADS * MLA_V), lambda b, i: (b, i, 0)),
        out_shape=jax.ShapeDtypeStruct((bsz, n, MLA_HEADS * MLA_V), BF16))(*ins)


def _mla_rope_tables(n):
    pos = jnp.arange(n)
    row, colp = (pos // GRID_W).astype(F32), (pos % GRID_W).astype(F32)
    n_freq = MLA_ROPE // 4
    inv = ROPE_BASE ** (-jnp.arange(n_freq, dtype=F32) / n_freq)
    ang = jnp.concatenate([row[:, None] * inv, colp[:, None] * inv], axis=-1)
    ones, zeros = jnp.ones((n, MLA_NOPE), F32), jnp.zeros((n, MLA_NOPE), F32)
    tail = MLA_HEAD_PAD - MLA_QK
    cos = jnp.concatenate([ones, jnp.cos(ang), jnp.cos(ang), jnp.ones((n, tail), F32)], axis=1)
    sin = jnp.concatenate([zeros, jnp.sin(ang), jnp.sin(ang), jnp.zeros((n, tail), F32)], axis=1)
    return cos, sin


def _mla_mixer(p_lat, p_ctx, prm, need_ctx, rope_tabs):
    q_norm_g, kv_norm_g, w_uq, w_ukv = prm
    n = p_lat.shape[1]
    width = MLA_HEADS * MLA_HEAD_PAD
    wq = jnp.pad(w_uq.reshape(MLA_Q_RANK, MLA_HEADS, MLA_QK),
                 ((0, 256 - MLA_Q_RANK), (0, 0), (0, MLA_HEAD_PAD - MLA_QK))).reshape(256, width).astype(BF16)
    wkv = w_ukv.reshape(MLA_KV_RANK, MLA_HEADS, MLA_NOPE + MLA_V)
    wk = jnp.pad(wkv[:, :, :MLA_NOPE], ((0, 0), (0, 0), (0, MLA_HEAD_PAD - MLA_NOPE))).reshape(MLA_KV_RANK, width)
    wv = wkv[:, :, MLA_NOPE:].reshape(MLA_KV_RANK, MLA_HEADS * MLA_V)
    place = np.zeros((128, width), np.float32)
    for h in range(MLA_HEADS):
        for j in range(MLA_ROPE):
            place[j, h * MLA_HEAD_PAD + MLA_NOPE + j] = 1.0
    place = jnp.asarray(place).astype(BF16)
    gq = jnp.pad(q_norm_g, (0, 256 - MLA_Q_RANK)).reshape(1, 256)
    wts = (gq, kv_norm_g.reshape(1, MLA_KV_RANK), wq, wk.astype(BF16), wv.T.astype(BF16), place)
    cos, sin = rope_tabs

    def rot_cols(w):
        w3 = w.reshape(w.shape[0], MLA_HEADS, MLA_HEAD_PAD)
        x1 = w3[..., MLA_NOPE:MLA_NOPE + MLA_ROPE // 2]
        x2 = w3[..., MLA_NOPE + MLA_ROPE // 2:MLA_QK]
        z = jnp.zeros_like
        return jnp.concatenate([z(w3[..., :MLA_NOPE]), -x2, x1, z(w3[..., MLA_QK:])], axis=-1).reshape(w.shape)

    rope = (rot_cols(wq), rot_cols(place), cos, sin)
    q_c, k_c, v_c = _mla_prep(p_ctx, wts)
    q_l, k_l, v_l = _mla_prep(p_lat, wts, rope)
    y_lat = _attention(q_l, [(jnp.concatenate([k_c, k_l], axis=1), jnp.concatenate([v_c, v_l], axis=2))])
    y_ctx = _attention(q_c, [(k_c, v_c)]) if need_ctx else None
    return y_lat, y_ctx


def _hy_filter_kernel(z_ref, zt_ref, w1t_ref, b1_ref, w2t_ref, b2_ref, w3_ref, fr_ref, dl_ref, k_ref, hid_scr):
    dotp = lambda a, b: jnp.dot(a, b, preferred_element_type=F32, precision=HIGHEST)

    @pl.when(pl.program_id(0) == 0)
    def _():
        fr = fr_ref[...]
        hid = jnp.sin(fr * (dotp(w1t_ref[...], zt_ref[...]) + b1_ref[...]))
        hid = jnp.sin(fr * (dotp(w2t_ref[...], hid) + b2_ref[...]))
        hid_scr[...] = hid.T

    k = dotp(hid_scr[...], w3_ref[...]) * jnp.exp(-z_ref[:, 0:1] * jnp.abs(dl_ref[...]))
    ss = jnp.sum(k * k, axis=0, keepdims=True)
    r = lax.rsqrt(ss[:, 0:W_GROUP] + ss[:, W_GROUP:] + EPS)
    k_ref[...] = (k * jnp.concatenate([r, r], axis=1)).astype(k_ref.dtype)


def _hy_filter(n, prm):
    w1, b1, w2, b2, w3, freq, deltas = prm
    pos = jnp.arange(n, dtype=F32)
    t01 = pos / (n - 1)
    bands = jnp.linspace(1e-4, HY_BANDS - 1, HY_BANDS, dtype=F32)
    ang = (2.0 * math.pi / n) * pos[:, None] * bands[None, :]
    z = jnp.concatenate([t01[:, None], jnp.cos(ang), -jnp.sin(ang)], axis=-1)
    emb = z.shape[1]
    z = jnp.pad(z, ((0, 0), (0, 128 - emb)))
    w1p = jnp.pad(w1, ((0, 128 - emb), (0, 0)))
    col = lambda a: a.reshape(-1, 1)
    ins = [z, z.T, w1p.T, col(b1), w2.T, col(b2), w3, col(freq), deltas.reshape(1, -1)]
    full = lambda a: pl.BlockSpec(a.shape, lambda o: (0,) * a.ndim)
    ncol = w3.shape[1]
    cw = 2 * W_GROUP
    in_specs = [full(a) for a in ins]
    in_specs[6] = pl.BlockSpec((HY_HIDDEN, cw), lambda o: (0, o))
    in_specs[8] = pl.BlockSpec((1, cw), lambda o: (0, o))
    return _call(_hy_filter_kernel, name="hy_filter", grid=(ncol // cw,), in_specs=in_specs,
                 out_specs=pl.BlockSpec((n, cw), lambda o: (0, o)),
                 out_shape=jax.ShapeDtypeStruct((n, ncol), BF16),
                 scratch_shapes=[pltpu.VMEM((n, HY_HIDDEN), F32)])(*ins)


def _dwconv_rows(x, w_ref, b_ref):
    n = x.shape[0]
    r = lax.broadcasted_iota(jnp.int32, x.shape, 0)
    prev = jnp.where(r == 0, 0.0, pltpu.roll(x, 1, axis=0))
    nxt = jnp.where(r == n - 1, 0.0, pltpu.roll(x, n - 1, axis=0))
    return prev * w_ref[0:1, :] + x * w_ref[1:2, :] + nxt * w_ref[2:3, :] + b_ref[...]


def _fft_consts(n_tok):
    n = 2 * n_tok
    n2n = FFT_N2
    n1n = n // n2n
    k1h = -(-(n1n // 2 + 1) // 8) * 8
    k1 = np.arange(k1h)[None, :, None]
    n1 = np.arange(n1n // 2)[None, None, :]
    n2 = np.arange(n2n)[:, None, None]
    keep = (k1 <= n1n // 2).astype(np.float64)
    th = 2.0 * np.pi * ((k1 * (n2n * n1 + n2)) % n) / n
    fa = np.concatenate([np.cos(th) * keep, -np.sin(th) * keep], axis=1)
    pair = np.where((k1 == 0) | (k1 == n1n // 2), 1.0, 2.0)
    fat = np.transpose(fa * np.concatenate([pair, pair], axis=1), (0, 2, 1)) / n
    a = np.arange(n2n)
    t2 = 2.0 * np.pi * ((a[:, None] * a[None, :]) % n2n) / n2n
    fr, fi = np.cos(t2), -np.sin(t2)
    f2 = np.block([[fr, -fi], [fi, fr]])
    f2i = np.block([[fr, fi], [-fi, fr]])
    as_bf16 = lambda m: jnp.asarray(m, F32).astype(BF16)
    return as_bf16(fa), as_bf16(fat), as_bf16(f2), as_bf16(f2i)


def _fft_a_kernel(x_ref, fa_ref, o_ref, *, nb, cw):
    for j in range(nb):
        o_ref[:, j * cw:(j + 1) * cw] = _dot(fa_ref[j], x_ref[:, j * cw:(j + 1) * cw]).astype(o_ref.dtype)


def _fft_a(x, fa, ctot):
    h1 = x.shape[0]
    r2 = fa.shape[1]
    nb = 4
    return _call(
        functools.partial(_fft_a_kernel, nb=nb, cw=ctot), name="hy_fft_a", grid=(FFT_N2 // nb,),
        in_specs=[pl.BlockSpec((h1, nb * ctot), lambda j: (0, j)), pl.BlockSpec((nb, r2, h1), lambda j: (j, 0, 0))],
        out_specs=pl.BlockSpec((r2, nb * ctot), lambda j: (0, j)),
        out_shape=jax.ShapeDtypeStruct((r2, FFT_N2 * ctot), BF16))(x, fa)


def _fft_spec_kernel(a_ref, k0_ref, f2_ref, h_ref, *, kb):
    c = W_GROUP
    for j in range(kb):
        for o in range(HY_ORDER):
            af = jnp.concatenate([a_ref[0, 0, j, :, 2 * o * c:(2 * o + 1) * c],
                                  a_ref[0, 1, j, :, 2 * o * c:(2 * o + 1) * c]], axis=0)
            ab = jnp.concatenate([a_ref[0, 0, j, :, (2 * o + 1) * c:(2 * o + 2) * c],
                                  a_ref[0, 1, j, :, (2 * o + 1) * c:(2 * o + 2) * c]], axis=0)
            xf, xb = _dot(f2_ref[...], af), _dot(f2_ref[...], ab)
            k0 = k0_ref[:, (2 * o + 1) * c:(2 * o + 2) * c].astype(F32)
            h_ref[o, j, 0:FFT_N2, :] = (xf[0:FFT_N2] + xb[0:FFT_N2] - k0).astype(h_ref.dtype)
            h_ref[o, j, FFT_N2:, :] = (xf[FFT_N2:] - xb[FFT_N2:]).astype(h_ref.dtype)


def _fft_spec(a5, k0, f2):
    n1n = a5.shape[2]
    kb = 8
    ctot = a5.shape[4]
    return _call(
        functools.partial(_fft_spec_kernel, kb=kb), name="hy_fft_spec", grid=(n1n // kb,),
        in_specs=[pl.BlockSpec((1, 2, kb, FFT_N2, ctot), lambda i: (0, 0, i, 0, 0)),
                  pl.BlockSpec((1, ctot), lambda i: (0, 0)), pl.BlockSpec(f2.shape, lambda i: (0, 0))],
        out_specs=pl.BlockSpec((HY_ORDER, kb, 2 * FFT_N2, W_GROUP), lambda i: (0, i, 0, 0)),
        out_shape=jax.ShapeDtypeStruct((HY_ORDER, n1n, 2 * FFT_N2, W_GROUP), BF16))(a5, k0, f2)


HY_SLAB = 128
PITCH_PAD = 8
HY_UNROLL = 16


def _hy_fused_kernel(x1_ref, x2_ref, v_ref, w1_ref, w2_ref, wv_ref, b1_ref, b2_ref, bv_ref, spec_ref,
                     fa_ref, fat_ref, f2_ref, f2i_ref, bias_ref, o_ref, x_scr, a_scr, z_scr, y_scr, g_scr, *, n):
    n2n, h1 = FFT_N2, n // FFT_N2
    n1n = fa_ref.shape[1] // 2
    px, pa, pz = n2n + PITCH_PAD, 2 * n1n + PITCH_PAD, 2 * n2n + PITCH_PAD
    al = lambda i: pl.multiple_of(i, 8)
    unroll_c = max(d for d in range(1, HY_UNROLL + HY_UNROLL // 2 + 1) if n1n % d == 0)
    unroll_g = min(HY_UNROLL, h1)

    g_scr[0] = _dwconv_rows(x1_ref[0].astype(F32), w1_ref, b1_ref).astype(BF16)
    g_scr[1] = _dwconv_rows(x2_ref[0].astype(F32), w2_ref, b2_ref).astype(BF16)
    u = _dwconv_rows(v_ref[0].astype(F32), wv_ref, bv_ref)
    for i in range(h1):
        x_scr[i * px:i * px + n2n, :] = u[i * n2n:(i + 1) * n2n]

    for o in range(HY_ORDER):
        def stage_a(t, carry):
            for jj in range(HY_UNROLL):
                n2 = t * HY_UNROLL + jj
                xs = x_scr[pl.ds(n2, h1, stride=px), :].astype(BF16)
                a_scr[pl.ds(al(n2 * pa), 2 * n1n), :] = _dot(fa_ref[n2], xs)
            return carry

        def stage_c(t, carry):
            for jj in range(unroll_c):
                k1 = t * unroll_c + jj
                a = jnp.concatenate([a_scr[pl.ds(k1, n2n, stride=pa), :],
                                     a_scr[pl.ds(n1n + k1, n2n, stride=pa), :]], axis=0).astype(BF16)
                x = _dot(f2_ref[...], a)
                xr, xi = x[0:n2n], x[n2n:]
                hr = spec_ref[o, k1, 0:n2n, :].astype(F32)
                hi = spec_ref[o, k1, n2n:, :].astype(F32)
                y = jnp.concatenate([xr * hr - xi * hi, xr * hi + xi * hr], axis=0).astype(BF16)
                z_scr[pl.ds(al(k1 * pz), 2 * n2n), :] = _dot(f2i_ref[...], y)
            return carry

        def stage_a_inv(t, carry):
            for jj in range(HY_UNROLL):
                n2 = t * HY_UNROLL + jj
                z = jnp.concatenate([z_scr[pl.ds(n2, n1n, stride=pz), :],
                                     z_scr[pl.ds(n2n + n2, n1n, stride=pz), :]], axis=0).astype(BF16)
                y_scr[pl.ds(al(n2 * px), h1), :] = _dot(fat_ref[n2], z)
            return carry

        def gate(t, carry):
            for jj in range(unroll_g):
                n1 = t * unroll_g + jj
                conv = y_scr[pl.ds(n1, n2n, stride=px), :]
                r0 = al(n1 * px)
                g = g_scr[o, pl.ds(pl.multiple_of(n1 * n2n, n2n), n2n), :].astype(F32)
                zn = g * (conv + x_scr[pl.ds(r0, n2n), :] * bias_ref[o:o + 1, :])
                if o + 1 < HY_ORDER:
                    x_scr[pl.ds(r0, n2n), :] = zn
                else:
                    o_ref[0, pl.ds(pl.multiple_of(n1 * n2n, n2n), n2n), :] = zn.astype(o_ref.dtype)
            return carry

        lax.fori_loop(0, n2n // HY_UNROLL, stage_a, 0)
        lax.fori_loop(0, n1n // unroll_c, stage_c, 0)
        lax.fori_loop(0, n2n // HY_UNROLL, stage_a_inv, 0)
        lax.fori_loop(0, h1 // unroll_g, gate, 0)


def _hyena_lat(p, prm):
    conv_w, conv_b, w1, b1, w2, b2, w3, freq, deltas, bias = prm
    bsz, n, ch = p.shape
    c = W_GROUP
    h1 = n // FFT_N2
    fa, fat, f2, f2i = _fft_consts(n)
    n1n = fa.shape[1] // 2
    kf = _hy_filter(n, (w1, b1, w2, b2, w3, freq, deltas))
    ak = _fft_a(kf.reshape(h1, FFT_N2 * 4 * c), fa, 4 * c)
    spec = _fft_spec(ak.reshape(1, 2, n1n, FFT_N2, 4 * c), kf[0:1], f2)
    ns = c // HY_SLAB
    px, pa, pz = FFT_N2 + PITCH_PAD, 2 * n1n + PITCH_PAD, 2 * FFT_N2 + PITCH_PAD
    once = pl.Buffered(1)
    part = lambda a, k, rows: pl.BlockSpec((rows, HY_SLAB), lambda s, b: (0, k * ns + s))
    data = lambda k: pl.BlockSpec((1, n, HY_SLAB), lambda s, b: (b, 0, k * ns + s))
    full = lambda a: pl.BlockSpec(a.shape, lambda s, b: (0,) * a.ndim, pipeline_mode=once)
    cb = conv_b.reshape(1, ch)
    return _call(
        functools.partial(_hy_fused_kernel, n=n), name="hy_fused", grid=(ns, bsz),
        in_specs=[data(0), data(1), data(2), part(conv_w, 0, 3), part(conv_w, 1, 3), part(conv_w, 2, 3),
                  part(cb, 0, 1), part(cb, 1, 1), part(cb, 2, 1),
                  pl.BlockSpec((HY_ORDER, n1n, 2 * FFT_N2, HY_SLAB), lambda s, b: (0, 0, 0, s), pipeline_mode=once),
                  full(fa), full(fat), full(f2), full(f2i),
                  pl.BlockSpec((HY_ORDER, HY_SLAB), lambda s, b: (0, s))],
        out_specs=pl.BlockSpec((1, n, HY_SLAB), lambda s, b: (b, 0, s)),
        out_shape=jax.ShapeDtypeStruct((bsz, n, c), BF16),
        scratch_shapes=[pltpu.VMEM((h1 * px, HY_SLAB), F32), pltpu.VMEM((FFT_N2 * pa, HY_SLAB), F32),
                        pltpu.VMEM((n1n * pz, HY_SLAB), F32), pltpu.VMEM((FFT_N2 * px, HY_SLAB), F32),
                        pltpu.VMEM((HY_ORDER, n, HY_SLAB), BF16)])(
        p, p, p, conv_w, conv_w, conv_w, cb, cb, cb, spec, fa, fat, f2, f2i, bias)


def _hy_ctx_kernel(p_ref, cw_ref, cb_ref, k_ref, fd_ref, fdi_ref, bias_ref, o_ref):
    c = W_GROUP
    u = _dwconv_rows(p_ref[0].astype(F32), cw_ref, cb_ref)
    nf = fd_ref.shape[0] // 2
    xk = _dot(fd_ref[...], k_ref[...])
    z = u[:, 2 * c:]
    for o, gate in enumerate((u[:, 0:c], u[:, c:2 * c])):
        kf, kb = xk[:, 2 * o * c:(2 * o + 1) * c], xk[:, (2 * o + 1) * c:(2 * o + 2) * c]
        k0 = k_ref[0:1, (2 * o + 1) * c:(2 * o + 2) * c].astype(F32)
        hr = kf[0:nf] + kb[0:nf] - k0
        hi = kf[nf:] - kb[nf:]
        x = _dot(fd_ref[...], z.astype(BF16))
        xr, xi = x[0:nf], x[nf:]
        y = jnp.concatenate([xr * hr - xi * hi, xr * hi + xi * hr], axis=0).astype(BF16)
        z = gate * (_dot(fdi_ref[...], y) + z * bias_ref[o:o + 1, :])
    o_ref[0] = z.astype(o_ref.dtype)


def _hyena_ctx(p, prm):
    conv_w, conv_b, w1, b1, w2, b2, w3, freq, deltas, bias = prm
    bsz, n, ch = p.shape
    kf = _hy_filter(n, (w1, b1, w2, b2, w3, freq, deltas))
    nn = 2 * n
    th = 2.0 * np.pi * ((np.arange(nn)[:, None] * np.arange(n)[None, :]) % nn) / nn
    fd = np.concatenate([np.cos(th), -np.sin(th)], axis=0)
    fdj = jnp.asarray(fd, F32).astype(BF16)
    fdi = jnp.asarray(fd.T / nn, F32).astype(BF16)
    ins = [p, conv_w, conv_b.reshape(1, ch), kf, fdj, fdi, bias]
    full = lambda a: pl.BlockSpec(a.shape, lambda b: (0,) * a.ndim)
    return _call(
        _hy_ctx_kernel, name="hy_ctx", grid=(bsz,),
        in_specs=[pl.BlockSpec((1, n, ch), lambda b: (b, 0, 0))] + [full(a) for a in ins[1:]],
        out_specs=pl.BlockSpec((1, n, W_GROUP), lambda b: (b, 0, 0)),
        out_shape=jax.ShapeDtypeStruct((bsz, n, W_GROUP), BF16))(*ins)


def _outproj_kernel(y0_ref, y1_ref, y2_ref, y3_ref, w_ref, x_ref, mod_ref, g_ref, o_ref):
    c = W_GROUP
    y = (_dot(y0_ref[0], w_ref[0:c, :]) + _dot(y1_ref[0], w_ref[c:2 * c, :])
         + _dot(y2_ref[0], w_ref[2 * c:3 * c, :]) + _dot(y3_ref[0], w_ref[3 * c:, :]))
    o_ref[0] = x_ref[0] + mod_ref[0, 2:3, :] * _rms(y, g_ref[...])


def _outproj(ys, w, x, mod, g):
    bsz, n, _ = x.shape
    tm = min(512, n)
    row = lambda width: pl.BlockSpec((1, tm, width), lambda b, i: (b, i, 0))
    return _call(
        _outproj_kernel, name="outproj", grid=(bsz, n // tm),
        in_specs=[row(W_GROUP)] * 4 + [pl.BlockSpec(w.shape, lambda b, i: (0, 0)), row(D_MODEL),
                                       pl.BlockSpec((1, 6, D_MODEL), lambda b, i: (b, 0, 0)),
                                       pl.BlockSpec((1, D_MODEL), lambda b, i: (0, 0))],
        out_specs=row(D_MODEL), out_shape=jax.ShapeDtypeStruct(x.shape, F32))(*ys, w, x, mod, g)


def _ffn_kernel(xp_ref, x_ref, xn_ref, mod_ref, g2_ref, g3_ref, wup_ref, cw_ref, cb_ref, wdn_ref, perm_ref, permt_ref,
                o_ref, h_scr, u_scr, acc_scr, *, tm, nchunk):
    i, nb = pl.program_id(1), pl.num_programs(1)
    shift, scale = mod_ref[0, 3:4, :], mod_ref[0, 4:5, :]
    pre = lambda x: _rms(x, g2_ref[...]) * (1.0 + scale) + shift
    h_scr[0:tm, :] = _dot(perm_ref[...], pre(x_ref[0]).astype(BF16)).astype(BF16)
    before = pre(xp_ref[0])[7:8] * (i > 0).astype(F32)
    after = pre(xn_ref[0])[0:1] * (i < nb - 1).astype(F32)
    hrow = lax.broadcasted_iota(jnp.int32, (16, D_MODEL), 0)
    h_scr[tm:, :] = jnp.where(hrow == 0, before, jnp.where(hrow == 1, after, 0.0)).astype(BF16)
    sub = lax.broadcasted_iota(jnp.int32, (8, 2 * FF_CHUNK), 0)

    def cols(c):
        return (slice(c * FF_CHUNK, (c + 1) * FF_CHUNK), slice(D_FF + c * FF_CHUNK, D_FF + (c + 1) * FF_CHUNK))

    def up(c):
        ga, va = cols(c)
        hb = h_scr[...]
        r = jnp.concatenate([_dot(hb, wup_ref[:, ga]), _dot(hb, wup_ref[:, va])], axis=1)
        ub = u_scr.at[c % nbuf]
        ub[8:8 + tm, :] = r[0:tm]
        ub[0:8, :] = jnp.where(sub == 0, r[tm:tm + 1], pltpu.roll(r[tm - 8:tm], 1, axis=0))
        ub[8 + tm:, :] = jnp.where(sub == 7, r[tm + 1:tm + 2], pltpu.roll(r[0:8], 7, axis=0))

    nbuf = FF_AHEAD + 1
    for c in range(min(FF_AHEAD, nchunk)):
        up(c)
    for c in range(nchunk):
        ub = u_scr.at[c % nbuf]
        if c + FF_AHEAD < nchunk:
            up(c + FF_AHEAD)
        ga, va = cols(c)
        cw = jnp.concatenate([cw_ref[:, ga], cw_ref[:, va]], axis=1)
        cb = jnp.concatenate([cb_ref[:, ga], cb_ref[:, va]], axis=1)
        u = ub[0:tm, :] * cw[0:1, :] + ub[8:8 + tm, :] * cw[1:2, :] + ub[16:16 + tm, :] * cw[2:3, :] + cb
        act = (_silu(u[:, 0:FF_CHUNK]) * u[:, FF_CHUNK:]).astype(BF16)
        if c == 0:
            acc_scr[...] = _dot(act, wdn_ref[c])
        else:
            acc_scr[...] += _dot(act, wdn_ref[c])
    branch = (mod_ref[0, 5:6, :] * _rms(acc_scr[...], g3_ref[...])).astype(BF16)
    o_ref[0] = x_ref[0] + _dot(permt_ref[...], branch)


def _ffn(x, mod, g2, g3, w_up, conv_w, conv_b, w_down):
    bsz, n, _ = x.shape
    tm = min(256, n)
    nb = n // tm
    nchunk = D_FF // FF_CHUNK
    t8 = tm // 8
    pm = np.zeros((tm, tm), np.float32)
    for j in range(t8):
        for s in range(8):
            pm[8 * j + s, s * t8 + j] = 1.0
    perm, permt = jnp.asarray(pm).astype(BF16), jnp.asarray(pm.T).astype(BF16)
    wup, cw, cb = w_up.astype(BF16), conv_w, conv_b.reshape(1, 2 * D_FF)
    wdn = w_down.reshape(nchunk, FF_CHUNK, D_MODEL).astype(BF16)
    r8 = tm // 8
    last8 = n // 8 - 1
    full = lambda a: pl.BlockSpec(a.shape, lambda b, i: (0,) * a.ndim)
    vec = pl.BlockSpec((1, D_MODEL), lambda b, i: (0, 0))
    return _call(
        functools.partial(_ffn_kernel, tm=tm, nchunk=nchunk), name="convffn", grid=(bsz, nb),
        in_specs=[pl.BlockSpec((1, 8, D_MODEL), lambda b, i: (b, jnp.maximum(i * r8 - 1, 0), 0)),
                  pl.BlockSpec((1, tm, D_MODEL), lambda b, i: (b, i, 0)),
                  pl.BlockSpec((1, 8, D_MODEL), lambda b, i: (b, jnp.minimum((i + 1) * r8, last8), 0)),
                  pl.BlockSpec((1, 6, D_MODEL), lambda b, i: (b, 0, 0)), vec, vec,
                  full(wup), full(cw), full(cb), full(wdn), full(perm), full(permt)],
        out_specs=pl.BlockSpec((1, tm, D_MODEL), lambda b, i: (b, i, 0)),
        out_shape=jax.ShapeDtypeStruct(x.shape, F32),
        scratch_shapes=[pltpu.VMEM((tm + 16, D_MODEL), BF16), pltpu.VMEM((FF_AHEAD + 1, tm + 16, 2 * FF_CHUNK), F32),
                        pltpu.VMEM((tm, D_MODEL), F32)])(x, x, x, mod, g2, g3, wup, cw, cb, wdn, perm, permt)


def kernel(x, c, ctx, c_ctx, ada_w, ada_b, norm_g, w_in, w_out, s5_lam_re, s5_lam_im, s5_log_dt, s5_b_re, s5_b_im, s5_c_re, s5_c_im, s5_d, s5_glu_w, s5_glu_b, hy_conv_w, hy_conv_b, hy_w1, hy_b1, hy_w2, hy_b2, hy_w3, hy_freq, hy_deltas, hy_bias, ret_decay_exp, ret_gn_g, mla_q_norm_g, mla_kv_norm_g, mla_w_uq, mla_w_ukv, ffn_w_up, ffn_conv_w, ffn_conv_b, ffn_w_down):
    depth = ada_w.shape[0]
    mod_lat, mod_ctx = _adaln(c, c_ctx, ada_w, ada_b)
    ret_rope, mla_rope = _ret_rope_tables(x.shape[1]), _mla_rope_tables(x.shape[1])
    for l in range(depth):
        need_ctx = l < depth - 1
        ml, mc = mod_lat[l], mod_ctx[l]
        g = lambda j: norm_g[l, j].reshape(1, D_MODEL)
        w_in_l = _perm_w_in(w_in[l])
        pl_s5, pl_hy, pl_ret, pl_mla = _inproj(x, ml, g(0), w_in_l)
        pc_s5, pc_hy, pc_ret, pc_mla = _inproj(ctx, mc, g(0), w_in_l)
        s5_p = (s5_lam_re[l], s5_lam_im[l], s5_log_dt[l], s5_b_re[l], s5_b_im[l], s5_c_re[l], s5_c_im[l],
                s5_d[l], s5_glu_w[l], s5_glu_b[l])
        hy_p = (hy_conv_w[l], hy_conv_b[l], hy_w1[l], hy_b1[l], hy_w2[l], hy_b2[l], hy_w3[l], hy_freq[l],
                hy_deltas[l], hy_bias[l])
        y_s5, yc_s5 = _s5_mixer(pl_s5, pc_s5, s5_p, need_ctx)
        y_hy = _hyena_lat(pl_hy, hy_p)
        y_ret, yc_ret = _ret_mixer(pl_ret, pc_ret, (ret_decay_exp[l], ret_gn_g[l]), need_ctx, ret_rope)
        y_mla, yc_mla = _mla_mixer(pl_mla, pc_mla, (mla_q_norm_g[l], mla_kv_norm_g[l], mla_w_uq[l], mla_w_ukv[l]),
                                   need_ctx, mla_rope)
        w_out_l = w_out[l].astype(BF16)
        ffn_p = (ffn_w_up[l], ffn_conv_w[l], ffn_conv_b[l], ffn_w_down[l])
        x = _outproj((y_s5, y_hy, y_ret, y_mla), w_out_l, x, ml, g(1))
        x = _ffn(x, ml, g(2), g(3), *ffn_p)
        if need_ctx:
            yc_hy = _hyena_ctx(pc_hy, hy_p)
            ctx = _outproj((yc_s5, yc_hy, yc_ret, yc_mla), w_out_l, ctx, mc, g(1))
            ctx = _ffn(ctx, mc, g(2), g(3), *ffn_p)
    return x
```

```python
import functools
import math

import numpy as np
import jax
import jax.numpy as jnp
from jax import lax
from jax.experimental import pallas as pl
from jax.experimental.pallas import tpu as pltpu

F32 = jnp.float32
BF16 = jnp.bfloat16
HIGHEST = lax.Precision.HIGHEST

EPS = 1e-6
D_MODEL = 1024
W_GROUP = 256
GRID_W = 64
S5_CH, S5_GROUPS, S5_STATE = 16, 16, 64
S5_COLS = S5_GROUPS * S5_STATE
S5_SUB = 16
S5_GRP = 128
HY_ORDER, HY_BANDS, HY_HIDDEN = 2, 16, 64
RET_HEADS, RET_DK = 4, 64
RET_ROPE_BASE = 10000.0
MLA_HEADS, MLA_NOPE, MLA_ROPE, MLA_V = 4, 64, 32, 64
MLA_Q_RANK, MLA_KV_RANK = 192, 128
MLA_QK = MLA_NOPE + MLA_ROPE
MLA_HEAD_PAD = 128
MLA_VROWS = MLA_V
ATTN_TQ = 512
ROPE_BASE = 10000.0
D_FF = 2816
FF_CHUNK = 256
FF_AHEAD = 2
N_IN = 2560
FFT_N2 = 64

VMEM_LIMIT_BYTES = 56 * 1024 * 1024


def _call(kernel, *, name, grid, in_specs, out_specs, out_shape, scratch_shapes=()):
    return pl.pallas_call(
        kernel, name=name, grid=grid, in_specs=in_specs, out_specs=out_specs, out_shape=out_shape,
        scratch_shapes=scratch_shapes,
        compiler_params=pltpu.CompilerParams(dimension_semantics=("arbitrary",) * len(grid),
                                             vmem_limit_bytes=VMEM_LIMIT_BYTES))


def _dot(a, b):
    return jnp.dot(a, b, preferred_element_type=F32)


def _dot_nt(a, b):
    return lax.dot_general(a, b, (((1,), (1,)), ((), ())), preferred_element_type=F32)


def _dot_tn(a, b):
    return lax.dot_general(a, b, (((0,), (0,)), ((), ())), preferred_element_type=F32)


def _rms(x, g):
    return x * lax.rsqrt(jnp.mean(x * x, axis=-1, keepdims=True) + EPS) * g


def _silu(x):
    return x * jax.nn.sigmoid(x)


def _mod_kernel(c_ref, w_ref, b_ref, o_ref):
    split = lambda a: (a.astype(BF16), (a - a.astype(BF16).astype(F32)).astype(BF16))
    s_hi, s_lo = split(_silu(c_ref[...]))
    w_hi, w_lo = split(w_ref[0])
    o_ref[0] = _dot(s_hi, w_hi) + (_dot(s_lo, w_hi) + _dot(s_hi, w_lo)) + b_ref[0]


def _adaln(c, c_ctx, ada_w, ada_b):
    bsz, depth, n6 = c.shape[0], ada_w.shape[0], ada_w.shape[2]
    rows = 8
    assert bsz + 1 <= rows
    cc = jnp.concatenate([c, c_ctx[None], jnp.zeros((rows - bsz - 1, D_MODEL), F32)], axis=0)
    tn = 1536
    out = _call(
        _mod_kernel, name="adaln", grid=(depth, n6 // tn),
        in_specs=[pl.BlockSpec((rows, D_MODEL), lambda l, j: (0, 0)),
                  pl.BlockSpec((1, D_MODEL, tn), lambda l, j: (l, 0, j)),
                  pl.BlockSpec((1, 1, tn), lambda l, j: (l, 0, j))],
        out_specs=pl.BlockSpec((1, rows, tn), lambda l, j: (l, 0, j)),
        out_shape=jax.ShapeDtypeStruct((depth, rows, n6), F32))(cc, ada_w, ada_b.reshape(depth, 1, n6))
    mod = out.reshape(depth, rows, 6, D_MODEL)
    return mod[:, :bsz], jnp.broadcast_to(mod[:, bsz:bsz + 1], (depth, bsz, 6, D_MODEL))


def _inproj_kernel(x_ref, mod_ref, g_ref, w_ref, s5_ref, hy_ref, ret_ref, mla_ref):
    h = _rms(x_ref[0], g_ref[...]) * (1.0 + mod_ref[0, 1:2, :]) + mod_ref[0, 0:1, :]
    hb = h.astype(BF16)
    s5_ref[0] = _dot(hb, w_ref[:, 0:256]).astype(BF16)
    hy_ref[0] = _dot(hb, w_ref[:, 256:1024]).astype(BF16)
    ret_ref[0] = _dot(hb, w_ref[:, 1024:2048]).astype(BF16)
    mla_ref[0] = _dot(hb, w_ref[:, 2048:2560]).astype(BF16)


def _inproj(x, mod, g, w):
    bsz, n, _ = x.shape
    tm = min(512, n)
    row = lambda width: pl.BlockSpec((1, tm, width), lambda b, i: (b, i, 0))
    shp = lambda width: jax.ShapeDtypeStruct((bsz, n, width), BF16)
    return _call(
        _inproj_kernel, name="inproj", grid=(bsz, n // tm),
        in_specs=[row(D_MODEL), pl.BlockSpec((1, 6, D_MODEL), lambda b, i: (b, 0, 0)),
                  pl.BlockSpec((1, D_MODEL), lambda b, i: (0, 0)),
                  pl.BlockSpec((D_MODEL, N_IN), lambda b, i: (0, 0))],
        out_specs=[row(256), row(768), row(1024), row(512)],
        out_shape=[shp(256), shp(768), shp(1024), shp(512)])(x, mod, g, w)


def _perm_w_in(w):
    z = lambda k: jnp.zeros((D_MODEL, k), w.dtype)
    return jnp.concatenate([w[:, :2048], w[:, 2048:2240], z(64), w[:, 2240:2368], w[:, 2368:2400], z(96)],
                           axis=1).astype(BF16)


def _gelu_tanh(x):
    return 0.5 * x * (1.0 + jnp.tanh(math.sqrt(2.0 / math.pi) * (x + 0.044715 * (x * x * x))))


def _s5_kernel(*refs, rev, tm, final):
    if final:
        (u_ref, bblk_ref, cblk_ref, tab_ref, tri_ref, h0_ref, yprev_ref, d_ref, gw_ref, gb_ref,
         y_ref, hfin_ref, hs_scr, carry_scr) = refs
    else:
        u_ref, bblk_ref, cblk_ref, tab_ref, tri_ref, h0_ref, y_ref, hfin_ref, hs_scr, carry_scr = refs
    nc = S5_COLS

    @pl.when(pl.program_id(1) == 0)
    def _():
        carry_scr[...] = h0_ref[0]

    car_r, car_i = tab_ref[4], tab_ref[5]
    nsub = S5_GRP // S5_SUB

    def cmul(t, xr, xi):
        tr, ti = tab_ref[t][None], tab_ref[t + 1][None]
        xr3, xi3 = xr.reshape(nsub, S5_SUB, nc), xi.reshape(nsub, S5_SUB, nc)
        return ((tr * xr3 - ti * xi3).reshape(S5_GRP, nc), (tr * xi3 + ti * xr3).reshape(S5_GRP, nc))

    order = list(range(tm // S5_GRP))
    subs = list(range(S5_GRP // S5_SUB))
    if rev:
        order, subs = order[::-1], subs[::-1]
    rows = lambda g: slice(g * S5_GRP, (g + 1) * S5_GRP)

    def project(g):
        return _dot(u_ref[0, rows(g), :], bblk_ref[...])

    def cumsum(bu):
        br, bi = bu[:, 0:nc], bu[:, nc:]
        gr, gi = cmul(0, br, bi)
        return _dot(tri_ref[...], jnp.concatenate([gr, gi], axis=1).astype(BF16))

    def scan(g, cs, hr_prev, hi_prev):
        csr, csi = cs[:, 0:nc], cs[:, nc:]
        edge = 0 if rev else S5_SUB - 1
        first = slice(0, 1)
        step = first if rev else slice(S5_SUB - 1, S5_SUB)
        rmul = lambda sl, xr, xi: (car_r[sl] * xr - car_i[sl] * xi, car_r[sl] * xi + car_i[sl] * xr)
        c_r, c_i = rmul(step if rev else first, hr_prev, hi_prev)
        post_r, post_i = tab_ref[2], tab_ref[3]
        r0 = g * S5_GRP
        hr = hi = None
        for s in subs:
            a0 = s * S5_SUB
            tr = csr[a0:a0 + S5_SUB] + c_r
            ti = csi[a0:a0 + S5_SUB] + c_i
            hr = post_r * tr - post_i * ti
            hi = post_r * ti + post_i * tr
            hs_scr[r0 + a0:r0 + a0 + S5_SUB, 0:nc] = hr.astype(BF16)
            hs_scr[r0 + a0:r0 + a0 + S5_SUB, nc:] = hi.astype(BF16)
            c_r, c_i = rmul(step, tr[edge:edge + 1], ti[edge:edge + 1])
        return hr[edge:edge + 1], hi[edge:edge + 1]

    def readout(g):
        y = _dot(hs_scr[rows(g), :], cblk_ref[...])
        if final:
            yt = _gelu_tanh(yprev_ref[0, rows(g), :] + y + d_ref[...] * u_ref[0, rows(g), :].astype(F32))
            z = _dot(yt.astype(BF16), gw_ref[...]) + gb_ref[...]
            y_ref[0, rows(g), :] = (yt * jax.nn.sigmoid(z)).astype(y_ref.dtype)
        else:
            y_ref[0, rows(g), :] = y

    cr, ci = carry_scr[:, 0:nc], carry_scr[:, nc:]
    ng = len(order)
    bus = {0: project(order[0])}
    if ng > 1:
        bus[1] = project(order[1])
    css = {0: cumsum(bus.pop(0))}
    for i in range(ng):
        if i + 2 < ng:
            bus[i + 2] = project(order[i + 2])
        if i + 1 < ng:
            css[i + 1] = cumsum(bus.pop(i + 1))
        cr, ci = scan(order[i], css.pop(i), cr, ci)
        readout(order[i])
    carry = jnp.concatenate([cr, ci], axis=1)
    carry_scr[...] = carry
    hfin_ref[0] = carry


def _s5_dir(u, bblk, cblk, tab, tri, h0, rev, final=None):
    bsz, n, _ = u.shape
    tm = min(1024, n)
    nb = n // tm
    blk = (lambda i: nb - 1 - i) if rev else (lambda i: i)
    row = pl.BlockSpec((1, tm, W_GROUP), lambda b, i: (b, blk(i), 0))
    full = lambda a: pl.BlockSpec(a.shape, lambda b, i: (0,) * a.ndim)
    state = pl.BlockSpec((1, 1, 2 * S5_COLS), lambda b, i: (b, 0, 0))
    ins = [u, bblk, cblk, tab, tri, h0]
    in_specs = [row, full(bblk), full(cblk), full(tab), full(tri), state]
    if final is not None:
        ins += list(final)
        in_specs += [row] + [full(a) for a in final[1:]]
    return _call(
        functools.partial(_s5_kernel, rev=rev, tm=tm, final=final is not None),
        name="s5_bwd" if rev else "s5_fwd", grid=(bsz, nb), in_specs=in_specs,
        out_specs=[row, state],
        out_shape=[jax.ShapeDtypeStruct((bsz, n, W_GROUP), BF16 if final is not None else F32),
                   jax.ShapeDtypeStruct((bsz, 1, 2 * S5_COLS), F32)],
        scratch_shapes=[pltpu.VMEM((tm, 2 * S5_COLS), BF16), pltpu.VMEM((1, 2 * S5_COLS), F32)])(*ins)


def _s5_weights(lam_re, lam_im, log_dt, b_re, b_im, c_re, c_im):
    g, p, ch = S5_GROUPS, S5_STATE, S5_CH
    dt = jnp.exp(log_dt)[..., None]
    re_dt = (lam_re * dt).reshape(2, 1, g * p)
    im_dt = (lam_im * dt).reshape(2, 1, g * p)
    lam = lax.complex(lam_re, lam_im)
    abar = jnp.exp(lax.complex(lam_re * dt, lam_im * dt))
    bbar = ((abar - 1.0) / lam)[..., None] * lax.complex(b_re, b_im)
    eye = jnp.eye(g, dtype=F32)
    place_b = lambda a: jnp.einsum('dgpc,gh->dgchp', a, eye).reshape(2, g * ch, g * p)
    bblk = jnp.concatenate([place_b(jnp.real(bbar)), place_b(jnp.imag(bbar))], axis=2).astype(BF16)
    place_c = lambda a: jnp.einsum('dgcp,gh->dgphc', a, eye).reshape(2, g * p, g * ch)
    cblk = jnp.concatenate([place_c(c_re), place_c(-c_im)], axis=1).astype(BF16)

    def powers(k, d):
        kk = k[:, None]
        mag = jnp.exp(kk * re_dt[d])
        return [mag * jnp.cos(kk * im_dt[d]), mag * jnp.sin(kk * im_dt[d])]

    i_sub = jnp.arange(S5_SUB, dtype=F32)
    tab_f = jnp.stack(powers(-i_sub, 0) + powers(i_sub, 0) + powers(i_sub + 1.0, 0))
    tab_b = jnp.stack(powers(i_sub, 1) + powers(-i_sub, 1) + powers(S5_SUB - i_sub, 1))
    ii = np.arange(S5_GRP)
    same = (ii[:, None] // S5_SUB) == (ii[None, :] // S5_SUB)
    tri_f = jnp.asarray(same & (ii[None, :] <= ii[:, None]), F32).astype(BF16)
    tri_b = jnp.asarray(same & (ii[None, :] >= ii[:, None]), F32).astype(BF16)
    return (bblk[0], cblk[0], tab_f, tri_f), (bblk[1], cblk[1], tab_b, tri_b)


def _s5_mixer(u_lat, u_ctx, prm, need_ctx):
    (lam_re, lam_im, log_dt, b_re, b_im, c_re, c_im, d_skip, glu_w, glu_b) = prm
    wf, wb = _s5_weights(lam_re, lam_im, log_dt, b_re, b_im, c_re, c_im)
    bsz = u_lat.shape[0]
    zero = jnp.zeros((bsz, 1, 2 * S5_COLS), F32)
    fin = lambda yprev: (yprev, d_skip.reshape(1, W_GROUP), glu_w.astype(BF16), glu_b.reshape(1, W_GROUP))
    yc_f, hc_f = _s5_dir(u_ctx, *wf, zero, rev=False)
    yc, hc_b = _s5_dir(u_ctx, *wb, zero, rev=True, final=fin(yc_f))
    yl_f, _ = _s5_dir(u_lat, *wf, hc_f, rev=False)
    yl, _ = _s5_dir(u_lat, *wb, hc_b, rev=True, final=fin(yl_f))
    return yl, (yc if need_ctx else None)


def _rot_half(x, half, period):
    lane = lax.broadcasted_iota(jnp.int32, x.shape, 1)
    width = x.shape[1]
    return jnp.where((lane % period) < half, pltpu.roll(x, width - half, axis=1), pltpu.roll(x, half, axis=1))


def _ret_kernel(*refs, rev, tm, nchunks, rotate, final):
    refs = list(refs)
    q_ref, k_ref, v_ref = refs[:3]
    del refs[:3]
    if rotate:
        cos_ref, sin_ref = refs[:2]
        del refs[:2]
    lgt_ref, lgc_ref, s0_ref = refs[:3]
    del refs[:3]
    if final:
        of_ref, g_ref, gn_ref, avg_ref = refs[:4]
        del refs[:4]
    o_ref, sfin_ref, d_scr, xz_scr, s_scr = refs
    hd = RET_HEADS * RET_DK

    @pl.when((pl.program_id(0) == 0) & (pl.program_id(1) == 0))
    def _():
        ri = lax.broadcasted_iota(jnp.int32, (tm, tm), 0)
        ci = lax.broadcasted_iota(jnp.int32, (tm, tm), 1)
        diff = (ci - ri) if rev else (ri - ci)
        dpos = jnp.maximum(diff, 0).astype(F32)
        for h in range(RET_HEADS):
            d_scr[h] = jnp.where(diff >= 0, jnp.exp(dpos * lgt_ref[h]), 0.0)
        pos = lax.broadcasted_iota(jnp.int32, (tm, hd), 0).astype(F32)
        lgc = lgc_ref[...]
        xz_scr[0] = jnp.exp(((tm - pos) if rev else (pos + 1.0)) * lgc)
        xz_scr[1] = jnp.exp((pos if rev else (tm - 1.0 - pos)) * lgc)

    @pl.when(pl.program_id(1) == 0)
    def _():
        s_scr[...] = s0_ref[0]

    head = lax.broadcasted_iota(jnp.int32, (tm, hd), 1) // RET_DK
    rh = lax.broadcasted_iota(jnp.int32, (hd, hd), 0) // RET_DK
    ch = lax.broadcasted_iota(jnp.int32, (hd, hd), 1) // RET_DK
    chunks = list(range(nchunks))
    state = s_scr[...]
    for c in (chunks[::-1] if rev else chunks):
        rows = slice(c * tm, (c + 1) * tm)
        q = q_ref[0, rows, :].astype(F32)
        k = k_ref[0, rows, :].astype(F32) * (RET_DK ** -0.5)
        if rotate:
            reps = hd // cos_ref.shape[1]
            cos = jnp.concatenate([cos_ref[rows, :]] * reps, axis=1)
            sin = jnp.concatenate([sin_ref[rows, :]] * reps, axis=1)
            q = q * cos + _rot_half(q, RET_DK // 2, RET_DK) * sin
            k = k * cos + _rot_half(k, RET_DK // 2, RET_DK) * sin
        qb, kb, v = q.astype(BF16), k.astype(BF16), v_ref[0, rows, :]
        o = _dot((q * xz_scr[0]).astype(BF16), state.astype(BF16))
        for h in range(RET_HEADS):
            m = head == h
            s = _dot_nt(jnp.where(m, qb, jnp.zeros_like(qb)), kb)
            oh = _dot((s * d_scr[h]).astype(BF16), v)
            o = o + jnp.where(m, oh, 0.0)
        ds = _dot_tn(kb, (v.astype(F32) * xz_scr[1]).astype(BF16))
        state = state * jnp.exp(tm * lgc_ref[...]) + jnp.where(rh == ch, ds, 0.0)
        if final:
            o = o + of_ref[0, rows, :]
            o2 = o * o
            hi = o2.astype(BF16)
            lo = (o2 - hi.astype(F32)).astype(BF16)
            ms = _dot(hi, avg_ref[...]) + _dot(lo, avg_ref[...])
            g = g_ref[0, rows, :].astype(F32)
            o_ref[0, rows, :] = (_silu(g) * (o * lax.rsqrt(ms + EPS) * gn_ref[...])).astype(o_ref.dtype)
        else:
            o_ref[0, rows, :] = o
    s_scr[...] = state
    sfin_ref[0] = state


def _ret_dir(p, lg, s0, rev, rope=None, final=None):
    bsz, n, _ = p.shape
    tm = min(256, n)
    nchunks = 2 if n % (2 * tm) == 0 else 1
    tb = tm * nchunks
    nb = n // tb
    hd = RET_HEADS * RET_DK
    blk = (lambda i: nb - 1 - i) if rev else (lambda i: i)
    col = lambda j: pl.BlockSpec((1, tb, hd), lambda b, i: (b, blk(i), j))
    full = lambda a: pl.BlockSpec(a.shape, lambda b, i: (0,) * a.ndim)
    state = pl.BlockSpec((1, hd, hd), lambda b, i: (b, 0, 0))
    lgt = jnp.broadcast_to(lg[:, None, None], (RET_HEADS, 1, tm))
    lgc = jnp.repeat(lg, RET_DK).reshape(1, hd)
    ins, in_specs = [p, p, p], [col(0), col(1), col(2)]
    if rope is not None:
        ins += list(rope)
        in_specs += [pl.BlockSpec((tb, rope[0].shape[1]), lambda b, i: (blk(i), 0))] * 2
    ins += [lgt, lgc, s0]
    in_specs += [full(lgt), full(lgc), state]
    if final is not None:
        of, gn = final
        avg = jnp.asarray(np.kron(np.eye(RET_HEADS), np.full((RET_DK, RET_DK), 1.0 / RET_DK)), F32).astype(BF16)
        ins += [of, p, gn.reshape(1, hd), avg]
        in_specs += [pl.BlockSpec((1, tb, hd), lambda b, i: (b, blk(i), 0)), col(3), full(gn.reshape(1, hd)), full(avg)]
    return _call(
        functools.partial(_ret_kernel, rev=rev, tm=tm, nchunks=nchunks, rotate=rope is not None,
                          final=final is not None),
        name="ret_bwd" if rev else "ret_fwd", grid=(bsz, nb), in_specs=in_specs,
        out_specs=[pl.BlockSpec((1, tb, hd), lambda b, i: (b, blk(i), 0)), state],
        out_shape=[jax.ShapeDtypeStruct((bsz, n, hd), BF16 if final is not None else F32),
                   jax.ShapeDtypeStruct((bsz, hd, hd), F32)],
        scratch_shapes=[pltpu.VMEM((RET_HEADS, tm, tm), F32), pltpu.VMEM((2, tm, hd), F32),
                        pltpu.VMEM((hd, hd), F32)])(*ins)


def _ret_rope_tables(n):
    theta = RET_ROPE_BASE ** (-jnp.linspace(0.0, 1.0, RET_DK // 2, dtype=F32))
    ang = jnp.arange(n, dtype=F32)[:, None] * theta
    cos = jnp.tile(jnp.cos(ang), (1, 4))
    sin = jnp.tile(jnp.concatenate([-jnp.sin(ang), jnp.sin(ang)], axis=1), (1, 2))
    return cos, sin


def _ret_mixer(p_lat, p_ctx, prm, need_ctx, rope):
    decay_exp, gn_g = prm
    lg = jnp.log1p(-jnp.exp2(-decay_exp))
    bsz, n, _ = p_lat.shape
    hd = RET_HEADS * RET_DK
    cos, sin = rope
    zero = jnp.zeros((bsz, hd, hd), F32)
    oc_f, sc_f = _ret_dir(p_ctx, lg[0], zero, rev=False)
    yc, sc_b = _ret_dir(p_ctx, lg[1], zero, rev=True, final=(oc_f, gn_g))
    ol_f, _ = _ret_dir(p_lat, lg[0], sc_f, rev=False, rope=(cos, sin))
    yl, _ = _ret_dir(p_lat, lg[1], sc_b, rev=True, rope=(cos, sin), final=(ol_f, gn_g))
    return yl, (yc if need_ctx else None)


def _mla_prep_kernel(*refs, rotate):
    if rotate:
        (p_ref, gq_ref, gkv_ref, wq_ref, wk_ref, wvt_ref, e_ref, wqr_ref, er_ref, cos_ref, sin_ref,
         q_ref, k_ref, vt_ref) = refs
    else:
        p_ref, gq_ref, gkv_ref, wq_ref, wk_ref, wvt_ref, e_ref, q_ref, k_ref, vt_ref = refs
    p = p_ref[0]
    cq = p[:, 0:256].astype(F32)
    cqn = cq * lax.rsqrt(jnp.sum(cq * cq, axis=-1, keepdims=True) * (1.0 / MLA_Q_RANK) + EPS) * gq_ref[...]
    cqb = cqn.astype(BF16)
    q = _dot(cqb, wq_ref[...])
    ckv = p[:, 256:384].astype(F32)
    ckvb = _rms(ckv, gkv_ref[...]).astype(BF16)
    k = _dot(ckvb, wk_ref[...]) + _dot(p[:, 384:512], e_ref[...])
    if rotate:
        cos = jnp.concatenate([cos_ref[...]] * MLA_HEADS, axis=1)
        sin = jnp.concatenate([sin_ref[...]] * MLA_HEADS, axis=1)
        q = q * cos + _dot(cqb, wqr_ref[...]) * sin
        k = k * cos + _dot(p[:, 384:512], er_ref[...]) * sin
    q_ref[0] = (q * (MLA_QK ** -0.5 * math.log2(math.e))).astype(BF16)
    k_ref[0] = k.astype(BF16)
    vt_ref[0] = _dot_nt(wvt_ref[...], ckvb).astype(BF16)


def _mla_prep(p, wts, rope=None):
    bsz, n, _ = p.shape
    tm = min(512, n)
    width = MLA_HEADS * MLA_HEAD_PAD
    full = lambda a: pl.BlockSpec(a.shape, lambda b, i: (0,) * a.ndim)
    row = lambda w: pl.BlockSpec((1, tm, w), lambda b, i: (b, i, 0))
    ins = [p] + list(wts)
    in_specs = [row(512)] + [full(a) for a in wts]
    if rope is not None:
        ins += list(rope)
        in_specs += [full(a) for a in rope[:2]] + [pl.BlockSpec((tm, MLA_HEAD_PAD), lambda b, i: (i, 0))] * 2
    return _call(
        functools.partial(_mla_prep_kernel, rotate=rope is not None), name="mla_prep", grid=(bsz, n // tm),
        in_specs=in_specs,
        out_specs=[row(width), row(width), pl.BlockSpec((1, MLA_HEADS * MLA_VROWS, tm), lambda b, i: (b, 0, i))],
        out_shape=[jax.ShapeDtypeStruct((bsz, n, width), BF16), jax.ShapeDtypeStruct((bsz, n, width), BF16),
                   jax.ShapeDtypeStruct((bsz, MLA_HEADS * MLA_VROWS, n), BF16)])(*ins)


def _kv_blocks(nk, target=1152):
    assert nk % 128 == 0
    n128 = nk // 128
    nblk = max(1, -(-nk // target))
    sizes = [(n128 // nblk + (1 if j < n128 % nblk else 0)) * 128 for j in range(nblk)]
    starts = [sum(sizes[:j]) for j in range(nblk)]
    return list(zip(starts, sizes))


def _attn_kernel(q_ref, *refs, blocks):
    o_ref = refs[-1]
    k_refs, vt_refs = refs[0:-1:2], refs[1:-1:2]
    q = q_ref[0]
    tq = q.shape[0]
    units = [(j, h) for j in range(len(blocks)) for h in range(MLA_HEADS)]

    def scores(j, h):
        src, k0, kn = blocks[j]
        c0 = h * MLA_HEAD_PAD
        return _dot_nt(k_refs[src][0, k0:k0 + kn, c0:c0 + MLA_HEAD_PAD], q[:, c0:c0 + MLA_HEAD_PAD])

    m = [jnp.full((1, tq), -jnp.inf, F32)] * MLA_HEADS
    l = [jnp.zeros((1, tq), F32)] * MLA_HEADS
    acc = [jnp.zeros((MLA_V, tq), F32)] * MLA_HEADS
    ahead = 2
    pending = [scores(*u) for u in units[:ahead]]
    for idx, (j, h) in enumerate(units):
        s = pending.pop(0)
        if idx + ahead < len(units):
            pending.append(scores(*units[idx + ahead]))
        src, k0, kn = blocks[j]
        m_new = jnp.maximum(m[h], jnp.max(s, axis=0, keepdims=True))
        e = jnp.exp2(s - m_new)
        alpha = jnp.exp2(m[h] - m_new)
        l[h] = alpha * l[h] + jnp.sum(e, axis=0, keepdims=True)
        m[h] = m_new
        pv = _dot(vt_refs[src][0, h * MLA_V:(h + 1) * MLA_V, k0:k0 + kn], e.astype(BF16))
        acc[h] = alpha * acc[h] + pv
    out_t = jnp.concatenate([acc[h] * (1.0 / l[h]) for h in range(MLA_HEADS)], axis=0)
    o_ref[0] = out_t.T.astype(o_ref.dtype)


def _attention(q, sources):
    bsz, n, width = q.shape
    tq = min(ATTN_TQ, n)
    blocks, ins = [], [q]
    in_specs = [pl.BlockSpec((1, tq, width), lambda b, i: (b, i, 0))]
    for src, (k, vt) in enumerate(sources):
        nk = k.shape[1]
        blocks += [(src, k0, kn) for k0, kn in _kv_blocks(nk)]
        ins += [k, vt]
        in_specs += [pl.BlockSpec((1, nk, width), lambda b, i: (b, 0, 0)),
                     pl.BlockSpec((1, MLA_HEADS * MLA_VROWS, nk), lambda b, i: (b, 0, 0))]
    return _call(
        functools.partial(_attn_kernel, blocks=blocks), name="mla_attn", grid=(bsz, n // tq),
        in_specs=in_specs, out_specs=pl.BlockSpec((1, tq, MLA_HEADS * MLA_V), lambda b, i: (b, i, 0)),
        out_shape=jax.ShapeDtypeStruct((bsz, n, MLA_HEADS * MLA_V), BF16))(*ins)


def _mla_rope_tables(n):
    pos = jnp.arange(n)
    row, colp = (pos // GRID_W).astype(F32), (pos % GRID_W).astype(F32)
    n_freq = MLA_ROPE // 4
    inv = ROPE_BASE ** (-jnp.arange(n_freq, dtype=F32) / n_freq)
    ang = jnp.concatenate([row[:, None] * inv, colp[:, None] * inv], axis=-1)
    ones, zeros = jnp.ones((n, MLA_NOPE), F32), jnp.zeros((n, MLA_NOPE), F32)
    tail = MLA_HEAD_PAD - MLA_QK
    cos = jnp.concatenate([ones, jnp.cos(ang), jnp.cos(ang), jnp.ones((n, tail), F32)], axis=1)
    sin = jnp.concatenate([zeros, jnp.sin(ang), jnp.sin(ang), jnp.zeros((n, tail), F32)], axis=1)
    return cos, sin


def _mla_mixer(p_lat, p_ctx, prm, need_ctx, rope_tabs):
    q_norm_g, kv_norm_g, w_uq, w_ukv = prm
    n = p_lat.shape[1]
    width = MLA_HEADS * MLA_HEAD_PAD
    wq = jnp.pad(w_uq.reshape(MLA_Q_RANK, MLA_HEADS, MLA_QK),
                 ((0, 256 - MLA_Q_RANK), (0, 0), (0, MLA_HEAD_PAD - MLA_QK))).reshape(256, width).astype(BF16)
    wkv = w_ukv.reshape(MLA_KV_RANK, MLA_HEADS, MLA_NOPE + MLA_V)
    wk = jnp.pad(wkv[:, :, :MLA_NOPE], ((0, 0), (0, 0), (0, MLA_HEAD_PAD - MLA_NOPE))).reshape(MLA_KV_RANK, width)
    wv = wkv[:, :, MLA_NOPE:].reshape(MLA_KV_RANK, MLA_HEADS * MLA_V)
    place = np.zeros((128, width), np.float32)
    for h in range(MLA_HEADS):
        for j in range(MLA_ROPE):
            place[j, h * MLA_HEAD_PAD + MLA_NOPE + j] = 1.0
    place = jnp.asarray(place).astype(BF16)
    gq = jnp.pad(q_norm_g, (0, 256 - MLA_Q_RANK)).reshape(1, 256)
    wts = (gq, kv_norm_g.reshape(1, MLA_KV_RANK), wq, wk.astype(BF16), wv.T.astype(BF16), place)
    cos, sin = rope_tabs

    def rot_cols(w):
        w3 = w.reshape(w.shape[0], MLA_HEADS, MLA_HEAD_PAD)
        x1 = w3[..., MLA_NOPE:MLA_NOPE + MLA_ROPE // 2]
        x2 = w3[..., MLA_NOPE + MLA_ROPE // 2:MLA_QK]
        z = jnp.zeros_like
        return jnp.concatenate([z(w3[..., :MLA_NOPE]), -x2, x1, z(w3[..., MLA_QK:])], axis=-1).reshape(w.shape)

    rope = (rot_cols(wq), rot_cols(place), cos, sin)
    q_c, k_c, v_c = _mla_prep(p_ctx, wts)
    q_l, k_l, v_l = _mla_prep(p_lat, wts, rope)
    y_lat = _attention(q_l, [(jnp.concatenate([k_c, k_l], axis=1), jnp.concatenate([v_c, v_l], axis=2))])
    y_ctx = _attention(q_c, [(k_c, v_c)]) if need_ctx else None
    return y_lat, y_ctx


def _hy_filter_kernel(z_ref, zt_ref, w1t_ref, b1_ref, w2t_ref, b2_ref, w3_ref, fr_ref, dl_ref, k_ref, hid_scr):
    dotp = lambda a, b: jnp.dot(a, b, preferred_element_type=F32, precision=HIGHEST)

    @pl.when(pl.program_id(0) == 0)
    def _():
        fr = fr_ref[...]
        hid = jnp.sin(fr * (dotp(w1t_ref[...], zt_ref[...]) + b1_ref[...]))
        hid = jnp.sin(fr * (dotp(w2t_ref[...], hid) + b2_ref[...]))
        hid_scr[...] = hid.T

    k = dotp(hid_scr[...], w3_ref[...]) * jnp.exp(-z_ref[:, 0:1] * jnp.abs(dl_ref[...]))
    ss = jnp.sum(k * k, axis=0, keepdims=True)
    r = lax.rsqrt(ss[:, 0:W_GROUP] + ss[:, W_GROUP:] + EPS)
    k_ref[...] = (k * jnp.concatenate([r, r], axis=1)).astype(k_ref.dtype)


def _hy_filter(n, prm):
    w1, b1, w2, b2, w3, freq, deltas = prm
    pos = jnp.arange(n, dtype=F32)
    t01 = pos / (n - 1)
    bands = jnp.linspace(1e-4, HY_BANDS - 1, HY_BANDS, dtype=F32)
    ang = (2.0 * math.pi / n) * pos[:, None] * bands[None, :]
    z = jnp.concatenate([t01[:, None], jnp.cos(ang), -jnp.sin(ang)], axis=-1)
    emb = z.shape[1]
    z = jnp.pad(z, ((0, 0), (0, 128 - emb)))
    w1p = jnp.pad(w1, ((0, 128 - emb), (0, 0)))
    col = lambda a: a.reshape(-1, 1)
    ins = [z, z.T, w1p.T, col(b1), w2.T, col(b2), w3, col(freq), deltas.reshape(1, -1)]
    full = lambda a: pl.BlockSpec(a.shape, lambda o: (0,) * a.ndim)
    ncol = w3.shape[1]
    cw = 2 * W_GROUP
    in_specs = [full(a) for a in ins]
    in_specs[6] = pl.BlockSpec((HY_HIDDEN, cw), lambda o: (0, o))
    in_specs[8] = pl.BlockSpec((1, cw), lambda o: (0, o))
    return _call(_hy_filter_kernel, name="hy_filter", grid=(ncol // cw,), in_specs=in_specs,
                 out_specs=pl.BlockSpec((n, cw), lambda o: (0, o)),
                 out_shape=jax.ShapeDtypeStruct((n, ncol), BF16),
                 scratch_shapes=[pltpu.VMEM((n, HY_HIDDEN), F32)])(*ins)


def _dwconv_rows(x, w_ref, b_ref):
    n = x.shape[0]
    r = lax.broadcasted_iota(jnp.int32, x.shape, 0)
    prev = jnp.where(r == 0, 0.0, pltpu.roll(x, 1, axis=0))
    nxt = jnp.where(r == n - 1, 0.0, pltpu.roll(x, n - 1, axis=0))
    return prev * w_ref[0:1, :] + x * w_ref[1:2, :] + nxt * w_ref[2:3, :] + b_ref[...]


def _fft_consts(n_tok):
    n = 2 * n_tok
    n2n = FFT_N2
    n1n = n // n2n
    k1h = -(-(n1n // 2 + 1) // 8) * 8
    k1 = np.arange(k1h)[None, :, None]
    n1 = np.arange(n1n // 2)[None, None, :]
    n2 = np.arange(n2n)[:, None, None]
    keep = (k1 <= n1n // 2).astype(np.float64)
    th = 2.0 * np.pi * ((k1 * (n2n * n1 + n2)) % n) / n
    fa = np.concatenate([np.cos(th) * keep, -np.sin(th) * keep], axis=1)
    pair = np.where((k1 == 0) | (k1 == n1n // 2), 1.0, 2.0)
    fat = np.transpose(fa * np.concatenate([pair, pair], axis=1), (0, 2, 1)) / n
    a = np.arange(n2n)
    t2 = 2.0 * np.pi * ((a[:, None] * a[None, :]) % n2n) / n2n
    fr, fi = np.cos(t2), -np.sin(t2)
    f2 = np.block([[fr, -fi], [fi, fr]])
    f2i = np.block([[fr, fi], [-fi, fr]])
    as_bf16 = lambda m: jnp.asarray(m, F32).astype(BF16)
    return as_bf16(fa), as_bf16(fat), as_bf16(f2), as_bf16(f2i)


def _fft_a_kernel(x_ref, fa_ref, o_ref, *, nb, cw):
    for j in range(nb):
        o_ref[:, j * cw:(j + 1) * cw] = _dot(fa_ref[j], x_ref[:, j * cw:(j + 1) * cw]).astype(o_ref.dtype)


def _fft_a(x, fa, ctot):
    h1 = x.shape[0]
    r2 = fa.shape[1]
    nb = 4
    return _call(
        functools.partial(_fft_a_kernel, nb=nb, cw=ctot), name="hy_fft_a", grid=(FFT_N2 // nb,),
        in_specs=[pl.BlockSpec((h1, nb * ctot), lambda j: (0, j)), pl.BlockSpec((nb, r2, h1), lambda j: (j, 0, 0))],
        out_specs=pl.BlockSpec((r2, nb * ctot), lambda j: (0, j)),
        out_shape=jax.ShapeDtypeStruct((r2, FFT_N2 * ctot), BF16))(x, fa)


def _fft_spec_kernel(a_ref, k0_ref, f2_ref, h_ref, *, kb):
    c = W_GROUP
    for j in range(kb):
        for o in range(HY_ORDER):
            af = jnp.concatenate([a_ref[0, 0, j, :, 2 * o * c:(2 * o + 1) * c],
                                  a_ref[0, 1, j, :, 2 * o * c:(2 * o + 1) * c]], axis=0)
            ab = jnp.concatenate([a_ref[0, 0, j, :, (2 * o + 1) * c:(2 * o + 2) * c],
                                  a_ref[0, 1, j, :, (2 * o + 1) * c:(2 * o + 2) * c]], axis=0)
            xf, xb = _dot(f2_ref[...], af), _dot(f2_ref[...], ab)
            k0 = k0_ref[:, (2 * o + 1) * c:(2 * o + 2) * c].astype(F32)
            h_ref[o, j, 0:FFT_N2, :] = (xf[0:FFT_N2] + xb[0:FFT_N2] - k0).astype(h_ref.dtype)
            h_ref[o, j, FFT_N2:, :] = (xf[FFT_N2:] - xb[FFT_N2:]).astype(h_ref.dtype)


def _fft_spec(a5, k0, f2):
    n1n = a5.shape[2]
    kb = 8
    ctot = a5.shape[4]
    return _call(
        functools.partial(_fft_spec_kernel, kb=kb), name="hy_fft_spec", grid=(n1n // kb,),
        in_specs=[pl.BlockSpec((1, 2, kb, FFT_N2, ctot), lambda i: (0, 0, i, 0, 0)),
                  pl.BlockSpec((1, ctot), lambda i: (0, 0)), pl.BlockSpec(f2.shape, lambda i: (0, 0))],
        out_specs=pl.BlockSpec((HY_ORDER, kb, 2 * FFT_N2, W_GROUP), lambda i: (0, i, 0, 0)),
        out_shape=jax.ShapeDtypeStruct((HY_ORDER, n1n, 2 * FFT_N2, W_GROUP), BF16))(a5, k0, f2)


HY_SLAB = 128
PITCH_PAD = 8
HY_UNROLL = 16


def _hy_fused_kernel(x1_ref, x2_ref, v_ref, w1_ref, w2_ref, wv_ref, b1_ref, b2_ref, bv_ref, spec_ref,
                     fa_ref, fat_ref, f2_ref, f2i_ref, bias_ref, o_ref, x_scr, a_scr, z_scr, y_scr, g_scr, *, n):
    n2n, h1 = FFT_N2, n // FFT_N2
    n1n = fa_ref.shape[1] // 2
    px, pa, pz = n2n + PITCH_PAD, 2 * n1n + PITCH_PAD, 2 * n2n + PITCH_PAD
    al = lambda i: pl.multiple_of(i, 8)
    unroll_c = max(d for d in range(1, HY_UNROLL + HY_UNROLL // 2 + 1) if n1n % d == 0)
    unroll_g = min(HY_UNROLL, h1)

    g_scr[0] = _dwconv_rows(x1_ref[0].astype(F32), w1_ref, b1_ref).astype(BF16)
    g_scr[1] = _dwconv_rows(x2_ref[0].astype(F32), w2_ref, b2_ref).astype(BF16)
    u = _dwconv_rows(v_ref[0].astype(F32), wv_ref, bv_ref)
    for i in range(h1):
        x_scr[i * px:i * px + n2n, :] = u[i * n2n:(i + 1) * n2n]

    for o in range(HY_ORDER):
        def stage_a(t, carry):
            for jj in range(HY_UNROLL):
                n2 = t * HY_UNROLL + jj
                xs = x_scr[pl.ds(n2, h1, stride=px), :].astype(BF16)
                a_scr[pl.ds(al(n2 * pa), 2 * n1n), :] = _dot(fa_ref[n2], xs)
            return carry

        def stage_c(t, carry):
            for jj in range(unroll_c):
                k1 = t * unroll_c + jj
                a = jnp.concatenate([a_scr[pl.ds(k1, n2n, stride=pa), :],
                                     a_scr[pl.ds(n1n + k1, n2n, stride=pa), :]], axis=0).astype(BF16)
                x = _dot(f2_ref[...], a)
                xr, xi = x[0:n2n], x[n2n:]
                hr = spec_ref[o, k1, 0:n2n, :].astype(F32)
                hi = spec_ref[o, k1, n2n:, :].astype(F32)
                y = jnp.concatenate([xr * hr - xi * hi, xr * hi + xi * hr], axis=0).astype(BF16)
                z_scr[pl.ds(al(k1 * pz), 2 * n2n), :] = _dot(f2i_ref[...], y)
            return carry

        def stage_a_inv(t, carry):
            for jj in range(HY_UNROLL):
                n2 = t * HY_UNROLL + jj
                z = jnp.concatenate([z_scr[pl.ds(n2, n1n, stride=pz), :],
                                     z_scr[pl.ds(n2n + n2, n1n, stride=pz), :]], axis=0).astype(BF16)
                y_scr[pl.ds(al(n2 * px), h1), :] = _dot(fat_ref[n2], z)
            return carry

        def gate(t, carry):
            for jj in range(unroll_g):
                n1 = t * unroll_g + jj
                conv = y_scr[pl.ds(n1, n2n, stride=px), :]
                r0 = al(n1 * px)
                g = g_scr[o, pl.ds(pl.multiple_of(n1 * n2n, n2n), n2n), :].astype(F32)
                zn = g * (conv + x_scr[pl.ds(r0, n2n), :] * bias_ref[o:o + 1, :])
                if o + 1 < HY_ORDER:
                    x_scr[pl.ds(r0, n2n), :] = zn
                else:
                    o_ref[0, pl.ds(pl.multiple_of(n1 * n2n, n2n), n2n), :] = zn.astype(o_ref.dtype)
            return carry

        lax.fori_loop(0, n2n // HY_UNROLL, stage_a, 0)
        lax.fori_loop(0, n1n // unroll_c, stage_c, 0)
        lax.fori_loop(0, n2n // HY_UNROLL, stage_a_inv, 0)
        lax.fori_loop(0, h1 // unroll_g, gate, 0)


def _hyena_lat(p, prm):
    conv_w, conv_b, w1, b1, w2, b2, w3, freq, deltas, bias = prm
    bsz, n, ch = p.shape
    c = W_GROUP
    h1 = n // FFT_N2
    fa, fat, f2, f2i = _fft_consts(n)
    n1n = fa.shape[1] // 2
    kf = _hy_filter(n, (w1, b1, w2, b2, w3, freq, deltas))
    ak = _fft_a(kf.reshape(h1, FFT_N2 * 4 * c), fa, 4 * c)
    spec = _fft_spec(ak.reshape(1, 2, n1n, FFT_N2, 4 * c), kf[0:1], f2)
    ns = c // HY_SLAB
    px, pa, pz = FFT_N2 + PITCH_PAD, 2 * n1n + PITCH_PAD, 2 * FFT_N2 + PITCH_PAD
    once = pl.Buffered(1)
    part = lambda a, k, rows: pl.BlockSpec((rows, HY_SLAB), lambda s, b: (0, k * ns + s))
    data = lambda k: pl.BlockSpec((1, n, HY_SLAB), lambda s, b: (b, 0, k * ns + s))
    full = lambda a: pl.BlockSpec(a.shape, lambda s, b: (0,) * a.ndim, pipeline_mode=once)
    cb = conv_b.reshape(1, ch)
    return _call(
        functools.partial(_hy_fused_kernel, n=n), name="hy_fused", grid=(ns, bsz),
        in_specs=[data(0), data(1), data(2), part(conv_w, 0, 3), part(conv_w, 1, 3), part(conv_w, 2, 3),
                  part(cb, 0, 1), part(cb, 1, 1), part(cb, 2, 1),
                  pl.BlockSpec((HY_ORDER, n1n, 2 * FFT_N2, HY_SLAB), lambda s, b: (0, 0, 0, s), pipeline_mode=once),
                  full(fa), full(fat), full(f2), full(f2i),
                  pl.BlockSpec((HY_ORDER, HY_SLAB), lambda s, b: (0, s))],
        out_specs=pl.BlockSpec((1, n, HY_SLAB), lambda s, b: (b, 0, s)),
        out_shape=jax.ShapeDtypeStruct((bsz, n, c), BF16),
        scratch_shapes=[pltpu.VMEM((h1 * px, HY_SLAB), F32), pltpu.VMEM((FFT_N2 * pa, HY_SLAB), F32),
                        pltpu.VMEM((n1n * pz, HY_SLAB), F32), pltpu.VMEM((FFT_N2 * px, HY_SLAB), F32),
                        pltpu.VMEM((HY_ORDER, n, HY_SLAB), BF16)])(
        p, p, p, conv_w, conv_w, conv_w, cb, cb, cb, spec, fa, fat, f2, f2i, bias)


def _hy_ctx_kernel(p_ref, cw_ref, cb_ref, k_ref, fd_ref, fdi_ref, bias_ref, o_ref):
    c = W_GROUP
    u = _dwconv_rows(p_ref[0].astype(F32), cw_ref, cb_ref)
    nf = fd_ref.shape[0] // 2
    xk = _dot(fd_ref[...], k_ref[...])
    z = u[:, 2 * c:]
    for o, gate in enumerate((u[:, 0:c], u[:, c:2 * c])):
        kf, kb = xk[:, 2 * o * c:(2 * o + 1) * c], xk[:, (2 * o + 1) * c:(2 * o + 2) * c]
        k0 = k_ref[0:1, (2 * o + 1) * c:(2 * o + 2) * c].astype(F32)
        hr = kf[0:nf] + kb[0:nf] - k0
        hi = kf[nf:] - kb[nf:]
        x = _dot(fd_ref[...], z.astype(BF16))
        xr, xi = x[0:nf], x[nf:]
        y = jnp.concatenate([xr * hr - xi * hi, xr * hi + xi * hr], axis=0).astype(BF16)
        z = gate * (_dot(fdi_ref[...], y) + z * bias_ref[o:o + 1, :])
    o_ref[0] = z.astype(o_ref.dtype)


def _hyena_ctx(p, prm):
    conv_w, conv_b, w1, b1, w2, b2, w3, freq, deltas, bias = prm
    bsz, n, ch = p.shape
    kf = _hy_filter(n, (w1, b1, w2, b2, w3, freq, deltas))
    nn = 2 * n
    th = 2.0 * np.pi * ((np.arange(nn)[:, None] * np.arange(n)[None, :]) % nn) / nn
    fd = np.concatenate([np.cos(th), -np.sin(th)], axis=0)
    fdj = jnp.asarray(fd, F32).astype(BF16)
    fdi = jnp.asarray(fd.T / nn, F32).astype(BF16)
    ins = [p, conv_w, conv_b.reshape(1, ch), kf, fdj, fdi, bias]
    full = lambda a: pl.BlockSpec(a.shape, lambda b: (0,) * a.ndim)
    return _call(
        _hy_ctx_kernel, name="hy_ctx", grid=(bsz,),
        in_specs=[pl.BlockSpec((1, n, ch), lambda b: (b, 0, 0))] + [full(a) for a in ins[1:]],
        out_specs=pl.BlockSpec((1, n, W_GROUP), lambda b: (b, 0, 0)),
        out_shape=jax.ShapeDtypeStruct((bsz, n, W_GROUP), BF16))(*ins)


def _outproj_kernel(y0_ref, y1_ref, y2_ref, y3_ref, w_ref, x_ref, mod_ref, g_ref, o_ref):
    c = W_GROUP
    y = (_dot(y0_ref[0], w_ref[0:c, :]) + _dot(y1_ref[0], w_ref[c:2 * c, :])
         + _dot(y2_ref[0], w_ref[2 * c:3 * c, :]) + _dot(y3_ref[0], w_ref[3 * c:, :]))
    o_ref[0] = x_ref[0] + mod_ref[0, 2:3, :] * _rms(y, g_ref[...])


def _outproj(ys, w, x, mod, g):
    bsz, n, _ = x.shape
    tm = min(1024, n)
    row = lambda width: pl.BlockSpec((1, tm, width), lambda b, i: (b, i, 0))
    return _call(
        _outproj_kernel, name="outproj", grid=(bsz, n // tm),
        in_specs=[row(W_GROUP)] * 4 + [pl.BlockSpec(w.shape, lambda b, i: (0, 0)), row(D_MODEL),
                                       pl.BlockSpec((1, 6, D_MODEL), lambda b, i: (b, 0, 0)),
                                       pl.BlockSpec((1, D_MODEL), lambda b, i: (0, 0))],
        out_specs=row(D_MODEL), out_shape=jax.ShapeDtypeStruct(x.shape, F32))(*ys, w, x, mod, g)


def _ffn_kernel(xp_ref, x_ref, xn_ref, mod_ref, g2_ref, g3_ref, wup_ref, cw_ref, cb_ref, wdn_ref, perm_ref, permt_ref,
                o_ref, h_scr, u_scr, acc_scr, *, tm, nchunk):
    i, nb = pl.program_id(1), pl.num_programs(1)
    shift, scale = mod_ref[0, 3:4, :], mod_ref[0, 4:5, :]
    pre = lambda x: _rms(x, g2_ref[...]) * (1.0 + scale) + shift
    h_scr[0:tm, :] = _dot(perm_ref[...], pre(x_ref[0]).astype(BF16)).astype(BF16)
    before = pre(xp_ref[0])[7:8] * (i > 0).astype(F32)
    after = pre(xn_ref[0])[0:1] * (i < nb - 1).astype(F32)
    hrow = lax.broadcasted_iota(jnp.int32, (16, D_MODEL), 0)
    h_scr[tm:, :] = jnp.where(hrow == 0, before, jnp.where(hrow == 1, after, 0.0)).astype(BF16)
    sub = lax.broadcasted_iota(jnp.int32, (8, 2 * FF_CHUNK), 0)

    def cols(c):
        return (slice(c * FF_CHUNK, (c + 1) * FF_CHUNK), slice(D_FF + c * FF_CHUNK, D_FF + (c + 1) * FF_CHUNK))

    def up(c):
        ga, va = cols(c)
        hb = h_scr[...]
        r = jnp.concatenate([_dot(hb, wup_ref[:, ga]), _dot(hb, wup_ref[:, va])], axis=1)
        ub = u_scr.at[c % nbuf]
        ub[8:8 + tm, :] = r[0:tm]
        ub[0:8, :] = jnp.where(sub == 0, r[tm:tm + 1], pltpu.roll(r[tm - 8:tm], 1, axis=0))
        ub[8 + tm:, :] = jnp.where(sub == 7, r[tm + 1:tm + 2], pltpu.roll(r[0:8], 7, axis=0))

    nbuf = FF_AHEAD + 1
    for c in range(min(FF_AHEAD, nchunk)):
        up(c)
    for c in range(nchunk):
        ub = u_scr.at[c % nbuf]
        if c + FF_AHEAD < nchunk:
            up(c + FF_AHEAD)
        ga, va = cols(c)
        cw = jnp.concatenate([cw_ref[:, ga], cw_ref[:, va]], axis=1)
        cb = jnp.concatenate([cb_ref[:, ga], cb_ref[:, va]], axis=1)
        u = ub[0:tm, :] * cw[0:1, :] + ub[8:8 + tm, :] * cw[1:2, :] + ub[16:16 + tm, :] * cw[2:3, :] + cb
        act = (_silu(u[:, 0:FF_CHUNK]) * u[:, FF_CHUNK:]).astype(BF16)
        if c == 0:
            acc_scr[...] = _dot(act, wdn_ref[c])
        else:
            acc_scr[...] += _dot(act, wdn_ref[c])
    branch = (mod_ref[0, 5:6, :] * _rms(acc_scr[...], g3_ref[...])).astype(BF16)
    o_ref[0] = x_ref[0] + _dot(permt_ref[...], branch)


def _ffn(x, mod, g2, g3, w_up, conv_w, conv_b, w_down):
    bsz, n, _ = x.shape
    tm = min(256, n)
    nb = n // tm
    nchunk = D_FF // FF_CHUNK
    t8 = tm // 8
    pm = np.zeros((tm, tm), np.float32)
    for j in range(t8):
        for s in range(8):
            pm[8 * j + s, s * t8 + j] = 1.0
    perm, permt = jnp.asarray(pm).astype(BF16), jnp.asarray(pm.T).astype(BF16)
    wup, cw, cb = w_up.astype(BF16), conv_w, conv_b.reshape(1, 2 * D_FF)
    wdn = w_down.reshape(nchunk, FF_CHUNK, D_MODEL).astype(BF16)
    r8 = tm // 8
    last8 = n // 8 - 1
    full = lambda a: pl.BlockSpec(a.shape, lambda b, i: (0,) * a.ndim)
    vec = pl.BlockSpec((1, D_MODEL), lambda b, i: (0, 0))
    return _call(
        functools.partial(_ffn_kernel, tm=tm, nchunk=nchunk), name="convffn", grid=(bsz, nb),
        in_specs=[pl.BlockSpec((1, 8, D_MODEL), lambda b, i: (b, jnp.maximum(i * r8 - 1, 0), 0)),
                  pl.BlockSpec((1, tm, D_MODEL), lambda b, i: (b, i, 0)),
                  pl.BlockSpec((1, 8, D_MODEL), lambda b, i: (b, jnp.minimum((i + 1) * r8, last8), 0)),
                  pl.BlockSpec((1, 6, D_MODEL), lambda b, i: (b, 0, 0)), vec, vec,
                  full(wup), full(cw), full(cb), full(wdn), full(perm), full(permt)],
        out_specs=pl.BlockSpec((1, tm, D_MODEL), lambda b, i: (b, i, 0)),
        out_shape=jax.ShapeDtypeStruct(x.shape, F32),
        scratch_shapes=[pltpu.VMEM((tm + 16, D_MODEL), BF16), pltpu.VMEM((FF_AHEAD + 1, tm + 16, 2 * FF_CHUNK), F32),
                        pltpu.VMEM((tm, D_MODEL), F32)])(x, x, x, mod, g2, g3, wup, cw, cb, wdn, perm, permt)


def kernel(x, c, ctx, c_ctx, ada_w, ada_b, norm_g, w_in, w_out, s5_lam_re, s5_lam_im, s5_log_dt, s5_b_re, s5_b_im, s5_c_re, s5_c_im, s5_d, s5_glu_w, s5_glu_b, hy_conv_w, hy_conv_b, hy_w1, hy_b1, hy_w2, hy_b2, hy_w3, hy_freq, hy_deltas, hy_bias, ret_decay_exp, ret_gn_g, mla_q_norm_g, mla_kv_norm_g, mla_w_uq, mla_w_ukv, ffn_w_up, ffn_conv_w, ffn_conv_b, ffn_w_down):
    depth = ada_w.shape[0]
    mod_lat, mod_ctx = _adaln(c, c_ctx, ada_w, ada_b)
    ret_rope, mla_rope = _ret_rope_tables(x.shape[1]), _mla_rope_tables(x.shape[1])
    for l in range(depth):
        need_ctx = l < depth - 1
        ml, mc = mod_lat[l], mod_ctx[l]
        g = lambda j: norm_g[l, j].reshape(1, D_MODEL)
        w_in_l = _perm_w_in(w_in[l])
        pl_s5, pl_hy, pl_ret, pl_mla = _inproj(x, ml, g(0), w_in_l)
        pc_s5, pc_hy, pc_ret, pc_mla = _inproj(ctx, mc, g(0), w_in_l)
        s5_p = (s5_lam_re[l], s5_lam_im[l], s5_log_dt[l], s5_b_re[l], s5_b_im[l], s5_c_re[l], s5_c_im[l],
                s5_d[l], s5_glu_w[l], s5_glu_b[l])
        hy_p = (hy_conv_w[l], hy_conv_b[l], hy_w1[l], hy_b1[l], hy_w2[l], hy_b2[l], hy_w3[l], hy_freq[l],
                hy_deltas[l], hy_bias[l])
        y_s5, yc_s5 = _s5_mixer(pl_s5, pc_s5, s5_p, need_ctx)
        y_hy = _hyena_lat(pl_hy, hy_p)
        y_ret, yc_ret = _ret_mixer(pl_ret, pc_ret, (ret_decay_exp[l], ret_gn_g[l]), need_ctx, ret_rope)
        y_mla, yc_mla = _mla_mixer(pl_mla, pc_mla, (mla_q_norm_g[l], mla_kv_norm_g[l], mla_w_uq[l], mla_w_ukv[l]),
                                   need_ctx, mla_rope)
        w_out_l = w_out[l].astype(BF16)
        ffn_p = (ffn_w_up[l], ffn_conv_w[l], ffn_conv_b[l], ffn_w_down[l])
        x = _outproj((y_s5, y_hy, y_ret, y_mla), w_out_l, x, ml, g(1))
        x = _ffn(x, ml, g(2), g(3), *ffn_p)
        if need_ctx:
            yc_hy = _hyena_ctx(pc_hy, hy_p)
            ctx = _outproj((yc_s5, yc_hy, yc_ret, yc_mla), w_out_l, ctx, mc, g(1))
            ctx = _ffn(ctx, mc, g(2), g(3), *ffn_p)
    return x
```

```python
import functools
import math

import numpy as np
import jax
import jax.numpy as jnp
from jax import lax
from jax.experimental import pallas as pl
from jax.experimental.pallas import tpu as pltpu

F32 = jnp.float32
BF16 = jnp.bfloat16
HIGHEST = lax.Precision.HIGHEST

EPS = 1e-6
D_MODEL = 1024
W_GROUP = 256
GRID_W = 64
S5_CH, S5_GROUPS, S5_STATE = 16, 16, 64
S5_COLS = S5_GROUPS * S5_STATE
S5_SUB = 16
S5_GRP = 128
HY_ORDER, HY_BANDS, HY_HIDDEN = 2, 16, 64
RET_HEADS, RET_DK = 4, 64
RET_ROPE_BASE = 10000.0
MLA_HEADS, MLA_NOPE, MLA_ROPE, MLA_V = 4, 64, 32, 64
MLA_Q_RANK, MLA_KV_RANK = 192, 128
MLA_QK = MLA_NOPE + MLA_ROPE
MLA_HEAD_PAD = 128
MLA_VROWS = MLA_V
ATTN_TQ = 512
ROPE_BASE = 10000.0
D_FF = 2816
FF_CHUNK = 256
FF_AHEAD = 2
N_IN = 2560
FFT_N2 = 64

VMEM_LIMIT_BYTES = 56 * 1024 * 1024


def _call(kernel, *, name, grid, in_specs, out_specs, out_shape, scratch_shapes=()):
    return pl.pallas_call(
        kernel, name=name, grid=grid, in_specs=in_specs, out_specs=out_specs, out_shape=out_shape,
        scratch_shapes=scratch_shapes,
        compiler_params=pltpu.CompilerParams(dimension_semantics=("arbitrary",) * len(grid),
                                             vmem_limit_bytes=VMEM_LIMIT_BYTES))


def _dot(a, b):
    return jnp.dot(a, b, preferred_element_type=F32)


def _dot_nt(a, b):
    return lax.dot_general(a, b, (((1,), (1,)), ((), ())), preferred_element_type=F32)


def _dot_tn(a, b):
    return lax.dot_general(a, b, (((0,), (0,)), ((), ())), preferred_element_type=F32)


def _rms(x, g):
    return x * lax.rsqrt(jnp.mean(x * x, axis=-1, keepdims=True) + EPS) * g


def _silu(x):
    return x * jax.nn.sigmoid(x)


def _mod_kernel(c_ref, w_ref, b_ref, o_ref):
    split = lambda a: (a.astype(BF16), (a - a.astype(BF16).astype(F32)).astype(BF16))
    s_hi, s_lo = split(_silu(c_ref[...]))
    w_hi, w_lo = split(w_ref[0])
    o_ref[0] = _dot(s_hi, w_hi) + (_dot(s_lo, w_hi) + _dot(s_hi, w_lo)) + b_ref[0]


def _adaln(c, c_ctx, ada_w, ada_b):
    bsz, depth, n6 = c.shape[0], ada_w.shape[0], ada_w.shape[2]
    rows = 8
    assert bsz + 1 <= rows
    cc = jnp.concatenate([c, c_ctx[None], jnp.zeros((rows - bsz - 1, D_MODEL), F32)], axis=0)
    tn = 1536
    out = _call(
        _mod_kernel, name="adaln", grid=(depth, n6 // tn),
        in_specs=[pl.BlockSpec((rows, D_MODEL), lambda l, j: (0, 0)),
                  pl.BlockSpec((1, D_MODEL, tn), lambda l, j: (l, 0, j)),
                  pl.BlockSpec((1, 1, tn), lambda l, j: (l, 0, j))],
        out_specs=pl.BlockSpec((1, rows, tn), lambda l, j: (l, 0, j)),
        out_shape=jax.ShapeDtypeStruct((depth, rows, n6), F32))(cc, ada_w, ada_b.reshape(depth, 1, n6))
    mod = out.reshape(depth, rows, 6, D_MODEL)
    return mod[:, :bsz], jnp.broadcast_to(mod[:, bsz:bsz + 1], (depth, bsz, 6, D_MODEL))


def _inproj_kernel(x_ref, mod_ref, g_ref, w_ref, s5_ref, hy_ref, ret_ref, mla_ref):
    h = _rms(x_ref[0], g_ref[...]) * (1.0 + mod_ref[0, 1:2, :]) + mod_ref[0, 0:1, :]
    hb = h.astype(BF16)
    s5_ref[0] = _dot(hb, w_ref[:, 0:256]).astype(BF16)
    hy_ref[0] = _dot(hb, w_ref[:, 256:1024]).astype(BF16)
    ret_ref[0] = _dot(hb, w_ref[:, 1024:2048]).astype(BF16)
    mla_ref[0] = _dot(hb, w_ref[:, 2048:2560]).astype(BF16)


def _inproj(x, mod, g, w):
    bsz, n, _ = x.shape
    tm = min(1024, n)
    row = lambda width: pl.BlockSpec((1, tm, width), lambda b, i: (b, i, 0))
    shp = lambda width: jax.ShapeDtypeStruct((bsz, n, width), BF16)
    return _call(
        _inproj_kernel, name="inproj", grid=(bsz, n // tm),
        in_specs=[row(D_MODEL), pl.BlockSpec((1, 6, D_MODEL), lambda b, i: (b, 0, 0)),
                  pl.BlockSpec((1, D_MODEL), lambda b, i: (0, 0)),
                  pl.BlockSpec((D_MODEL, N_IN), lambda b, i: (0, 0))],
        out_specs=[row(256), row(768), row(1024), row(512)],
        out_shape=[shp(256), shp(768), shp(1024), shp(512)])(x, mod, g, w)


def _perm_w_in(w):
    z = lambda k: jnp.zeros((D_MODEL, k), w.dtype)
    return jnp.concatenate([w[:, :2048], w[:, 2048:2240], z(64), w[:, 2240:2368], w[:, 2368:2400], z(96)],
                           axis=1).astype(BF16)


def _gelu_tanh(x):
    return 0.5 * x * (1.0 + jnp.tanh(math.sqrt(2.0 / math.pi) * (x + 0.044715 * (x * x * x))))


def _s5_kernel(*refs, rev, tm, final):
    if final:
        (u_ref, bblk_ref, cblk_ref, tab_ref, tri_ref, h0_ref, yprev_ref, d_ref, gw_ref, gb_ref,
         y_ref, hfin_ref, hs_scr, carry_scr) = refs
    else:
        u_ref, bblk_ref, cblk_ref, tab_ref, tri_ref, h0_ref, y_ref, hfin_ref, hs_scr, carry_scr = refs
    nc = S5_COLS

    @pl.when(pl.program_id(1) == 0)
    def _():
        carry_scr[...] = h0_ref[0]

    car_r, car_i = tab_ref[4], tab_ref[5]
    nsub = S5_GRP // S5_SUB

    def cmul(t, xr, xi):
        tr, ti = tab_ref[t][None], tab_ref[t + 1][None]
        xr3, xi3 = xr.reshape(nsub, S5_SUB, nc), xi.reshape(nsub, S5_SUB, nc)
        return ((tr * xr3 - ti * xi3).reshape(S5_GRP, nc), (tr * xi3 + ti * xr3).reshape(S5_GRP, nc))

    order = list(range(tm // S5_GRP))
    subs = list(range(S5_GRP // S5_SUB))
    if rev:
        order, subs = order[::-1], subs[::-1]
    rows = lambda g: slice(g * S5_GRP, (g + 1) * S5_GRP)

    def project(g):
        return _dot(u_ref[0, rows(g), :], bblk_ref[...])

    def cumsum(bu):
        br, bi = bu[:, 0:nc], bu[:, nc:]
        gr, gi = cmul(0, br, bi)
        return _dot(tri_ref[...], jnp.concatenate([gr, gi], axis=1).astype(BF16))

    def scan(g, cs, hr_prev, hi_prev):
        csr, csi = cs[:, 0:nc], cs[:, nc:]
        edge = 0 if rev else S5_SUB - 1
        first = slice(0, 1)
        step = first if rev else slice(S5_SUB - 1, S5_SUB)
        rmul = lambda sl, xr, xi: (car_r[sl] * xr - car_i[sl] * xi, car_r[sl] * xi + car_i[sl] * xr)
        c_r, c_i = rmul(step if rev else first, hr_prev, hi_prev)
        post_r, post_i = tab_ref[2], tab_ref[3]
        r0 = g * S5_GRP
        hr = hi = None
        for s in subs:
            a0 = s * S5_SUB
            tr = csr[a0:a0 + S5_SUB] + c_r
            ti = csi[a0:a0 + S5_SUB] + c_i
            hr = post_r * tr - post_i * ti
            hi = post_r * ti + post_i * tr
            hs_scr[r0 + a0:r0 + a0 + S5_SUB, 0:nc] = hr.astype(BF16)
            hs_scr[r0 + a0:r0 + a0 + S5_SUB, nc:] = hi.astype(BF16)
            c_r, c_i = rmul(step, tr[edge:edge + 1], ti[edge:edge + 1])
        return hr[edge:edge + 1], hi[edge:edge + 1]

    def readout(g):
        y = _dot(hs_scr[rows(g), :], cblk_ref[...])
        if final:
            yt = _gelu_tanh(yprev_ref[0, rows(g), :] + y + d_ref[...] * u_ref[0, rows(g), :].astype(F32))
            z = _dot(yt.astype(BF16), gw_ref[...]) + gb_ref[...]
            y_ref[0, rows(g), :] = (yt * jax.nn.sigmoid(z)).astype(y_ref.dtype)
        else:
            y_ref[0, rows(g), :] = y

    cr, ci = carry_scr[:, 0:nc], carry_scr[:, nc:]
    ng = len(order)
    bus = {0: project(order[0])}
    if ng > 1:
        bus[1] = project(order[1])
    css = {0: cumsum(bus.pop(0))}
    for i in range(ng):
        if i + 2 < ng:
            bus[i + 2] = project(order[i + 2])
        if i + 1 < ng:
            css[i + 1] = cumsum(bus.pop(i + 1))
        cr, ci = scan(order[i], css.pop(i), cr, ci)
        readout(order[i])
    carry = jnp.concatenate([cr, ci], axis=1)
    carry_scr[...] = carry
    hfin_ref[0] = carry


def _s5_dir(u, bblk, cblk, tab, tri, h0, rev, final=None):
    bsz, n, _ = u.shape
    tm = min(1024, n)
    nb = n // tm
    blk = (lambda i: nb - 1 - i) if rev else (lambda i: i)
    row = pl.BlockSpec((1, tm, W_GROUP), lambda b, i: (b, blk(i), 0))
    full = lambda a: pl.BlockSpec(a.shape, lambda b, i: (0,) * a.ndim)
    state = pl.BlockSpec((1, 1, 2 * S5_COLS), lambda b, i: (b, 0, 0))
    ins = [u, bblk, cblk, tab, tri, h0]
    in_specs = [row, full(bblk), full(cblk), full(tab), full(tri), state]
    if final is not None:
        ins += list(final)
        in_specs += [row] + [full(a) for a in final[1:]]
    return _call(
        functools.partial(_s5_kernel, rev=rev, tm=tm, final=final is not None),
        name="s5_bwd" if rev else "s5_fwd", grid=(bsz, nb), in_specs=in_specs,
        out_specs=[row, state],
        out_shape=[jax.ShapeDtypeStruct((bsz, n, W_GROUP), BF16 if final is not None else F32),
                   jax.ShapeDtypeStruct((bsz, 1, 2 * S5_COLS), F32)],
        scratch_shapes=[pltpu.VMEM((tm, 2 * S5_COLS), BF16), pltpu.VMEM((1, 2 * S5_COLS), F32)])(*ins)


def _s5_weights(lam_re, lam_im, log_dt, b_re, b_im, c_re, c_im):
    g, p, ch = S5_GROUPS, S5_STATE, S5_CH
    dt = jnp.exp(log_dt)[..., None]
    re_dt = (lam_re * dt).reshape(2, 1, g * p)
    im_dt = (lam_im * dt).reshape(2, 1, g * p)
    lam = lax.complex(lam_re, lam_im)
    abar = jnp.exp(lax.complex(lam_re * dt, lam_im * dt))
    bbar = ((abar - 1.0) / lam)[..., None] * lax.complex(b_re, b_im)
    eye = jnp.eye(g, dtype=F32)
    place_b = lambda a: jnp.einsum('dgpc,gh->dgchp', a, eye).reshape(2, g * ch, g * p)
    bblk = jnp.concatenate([place_b(jnp.real(bbar)), place_b(jnp.imag(bbar))], axis=2).astype(BF16)
    place_c = lambda a: jnp.einsum('dgcp,gh->dgphc', a, eye).reshape(2, g * p, g * ch)
    cblk = jnp.concatenate([place_c(c_re), place_c(-c_im)], axis=1).astype(BF16)

    def powers(k, d):
        kk = k[:, None]
        mag = jnp.exp(kk * re_dt[d])
        return [mag * jnp.cos(kk * im_dt[d]), mag * jnp.sin(kk * im_dt[d])]

    i_sub = jnp.arange(S5_SUB, dtype=F32)
    tab_f = jnp.stack(powers(-i_sub, 0) + powers(i_sub, 0) + powers(i_sub + 1.0, 0))
    tab_b = jnp.stack(powers(i_sub, 1) + powers(-i_sub, 1) + powers(S5_SUB - i_sub, 1))
    ii = np.arange(S5_GRP)
    same = (ii[:, None] // S5_SUB) == (ii[None, :] // S5_SUB)
    tri_f = jnp.asarray(same & (ii[None, :] <= ii[:, None]), F32).astype(BF16)
    tri_b = jnp.asarray(same & (ii[None, :] >= ii[:, None]), F32).astype(BF16)
    return (bblk[0], cblk[0], tab_f, tri_f), (bblk[1], cblk[1], tab_b, tri_b)


def _s5_mixer(u_lat, u_ctx, prm, need_ctx):
    (lam_re, lam_im, log_dt, b_re, b_im, c_re, c_im, d_skip, glu_w, glu_b) = prm
    wf, wb = _s5_weights(lam_re, lam_im, log_dt, b_re, b_im, c_re, c_im)
    bsz = u_lat.shape[0]
    zero = jnp.zeros((bsz, 1, 2 * S5_COLS), F32)
    fin = lambda yprev: (yprev, d_skip.reshape(1, W_GROUP), glu_w.astype(BF16), glu_b.reshape(1, W_GROUP))
    yc_f, hc_f = _s5_dir(u_ctx, *wf, zero, rev=False)
    yc, hc_b = _s5_dir(u_ctx, *wb, zero, rev=True, final=fin(yc_f))
    yl_f, _ = _s5_dir(u_lat, *wf, hc_f, rev=False)
    yl, _ = _s5_dir(u_lat, *wb, hc_b, rev=True, final=fin(yl_f))
    return yl, (yc if need_ctx else None)


def _rot_half(x, half, period):
    lane = lax.broadcasted_iota(jnp.int32, x.shape, 1)
    width = x.shape[1]
    return jnp.where((lane % period) < half, pltpu.roll(x, width - half, axis=1), pltpu.roll(x, half, axis=1))


def _ret_kernel(*refs, rev, tm, nchunks, rotate, final):
    refs = list(refs)
    q_ref, k_ref, v_ref = refs[:3]
    del refs[:3]
    if rotate:
        cos_ref, sin_ref = refs[:2]
        del refs[:2]
    lgt_ref, lgc_ref, s0_ref = refs[:3]
    del refs[:3]
    if final:
        of_ref, g_ref, gn_ref, avg_ref = refs[:4]
        del refs[:4]
    o_ref, sfin_ref, d_scr, xz_scr, s_scr = refs
    hd = RET_HEADS * RET_DK

    @pl.when((pl.program_id(0) == 0) & (pl.program_id(1) == 0))
    def _():
        ri = lax.broadcasted_iota(jnp.int32, (tm, tm), 0)
        ci = lax.broadcasted_iota(jnp.int32, (tm, tm), 1)
        diff = (ci - ri) if rev else (ri - ci)
        dpos = jnp.maximum(diff, 0).astype(F32)
        for h in range(RET_HEADS):
            d_scr[h] = jnp.where(diff >= 0, jnp.exp(dpos * lgt_ref[h]), 0.0)
        pos = lax.broadcasted_iota(jnp.int32, (tm, hd), 0).astype(F32)
        lgc = lgc_ref[...]
        xz_scr[0] = jnp.exp(((tm - pos) if rev else (pos + 1.0)) * lgc)
        xz_scr[1] = jnp.exp((pos if rev else (tm - 1.0 - pos)) * lgc)

    @pl.when(pl.program_id(1) == 0)
    def _():
        s_scr[...] = s0_ref[0]

    head = lax.broadcasted_iota(jnp.int32, (tm, hd), 1) // RET_DK
    rh = lax.broadcasted_iota(jnp.int32, (hd, hd), 0) // RET_DK
    ch = lax.broadcasted_iota(jnp.int32, (hd, hd), 1) // RET_DK
    chunks = list(range(nchunks))
    state = s_scr[...]
    for c in (chunks[::-1] if rev else chunks):
        rows = slice(c * tm, (c + 1) * tm)
        q = q_ref[0, rows, :].astype(F32)
        k = k_ref[0, rows, :].astype(F32) * (RET_DK ** -0.5)
        if rotate:
            reps = hd // cos_ref.shape[1]
            cos = jnp.concatenate([cos_ref[rows, :]] * reps, axis=1)
            sin = jnp.concatenate([sin_ref[rows, :]] * reps, axis=1)
            q = q * cos + _rot_half(q, RET_DK // 2, RET_DK) * sin
            k = k * cos + _rot_half(k, RET_DK // 2, RET_DK) * sin
        qb, kb, v = q.astype(BF16), k.astype(BF16), v_ref[0, rows, :]
        o = _dot((q * xz_scr[0]).astype(BF16), state.astype(BF16))
        for h in range(RET_HEADS):
            m = head == h
            s = _dot_nt(jnp.where(m, qb, jnp.zeros_like(qb)), kb)
            oh = _dot((s * d_scr[h]).astype(BF16), v)
            o = o + jnp.where(m, oh, 0.0)
        ds = _dot_tn(kb, (v.astype(F32) * xz_scr[1]).astype(BF16))
        state = state * jnp.exp(tm * lgc_ref[...]) + jnp.where(rh == ch, ds, 0.0)
        if final:
            o = o + of_ref[0, rows, :]
            o2 = o * o
            hi = o2.astype(BF16)
            lo = (o2 - hi.astype(F32)).astype(BF16)
            ms = _dot(hi, avg_ref[...]) + _dot(lo, avg_ref[...])
            g = g_ref[0, rows, :].astype(F32)
            o_ref[0, rows, :] = (_silu(g) * (o * lax.rsqrt(ms + EPS) * gn_ref[...])).astype(o_ref.dtype)
        else:
            o_ref[0, rows, :] = o
    s_scr[...] = state
    sfin_ref[0] = state


def _ret_dir(p, lg, s0, rev, rope=None, final=None):
    bsz, n, _ = p.shape
    tm = min(256, n)
    nchunks = max(d for d in (1, 2, 4) if n % (d * tm) == 0)
    tb = tm * nchunks
    nb = n // tb
    hd = RET_HEADS * RET_DK
    blk = (lambda i: nb - 1 - i) if rev else (lambda i: i)
    col = lambda j: pl.BlockSpec((1, tb, hd), lambda b, i: (b, blk(i), j))
    full = lambda a: pl.BlockSpec(a.shape, lambda b, i: (0,) * a.ndim)
    state = pl.BlockSpec((1, hd, hd), lambda b, i: (b, 0, 0))
    lgt = jnp.broadcast_to(lg[:, None, None], (RET_HEADS, 1, tm))
    lgc = jnp.repeat(lg, RET_DK).reshape(1, hd)
    ins, in_specs = [p, p, p], [col(0), col(1), col(2)]
    if rope is not None:
        ins += list(rope)
        in_specs += [pl.BlockSpec((tb, rope[0].shape[1]), lambda b, i: (blk(i), 0))] * 2
    ins += [lgt, lgc, s0]
    in_specs += [full(lgt), full(lgc), state]
    if final is not None:
        of, gn = final
        avg = jnp.asarray(np.kron(np.eye(RET_HEADS), np.full((RET_DK, RET_DK), 1.0 / RET_DK)), F32).astype(BF16)
        ins += [of, p, gn.reshape(1, hd), avg]
        in_specs += [pl.BlockSpec((1, tb, hd), lambda b, i: (b, blk(i), 0)), col(3), full(gn.reshape(1, hd)), full(avg)]
    return _call(
        functools.partial(_ret_kernel, rev=rev, tm=tm, nchunks=nchunks, rotate=rope is not None,
                          final=final is not None),
        name="ret_bwd" if rev else "ret_fwd", grid=(bsz, nb), in_specs=in_specs,
        out_specs=[pl.BlockSpec((1, tb, hd), lambda b, i: (b, blk(i), 0)), state],
        out_shape=[jax.ShapeDtypeStruct((bsz, n, hd), BF16 if final is not None else F32),
                   jax.ShapeDtypeStruct((bsz, hd, hd), F32)],
        scratch_shapes=[pltpu.VMEM((RET_HEADS, tm, tm), F32), pltpu.VMEM((2, tm, hd), F32),
                        pltpu.VMEM((hd, hd), F32)])(*ins)


def _ret_rope_tables(n):
    theta = RET_ROPE_BASE ** (-jnp.linspace(0.0, 1.0, RET_DK // 2, dtype=F32))
    ang = jnp.arange(n, dtype=F32)[:, None] * theta
    cos = jnp.tile(jnp.cos(ang), (1, 4))
    sin = jnp.tile(jnp.concatenate([-jnp.sin(ang), jnp.sin(ang)], axis=1), (1, 2))
    return cos, sin


def _ret_mixer(p_lat, p_ctx, prm, need_ctx, rope):
    decay_exp, gn_g = prm
    lg = jnp.log1p(-jnp.exp2(-decay_exp))
    bsz, n, _ = p_lat.shape
    hd = RET_HEADS * RET_DK
    cos, sin = rope
    zero = jnp.zeros((bsz, hd, hd), F32)
    oc_f, sc_f = _ret_dir(p_ctx, lg[0], zero, rev=False)
    yc, sc_b = _ret_dir(p_ctx, lg[1], zero, rev=True, final=(oc_f, gn_g))
    ol_f, _ = _ret_dir(p_lat, lg[0], sc_f, rev=False, rope=(cos, sin))
    yl, _ = _ret_dir(p_lat, lg[1], sc_b, rev=True, rope=(cos, sin), final=(ol_f, gn_g))
    return yl, (yc if need_ctx else None)


def _mla_prep_kernel(*refs, rotate):
    if rotate:
        (p_ref, gq_ref, gkv_ref, wq_ref, wk_ref, wvt_ref, e_ref, wqr_ref, er_ref, cos_ref, sin_ref,
         q_ref, k_ref, vt_ref) = refs
    else:
        p_ref, gq_ref, gkv_ref, wq_ref, wk_ref, wvt_ref, e_ref, q_ref, k_ref, vt_ref = refs
    p = p_ref[0]
    cq = p[:, 0:256].astype(F32)
    cqn = cq * lax.rsqrt(jnp.sum(cq * cq, axis=-1, keepdims=True) * (1.0 / MLA_Q_RANK) + EPS) * gq_ref[...]
    cqb = cqn.astype(BF16)
    q = _dot(cqb, wq_ref[...])
    ckv = p[:, 256:384].astype(F32)
    ckvb = _rms(ckv, gkv_ref[...]).astype(BF16)
    k = _dot(ckvb, wk_ref[...]) + _dot(p[:, 384:512], e_ref[...])
    if rotate:
        cos = jnp.concatenate([cos_ref[...]] * MLA_HEADS, axis=1)
        sin = jnp.concatenate([sin_ref[...]] * MLA_HEADS, axis=1)
        q = q * cos + _dot(cqb, wqr_ref[...]) * sin
        k = k * cos + _dot(p[:, 384:512], er_ref[...]) * sin
    q_ref[0] = (q * (MLA_QK ** -0.5 * math.log2(math.e))).astype(BF16)
    k_ref[0] = k.astype(BF16)
    vt_ref[0] = _dot_nt(wvt_ref[...], ckvb).astype(BF16)


def _mla_prep(p, wts, rope=None):
    bsz, n, _ = p.shape
    tm = min(512, n)
    width = MLA_HEADS * MLA_HEAD_PAD
    full = lambda a: pl.BlockSpec(a.shape, lambda b, i: (0,) * a.ndim)
    row = lambda w: pl.BlockSpec((1, tm, w), lambda b, i: (b, i, 0))
    ins = [p] + list(wts)
    in_specs = [row(512)] + [full(a) for a in wts]
    if rope is not None:
        ins += list(rope)
        in_specs += [full(a) for a in rope[:2]] + [pl.BlockSpec((tm, MLA_HEAD_PAD), lambda b, i: (i, 0))] * 2
    return _call(
        functools.partial(_mla_prep_kernel, rotate=rope is not None), name="mla_prep", grid=(bsz, n // tm),
        in_specs=in_specs,
        out_specs=[row(width), row(width), pl.BlockSpec((1, MLA_HEADS * MLA_VROWS, tm), lambda b, i: (b, 0, i))],
        out_shape=[jax.ShapeDtypeStruct((bsz, n, width), BF16), jax.ShapeDtypeStruct((bsz, n, width), BF16),
                   jax.ShapeDtypeStruct((bsz, MLA_HEADS * MLA_VROWS, n), BF16)])(*ins)


def _kv_blocks(nk, target=1152):
    assert nk % 128 == 0
    n128 = nk // 128
    nblk = max(1, -(-nk // target))
    sizes = [(n128 // nblk + (1 if j < n128 % nblk else 0)) * 128 for j in range(nblk)]
    starts = [sum(sizes[:j]) for j in range(nblk)]
    return list(zip(starts, sizes))


def _attn_kernel(q_ref, *refs, blocks):
    o_ref = refs[-1]
    k_refs, vt_refs = refs[0:-1:2], refs[1:-1:2]
    q = q_ref[0]
    tq = q.shape[0]
    units = [(j, h) for j in range(len(blocks)) for h in range(MLA_HEADS)]

    def scores(j, h):
        src, k0, kn = blocks[j]
        c0 = h * MLA_HEAD_PAD
        return _dot_nt(k_refs[src][0, k0:k0 + kn, c0:c0 + MLA_HEAD_PAD], q[:, c0:c0 + MLA_HEAD_PAD])

    m = [jnp.full((1, tq), -jnp.inf, F32)] * MLA_HEADS
    l = [jnp.zeros((1, tq), F32)] * MLA_HEADS
    acc = [jnp.zeros((MLA_V, tq), F32)] * MLA_HEADS
    ahead = 2
    pending = [scores(*u) for u in units[:ahead]]
    for idx, (j, h) in enumerate(units):
        s = pending.pop(0)
        if idx + ahead < len(units):
            pending.append(scores(*units[idx + ahead]))
        src, k0, kn = blocks[j]
        m_new = jnp.maximum(m[h], jnp.max(s, axis=0, keepdims=True))
        e = jnp.exp2(s - m_new)
        alpha = jnp.exp2(m[h] - m_new)
        l[h] = alpha * l[h] + jnp.sum(e, axis=0, keepdims=True)
        m[h] = m_new
        pv = _dot(vt_refs[src][0, h * MLA_V:(h + 1) * MLA_V, k0:k0 + kn], e.astype(BF16))
        acc[h] = alpha * acc[h] + pv
    out_t = jnp.concatenate([acc[h] * (1.0 / l[h]) for h in range(MLA_HEADS)], axis=0)
    o_ref[0] = out_t.T.astype(o_ref.dtype)


def _attention(q, sources):
    bsz, n, width = q.shape
    tq = min(ATTN_TQ, n)
    blocks, ins = [], [q]
    in_specs = [pl.BlockSpec((1, tq, width), lambda b, i: (b, i, 0))]
    for src, (k, vt) in enumerate(sources):
        nk = k.shape[1]
        blocks += [(src, k0, kn) for k0, kn in _kv_blocks(nk)]
        ins += [k, vt]
        in_specs += [pl.BlockSpec((1, nk, width), lambda b, i: (b, 0, 0)),
                     pl.BlockSpec((1, MLA_HEADS * MLA_VROWS, nk), lambda b, i: (b, 0, 0))]
    return _call(
        functools.partial(_attn_kernel, blocks=blocks), name="mla_attn", grid=(bsz, n // tq),
        in_specs=in_specs, out_specs=pl.BlockSpec((1, tq, MLA_HEADS * MLA_V), lambda b, i: (b, i, 0)),
        out_shape=jax.ShapeDtypeStruct((bsz, n, MLA_HEADS * MLA_V), BF16))(*ins)


def _mla_rope_tables(n):
    pos = jnp.arange(n)
    row, colp = (pos // GRID_W).astype(F32), (pos % GRID_W).astype(F32)
    n_freq = MLA_ROPE // 4
    inv = ROPE_BASE ** (-jnp.arange(n_freq, dtype=F32) / n_freq)
    ang = jnp.concatenate([row[:, None] * inv, colp[:, None] * inv], axis=-1)
    ones, zeros = jnp.ones((n, MLA_NOPE), F32), jnp.zeros((n, MLA_NOPE), F32)
    tail = MLA_HEAD_PAD - MLA_QK
    cos = jnp.concatenate([ones, jnp.cos(ang), jnp.cos(ang), jnp.ones((n, tail), F32)], axis=1)
    sin = jnp.concatenate([zeros, jnp.sin(ang), jnp.sin(ang), jnp.zeros((n, tail), F32)], axis=1)
    return cos, sin


def _mla_mixer(p_lat, p_ctx, prm, need_ctx, rope_tabs):
    q_norm_g, kv_norm_g, w_uq, w_ukv = prm
    n = p_lat.shape[1]
    width = MLA_HEADS * MLA_HEAD_PAD
    wq = jnp.pad(w_uq.reshape(MLA_Q_RANK, MLA_HEADS, MLA_QK),
                 ((0, 256 - MLA_Q_RANK), (0, 0), (0, MLA_HEAD_PAD - MLA_QK))).reshape(256, width).astype(BF16)
    wkv = w_ukv.reshape(MLA_KV_RANK, MLA_HEADS, MLA_NOPE + MLA_V)
    wk = jnp.pad(wkv[:, :, :MLA_NOPE], ((0, 0), (0, 0), (0, MLA_HEAD_PAD - MLA_NOPE))).reshape(MLA_KV_RANK, width)
    wv = wkv[:, :, MLA_NOPE:].reshape(MLA_KV_RANK, MLA_HEADS * MLA_V)
    place = np.zeros((128, width), np.float32)
    for h in range(MLA_HEADS):
        for j in range(MLA_ROPE):
            place[j, h * MLA_HEAD_PAD + MLA_NOPE + j] = 1.0
    place = jnp.asarray(place).astype(BF16)
    gq = jnp.pad(q_norm_g, (0, 256 - MLA_Q_RANK)).reshape(1, 256)
    wts = (gq, kv_norm_g.reshape(1, MLA_KV_RANK), wq, wk.astype(BF16), wv.T.astype(BF16), place)
    cos, sin = rope_tabs

    def rot_cols(w):
        w3 = w.reshape(w.shape[0], MLA_HEADS, MLA_HEAD_PAD)
        x1 = w3[..., MLA_NOPE:MLA_NOPE + MLA_ROPE // 2]
        x2 = w3[..., MLA_NOPE + MLA_ROPE // 2:MLA_QK]
        z = jnp.zeros_like
        return jnp.concatenate([z(w3[..., :MLA_NOPE]), -x2, x1, z(w3[..., MLA_QK:])], axis=-1).reshape(w.shape)

    rope = (rot_cols(wq), rot_cols(place), cos, sin)
    q_c, k_c, v_c = _mla_prep(p_ctx, wts)
    q_l, k_l, v_l = _mla_prep(p_lat, wts, rope)
    y_lat = _attention(q_l, [(jnp.concatenate([k_c, k_l], axis=1), jnp.concatenate([v_c, v_l], axis=2))])
    y_ctx = _attention(q_c, [(k_c, v_c)]) if need_ctx else None
    return y_lat, y_ctx


def _hy_filter_kernel(z_ref, zt_ref, w1t_ref, b1_ref, w2t_ref, b2_ref, w3_ref, fr_ref, dl_ref, k_ref, hid_scr):
    dotp = lambda a, b: jnp.dot(a, b, preferred_element_type=F32, precision=HIGHEST)

    @pl.when(pl.program_id(0) == 0)
    def _():
        fr = fr_ref[...]
        hid = jnp.sin(fr * (dotp(w1t_ref[...], zt_ref[...]) + b1_ref[...]))
        hid = jnp.sin(fr * (dotp(w2t_ref[...], hid) + b2_ref[...]))
        hid_scr[...] = hid.T

    k = dotp(hid_scr[...], w3_ref[...]) * jnp.exp(-z_ref[:, 0:1] * jnp.abs(dl_ref[...]))
    ss = jnp.sum(k * k, axis=0, keepdims=True)
    r = lax.rsqrt(ss[:, 0:W_GROUP] + ss[:, W_GROUP:] + EPS)
    k_ref[...] = (k * jnp.concatenate([r, r], axis=1)).astype(k_ref.dtype)


def _hy_filter(n, prm):
    w1, b1, w2, b2, w3, freq, deltas = prm
    pos = jnp.arange(n, dtype=F32)
    t01 = pos / (n - 1)
    bands = jnp.linspace(1e-4, HY_BANDS - 1, HY_BANDS, dtype=F32)
    ang = (2.0 * math.pi / n) * pos[:, None] * bands[None, :]
    z = jnp.concatenate([t01[:, None], jnp.cos(ang), -jnp.sin(ang)], axis=-1)
    emb = z.shape[1]
    z = jnp.pad(z, ((0, 0), (0, 128 - emb)))
    w1p = jnp.pad(w1, ((0, 128 - emb), (0, 0)))
    col = lambda a: a.reshape(-1, 1)
    ins = [z, z.T, w1p.T, col(b1), w2.T, col(b2), w3, col(freq), deltas.reshape(1, -1)]
    full = lambda a: pl.BlockSpec(a.shape, lambda o: (0,) * a.ndim)
    ncol = w3.shape[1]
    cw = 2 * W_GROUP
    in_specs = [full(a) for a in ins]
    in_specs[6] = pl.BlockSpec((HY_HIDDEN, cw), lambda o: (0, o))
    in_specs[8] = pl.BlockSpec((1, cw), lambda o: (0, o))
    return _call(_hy_filter_kernel, name="hy_filter", grid=(ncol // cw,), in_specs=in_specs,
                 out_specs=pl.BlockSpec((n, cw), lambda o: (0, o)),
                 out_shape=jax.ShapeDtypeStruct((n, ncol), BF16),
                 scratch_shapes=[pltpu.VMEM((n, HY_HIDDEN), F32)])(*ins)


def _dwconv_rows(x, w_ref, b_ref):
    n = x.shape[0]
    r = lax.broadcasted_iota(jnp.int32, x.shape, 0)
    prev = jnp.where(r == 0, 0.0, pltpu.roll(x, 1, axis=0))
    nxt = jnp.where(r == n - 1, 0.0, pltpu.roll(x, n - 1, axis=0))
    return prev * w_ref[0:1, :] + x * w_ref[1:2, :] + nxt * w_ref[2:3, :] + b_ref[...]


def _fft_consts(n_tok):
    n = 2 * n_tok
    n2n = FFT_N2
    n1n = n // n2n
    k1h = -(-(n1n // 2 + 1) // 8) * 8
    k1 = np.arange(k1h)[None, :, None]
    n1 = np.arange(n1n // 2)[None, None, :]
    n2 = np.arange(n2n)[:, None, None]
    keep = (k1 <= n1n // 2).astype(np.float64)
    th = 2.0 * np.pi * ((k1 * (n2n * n1 + n2)) % n) / n
    fa = np.concatenate([np.cos(th) * keep, -np.sin(th) * keep], axis=1)
    pair = np.where((k1 == 0) | (k1 == n1n // 2), 1.0, 2.0)
    fat = np.transpose(fa * np.concatenate([pair, pair], axis=1), (0, 2, 1)) / n
    a = np.arange(n2n)
    t2 = 2.0 * np.pi * ((a[:, None] * a[None, :]) % n2n) / n2n
    fr, fi = np.cos(t2), -np.sin(t2)
    f2 = np.block([[fr, -fi], [fi, fr]])
    f2i = np.block([[fr, fi], [-fi, fr]])
    as_bf16 = lambda m: jnp.asarray(m, F32).astype(BF16)
    return as_bf16(fa), as_bf16(fat), as_bf16(f2), as_bf16(f2i)


def _fft_a_kernel(x_ref, fa_ref, o_ref, *, nb, cw):
    for j in range(nb):
        o_ref[:, j * cw:(j + 1) * cw] = _dot(fa_ref[j], x_ref[:, j * cw:(j + 1) * cw]).astype(o_ref.dtype)


def _fft_a(x, fa, ctot):
    h1 = x.shape[0]
    r2 = fa.shape[1]
    nb = 4
    return _call(
        functools.partial(_fft_a_kernel, nb=nb, cw=ctot), name="hy_fft_a", grid=(FFT_N2 // nb,),
        in_specs=[pl.BlockSpec((h1, nb * ctot), lambda j: (0, j)), pl.BlockSpec((nb, r2, h1), lambda j: (j, 0, 0))],
        out_specs=pl.BlockSpec((r2, nb * ctot), lambda j: (0, j)),
        out_shape=jax.ShapeDtypeStruct((r2, FFT_N2 * ctot), BF16))(x, fa)


def _fft_spec_kernel(a_ref, k0_ref, f2_ref, h_ref, *, kb):
    c = W_GROUP
    for j in range(kb):
        for o in range(HY_ORDER):
            af = jnp.concatenate([a_ref[0, 0, j, :, 2 * o * c:(2 * o + 1) * c],
                                  a_ref[0, 1, j, :, 2 * o * c:(2 * o + 1) * c]], axis=0)
            ab = jnp.concatenate([a_ref[0, 0, j, :, (2 * o + 1) * c:(2 * o + 2) * c],
                                  a_ref[0, 1, j, :, (2 * o + 1) * c:(2 * o + 2) * c]], axis=0)
            xf, xb = _dot(f2_ref[...], af), _dot(f2_ref[...], ab)
            k0 = k0_ref[:, (2 * o + 1) * c:(2 * o + 2) * c].astype(F32)
            h_ref[o, j, 0:FFT_N2, :] = (xf[0:FFT_N2] + xb[0:FFT_N2] - k0).astype(h_ref.dtype)
            h_ref[o, j, FFT_N2:, :] = (xf[FFT_N2:] - xb[FFT_N2:]).astype(h_ref.dtype)


def _fft_spec(a5, k0, f2):
    n1n = a5.shape[2]
    kb = 8
    ctot = a5.shape[4]
    return _call(
        functools.partial(_fft_spec_kernel, kb=kb), name="hy_fft_spec", grid=(n1n // kb,),
        in_specs=[pl.BlockSpec((1, 2, kb, FFT_N2, ctot), lambda i: (0, 0, i, 0, 0)),
                  pl.BlockSpec((1, ctot), lambda i: (0, 0)), pl.BlockSpec(f2.shape, lambda i: (0, 0))],
        out_specs=pl.BlockSpec((HY_ORDER, kb, 2 * FFT_N2, W_GROUP), lambda i: (0, i, 0, 0)),
        out_shape=jax.ShapeDtypeStruct((HY_ORDER, n1n, 2 * FFT_N2, W_GROUP), BF16))(a5, k0, f2)


HY_SLAB = 128
PITCH_PAD = 8
HY_UNROLL = 16


def _hy_fused_kernel(x1_ref, x2_ref, v_ref, w1_ref, w2_ref, wv_ref, b1_ref, b2_ref, bv_ref, spec_ref,
                     fa_ref, fat_ref, f2_ref, f2i_ref, bias_ref, o_ref, x_scr, a_scr, z_scr, y_scr, g_scr, *, n):
    n2n, h1 = FFT_N2, n // FFT_N2
    n1n = fa_ref.shape[1] // 2
    px, pa, pz = n2n + PITCH_PAD, 2 * n1n + PITCH_PAD, 2 * n2n + PITCH_PAD
    al = lambda i: pl.multiple_of(i, 8)
    unroll_c = max(d for d in range(1, HY_UNROLL + HY_UNROLL // 2 + 1) if n1n % d == 0)
    unroll_g = min(HY_UNROLL, h1)

    g_scr[0] = _dwconv_rows(x1_ref[0].astype(F32), w1_ref, b1_ref).astype(BF16)
    g_scr[1] = _dwconv_rows(x2_ref[0].astype(F32), w2_ref, b2_ref).astype(BF16)
    u = _dwconv_rows(v_ref[0].astype(F32), wv_ref, bv_ref)
    for i in range(h1):
        x_scr[i * px:i * px + n2n, :] = u[i * n2n:(i + 1) * n2n]

    for o in range(HY_ORDER):
        def stage_a(t, carry):
            for jj in range(HY_UNROLL):
                n2 = t * HY_UNROLL + jj
                xs = x_scr[pl.ds(n2, h1, stride=px), :].astype(BF16)
                a_scr[pl.ds(al(n2 * pa), 2 * n1n), :] = _dot(fa_ref[n2], xs)
            return carry

        def stage_c(t, carry):
            for jj in range(unroll_c):
                k1 = t * unroll_c + jj
                a = jnp.concatenate([a_scr[pl.ds(k1, n2n, stride=pa), :],
                                     a_scr[pl.ds(n1n + k1, n2n, stride=pa), :]], axis=0).astype(BF16)
                x = _dot(f2_ref[...], a)
                xr, xi = x[0:n2n], x[n2n:]
                hr = spec_ref[o, k1, 0:n2n, :].astype(F32)
                hi = spec_ref[o, k1, n2n:, :].astype(F32)
                y = jnp.concatenate([xr * hr - xi * hi, xr * hi + xi * hr], axis=0).astype(BF16)
                z_scr[pl.ds(al(k1 * pz), 2 * n2n), :] = _dot(f2i_ref[...], y)
            return carry

        def stage_a_inv(t, carry):
            for jj in range(HY_UNROLL):
                n2 = t * HY_UNROLL + jj
                z = jnp.concatenate([z_scr[pl.ds(n2, n1n, stride=pz), :],
                                     z_scr[pl.ds(n2n + n2, n1n, stride=pz), :]], axis=0).astype(BF16)
                y_scr[pl.ds(al(n2 * px), h1), :] = _dot(fat_ref[n2], z)
            return carry

        def gate(t, carry):
            for jj in range(unroll_g):
                n1 = t * unroll_g + jj
                conv = y_scr[pl.ds(n1, n2n, stride=px), :]
                r0 = al(n1 * px)
                g = g_scr[o, pl.ds(pl.multiple_of(n1 * n2n, n2n), n2n), :].astype(F32)
                zn = g * (conv + x_scr[pl.ds(r0, n2n), :] * bias_ref[o:o + 1, :])
                if o + 1 < HY_ORDER:
                    x_scr[pl.ds(r0, n2n), :] = zn
                else:
                    o_ref[0, pl.ds(pl.multiple_of(n1 * n2n, n2n), n2n), :] = zn.astype(o_ref.dtype)
            return carry

        lax.fori_loop(0, n2n // HY_UNROLL, stage_a, 0)
        lax.fori_loop(0, n1n // unroll_c, stage_c, 0)
        lax.fori_loop(0, n2n // HY_UNROLL, stage_a_inv, 0)
        lax.fori_loop(0, h1 // unroll_g, gate, 0)


def _hyena_lat(p, prm):
    conv_w, conv_b, w1, b1, w2, b2, w3, freq, deltas, bias = prm
    bsz, n, ch = p.shape
    c = W_GROUP
    h1 = n // FFT_N2
    fa, fat, f2, f2i = _fft_consts(n)
    n1n = fa.shape[1] // 2
    kf = _hy_filter(n, (w1, b1, w2, b2, w3, freq, deltas))
    ak = _fft_a(kf.reshape(h1, FFT_N2 * 4 * c), fa, 4 * c)
    spec = _fft_spec(ak.reshape(1, 2, n1n, FFT_N2, 4 * c), kf[0:1], f2)
    ns = c // HY_SLAB
    px, pa, pz = FFT_N2 + PITCH_PAD, 2 * n1n + PITCH_PAD, 2 * FFT_N2 + PITCH_PAD
    once = pl.Buffered(1)
    part = lambda a, k, rows: pl.BlockSpec((rows, HY_SLAB), lambda s, b: (0, k * ns + s))
    data = lambda k: pl.BlockSpec((1, n, HY_SLAB), lambda s, b: (b, 0, k * ns + s))
    full = lambda a: pl.BlockSpec(a.shape, lambda s, b: (0,) * a.ndim, pipeline_mode=once)
    cb = conv_b.reshape(1, ch)
    return _call(
        functools.partial(_hy_fused_kernel, n=n), name="hy_fused", grid=(ns, bsz),
        in_specs=[data(0), data(1), data(2), part(conv_w, 0, 3), part(conv_w, 1, 3), part(conv_w, 2, 3),
                  part(cb, 0, 1), part(cb, 1, 1), part(cb, 2, 1),
                  pl.BlockSpec((HY_ORDER, n1n, 2 * FFT_N2, HY_SLAB), lambda s, b: (0, 0, 0, s), pipeline_mode=once),
                  full(fa), full(fat), full(f2), full(f2i),
                  pl.BlockSpec((HY_ORDER, HY_SLAB), lambda s, b: (0, s))],
        out_specs=pl.BlockSpec((1, n, HY_SLAB), lambda s, b: (b, 0, s)),
        out_shape=jax.ShapeDtypeStruct((bsz, n, c), BF16),
        scratch_shapes=[pltpu.VMEM((h1 * px, HY_SLAB), F32), pltpu.VMEM((FFT_N2 * pa, HY_SLAB), F32),
                        pltpu.VMEM((n1n * pz, HY_SLAB), F32), pltpu.VMEM((FFT_N2 * px, HY_SLAB), F32),
                        pltpu.VMEM((HY_ORDER, n, HY_SLAB), BF16)])(
        p, p, p, conv_w, conv_w, conv_w, cb, cb, cb, spec, fa, fat, f2, f2i, bias)


def _hy_ctx_kernel(p_ref, cw_ref, cb_ref, k_ref, fd_ref, fdi_ref, bias_ref, o_ref):
    c = W_GROUP
    u = _dwconv_rows(p_ref[0].astype(F32), cw_ref, cb_ref)
    nf = fd_ref.shape[0] // 2
    xk = _dot(fd_ref[...], k_ref[...])
    z = u[:, 2 * c:]
    for o, gate in enumerate((u[:, 0:c], u[:, c:2 * c])):
        kf, kb = xk[:, 2 * o * c:(2 * o + 1) * c], xk[:, (2 * o + 1) * c:(2 * o + 2) * c]
        k0 = k_ref[0:1, (2 * o + 1) * c:(2 * o + 2) * c].astype(F32)
        hr = kf[0:nf] + kb[0:nf] - k0
        hi = kf[nf:] - kb[nf:]
        x = _dot(fd_ref[...], z.astype(BF16))
        xr, xi = x[0:nf], x[nf:]
        y = jnp.concatenate([xr * hr - xi * hi, xr * hi + xi * hr], axis=0).astype(BF16)
        z = gate * (_dot(fdi_ref[...], y) + z * bias_ref[o:o + 1, :])
    o_ref[0] = z.astype(o_ref.dtype)


def _hyena_ctx(p, prm):
    conv_w, conv_b, w1, b1, w2, b2, w3, freq, deltas, bias = prm
    bsz, n, ch = p.shape
    kf = _hy_filter(n, (w1, b1, w2, b2, w3, freq, deltas))
    nn = 2 * n
    th = 2.0 * np.pi * ((np.arange(nn)[:, None] * np.arange(n)[None, :]) % nn) / nn
    fd = np.concatenate([np.cos(th), -np.sin(th)], axis=0)
    fdj = jnp.asarray(fd, F32).astype(BF16)
    fdi = jnp.asarray(fd.T / nn, F32).astype(BF16)
    ins = [p, conv_w, conv_b.reshape(1, ch), kf, fdj, fdi, bias]
    full = lambda a: pl.BlockSpec(a.shape, lambda b: (0,) * a.ndim)
    return _call(
        _hy_ctx_kernel, name="hy_ctx", grid=(bsz,),
        in_specs=[pl.BlockSpec((1, n, ch), lambda b: (b, 0, 0))] + [full(a) for a in ins[1:]],
        out_specs=pl.BlockSpec((1, n, W_GROUP), lambda b: (b, 0, 0)),
        out_shape=jax.ShapeDtypeStruct((bsz, n, W_GROUP), BF16))(*ins)


def _outproj_kernel(y0_ref, y1_ref, y2_ref, y3_ref, w_ref, x_ref, mod_ref, g_ref, o_ref):
    c = W_GROUP
    y = (_dot(y0_ref[0], w_ref[0:c, :]) + _dot(y1_ref[0], w_ref[c:2 * c, :])
         + _dot(y2_ref[0], w_ref[2 * c:3 * c, :]) + _dot(y3_ref[0], w_ref[3 * c:, :]))
    o_ref[0] = x_ref[0] + mod_ref[0, 2:3, :] * _rms(y, g_ref[...])


def _outproj(ys, w, x, mod, g):
    bsz, n, _ = x.shape
    tm = min(1024, n)
    row = lambda width: pl.BlockSpec((1, tm, width), lambda b, i: (b, i, 0))
    return _call(
        _outproj_kernel, name="outproj", grid=(bsz, n // tm),
        in_specs=[row(W_GROUP)] * 4 + [pl.BlockSpec(w.shape, lambda b, i: (0, 0)), row(D_MODEL),
                                       pl.BlockSpec((1, 6, D_MODEL), lambda b, i: (b, 0, 0)),
                                       pl.BlockSpec((1, D_MODEL), lambda b, i: (0, 0))],
        out_specs=row(D_MODEL), out_shape=jax.ShapeDtypeStruct(x.shape, F32))(*ys, w, x, mod, g)


def _ffn_kernel(xp_ref, x_ref, xn_ref, mod_ref, g2_ref, g3_ref, wup_ref, cw_ref, cb_ref, wdn_ref, perm_ref, permt_ref,
                o_ref, h_scr, u_scr, acc_scr, *, tm, nchunk):
    i, nb = pl.program_id(1), pl.num_programs(1)
    shift, scale = mod_ref[0, 3:4, :], mod_ref[0, 4:5, :]
    pre = lambda x: _rms(x, g2_ref[...]) * (1.0 + scale) + shift
    h_scr[0:tm, :] = _dot(perm_ref[...], pre(x_ref[0]).astype(BF16)).astype(BF16)
    before = pre(xp_ref[0])[7:8] * (i > 0).astype(F32)
    after = pre(xn_ref[0])[0:1] * (i < nb - 1).astype(F32)
    hrow = lax.broadcasted_iota(jnp.int32, (16, D_MODEL), 0)
    h_scr[tm:, :] = jnp.where(hrow == 0, before, jnp.where(hrow == 1, after, 0.0)).astype(BF16)
    sub = lax.broadcasted_iota(jnp.int32, (8, 2 * FF_CHUNK), 0)

    def cols(c):
        return (slice(c * FF_CHUNK, (c + 1) * FF_CHUNK), slice(D_FF + c * FF_CHUNK, D_FF + (c + 1) * FF_CHUNK))

    def up(c):
        ga, va = cols(c)
        hb = h_scr[...]
        r = jnp.concatenate([_dot(hb, wup_ref[:, ga]), _dot(hb, wup_ref[:, va])], axis=1)
        ub = u_scr.at[c % nbuf]
        ub[8:8 + tm, :] = r[0:tm]
        ub[0:8, :] = jnp.where(sub == 0, r[tm:tm + 1], pltpu.roll(r[tm - 8:tm], 1, axis=0))
        ub[8 + tm:, :] = jnp.where(sub == 7, r[tm + 1:tm + 2], pltpu.roll(r[0:8], 7, axis=0))

    nbuf = FF_AHEAD + 1
    for c in range(min(FF_AHEAD, nchunk)):
        up(c)
    for c in range(nchunk):
        ub = u_scr.at[c % nbuf]
        if c + FF_AHEAD < nchunk:
            up(c + FF_AHEAD)
        ga, va = cols(c)
        cw = jnp.concatenate([cw_ref[:, ga], cw_ref[:, va]], axis=1)
        cb = jnp.concatenate([cb_ref[:, ga], cb_ref[:, va]], axis=1)
        u = ub[0:tm, :] * cw[0:1, :] + ub[8:8 + tm, :] * cw[1:2, :] + ub[16:16 + tm, :] * cw[2:3, :] + cb
        act = (_silu(u[:, 0:FF_CHUNK]) * u[:, FF_CHUNK:]).astype(BF16)
        if c == 0:
            acc_scr[...] = _dot(act, wdn_ref[c])
        else:
            acc_scr[...] += _dot(act, wdn_ref[c])
    branch = (mod_ref[0, 5:6, :] * _rms(acc_scr[...], g3_ref[...])).astype(BF16)
    o_ref[0] = x_ref[0] + _dot(permt_ref[...], branch)


def _ffn(x, mod, g2, g3, w_up, conv_w, conv_b, w_down):
    bsz, n, _ = x.shape
    tm = min(256, n)
    nb = n // tm
    nchunk = D_FF // FF_CHUNK
    t8 = tm // 8
    pm = np.zeros((tm, tm), np.float32)
    for j in range(t8):
        for s in range(8):
            pm[8 * j + s, s * t8 + j] = 1.0
    perm, permt = jnp.asarray(pm).astype(BF16), jnp.asarray(pm.T).astype(BF16)
    wup, cw, cb = w_up.astype(BF16), conv_w, conv_b.reshape(1, 2 * D_FF)
    wdn = w_down.reshape(nchunk, FF_CHUNK, D_MODEL).astype(BF16)
    r8 = tm // 8
    last8 = n // 8 - 1
    full = lambda a: pl.BlockSpec(a.shape, lambda b, i: (0,) * a.ndim)
    vec = pl.BlockSpec((1, D_MODEL), lambda b, i: (0, 0))
    return _call(
        functools.partial(_ffn_kernel, tm=tm, nchunk=nchunk), name="convffn", grid=(bsz, nb),
        in_specs=[pl.BlockSpec((1, 8, D_MODEL), lambda b, i: (b, jnp.maximum(i * r8 - 1, 0), 0)),
                  pl.BlockSpec((1, tm, D_MODEL), lambda b, i: (b, i, 0)),
                  pl.BlockSpec((1, 8, D_MODEL), lambda b, i: (b, jnp.minimum((i + 1) * r8, last8), 0)),
                  pl.BlockSpec((1, 6, D_MODEL), lambda b, i: (b, 0, 0)), vec, vec,
                  full(wup), full(cw), full(cb), full(wdn), full(perm), full(permt)],
        out_specs=pl.BlockSpec((1, tm, D_MODEL), lambda b, i: (b, i, 0)),
        out_shape=jax.ShapeDtypeStruct(x.shape, F32),
        scratch_shapes=[pltpu.VMEM((tm + 16, D_MODEL), BF16), pltpu.VMEM((FF_AHEAD + 1, tm + 16, 2 * FF_CHUNK), F32),
                        pltpu.VMEM((tm, D_MODEL), F32)])(x, x, x, mod, g2, g3, wup, cw, cb, wdn, perm, permt)


def kernel(x, c, ctx, c_ctx, ada_w, ada_b, norm_g, w_in, w_out, s5_lam_re, s5_lam_im, s5_log_dt, s5_b_re, s5_b_im, s5_c_re, s5_c_im, s5_d, s5_glu_w, s5_glu_b, hy_conv_w, hy_conv_b, hy_w1, hy_b1, hy_w2, hy_b2, hy_w3, hy_freq, hy_deltas, hy_bias, ret_decay_exp, ret_gn_g, mla_q_norm_g, mla_kv_norm_g, mla_w_uq, mla_w_ukv, ffn_w_up, ffn_conv_w, ffn_conv_b, ffn_w_down):
    depth = ada_w.shape[0]
    mod_lat, mod_ctx = _adaln(c, c_ctx, ada_w, ada_b)
    ret_rope, mla_rope = _ret_rope_tables(x.shape[1]), _mla_rope_tables(x.shape[1])
    for l in range(depth):
        need_ctx = l < depth - 1
        ml, mc = mod_lat[l], mod_ctx[l]
        g = lambda j: norm_g[l, j].reshape(1, D_MODEL)
        w_in_l = _perm_w_in(w_in[l])
        pl_s5, pl_hy, pl_ret, pl_mla = _inproj(x, ml, g(0), w_in_l)
        pc_s5, pc_hy, pc_ret, pc_mla = _inproj(ctx, mc, g(0), w_in_l)
        s5_p = (s5_lam_re[l], s5_lam_im[l], s5_log_dt[l], s5_b_re[l], s5_b_im[l], s5_c_re[l], s5_c_im[l],
                s5_d[l], s5_glu_w[l], s5_glu_b[l])
        hy_p = (hy_conv_w[l], hy_conv_b[l], hy_w1[l], hy_b1[l], hy_w2[l], hy_b2[l], hy_w3[l], hy_freq[l],
                hy_deltas[l], hy_bias[l])
        y_s5, yc_s5 = _s5_mixer(pl_s5, pc_s5, s5_p, need_ctx)
        y_hy = _hyena_lat(pl_hy, hy_p)
        y_ret, yc_ret = _ret_mixer(pl_ret, pc_ret, (ret_decay_exp[l], ret_gn_g[l]), need_ctx, ret_rope)
        y_mla, yc_mla = _mla_mixer(pl_mla, pc_mla, (mla_q_norm_g[l], mla_kv_norm_g[l], mla_w_uq[l], mla_w_ukv[l]),
                                   need_ctx, mla_rope)
        w_out_l = w_out[l].astype(BF16)
        ffn_p = (ffn_w_up[l], ffn_conv_w[l], ffn_conv_b[l], ffn_w_down[l])
        x = _outproj((y_s5, y_hy, y_ret, y_mla), w_out_l, x, ml, g(1))
        x = _ffn(x, ml, g(2), g(3), *ffn_p)
        if need_ctx:
            yc_hy = _hyena_ctx(pc_hy, hy_p)
            ctx = _outproj((yc_s5, yc_hy, yc_ret, yc_mla), w_out_l, ctx, mc, g(1))
            ctx = _ffn(ctx, mc, g(2), g(3), *ffn_p)
    return x
```

```python
import functools
import math

import numpy as np
import jax
import jax.numpy as jnp
from jax import lax
from jax.experimental import pallas as pl
from jax.experimental.pallas import tpu as pltpu

F32 = jnp.float32
BF16 = jnp.bfloat16
HIGHEST = lax.Precision.HIGHEST

EPS = 1e-6
D_MODEL = 1024
W_GROUP = 256
GRID_W = 64
S5_CH, S5_GROUPS, S5_STATE = 16, 16, 64
S5_COLS = S5_GROUPS * S5_STATE
S5_SUB = 16
S5_GRP = 128
HY_ORDER, HY_BANDS, HY_HIDDEN = 2, 16, 64
RET_HEADS, RET_DK = 4, 64
RET_ROPE_BASE = 10000.0
MLA_HEADS, MLA_NOPE, MLA_ROPE, MLA_V = 4, 64, 32, 64
MLA_Q_RANK, MLA_KV_RANK = 192, 128
MLA_QK = MLA_NOPE + MLA_ROPE
MLA_HEAD_PAD = 128
MLA_VROWS = MLA_V
ATTN_TQ = 512
ROPE_BASE = 10000.0
D_FF = 2816
FF_CHUNK = 256
FF_AHEAD = 2
N_IN = 2560
FFT_N2 = 64

VMEM_LIMIT_BYTES = 56 * 1024 * 1024


def _call(kernel, *, name, grid, in_specs, out_specs, out_shape, scratch_shapes=()):
    return pl.pallas_call(
        kernel, name=name, grid=grid, in_specs=in_specs, out_specs=out_specs, out_shape=out_shape,
        scratch_shapes=scratch_shapes,
        compiler_params=pltpu.CompilerParams(dimension_semantics=("arbitrary",) * len(grid),
                                             vmem_limit_bytes=VMEM_LIMIT_BYTES))


def _dot(a, b):
    return jnp.dot(a, b, preferred_element_type=F32)


def _dot_nt(a, b):
    return lax.dot_general(a, b, (((1,), (1,)), ((), ())), preferred_element_type=F32)


def _dot_tn(a, b):
    return lax.dot_general(a, b, (((0,), (0,)), ((), ())), preferred_element_type=F32)


def _rms(x, g):
    return x * lax.rsqrt(jnp.mean(x * x, axis=-1, keepdims=True) + EPS) * g


def _silu(x):
    return x * jax.nn.sigmoid(x)


def _mod_kernel(c_ref, w_ref, b_ref, o_ref):
    split = lambda a: (a.astype(BF16), (a - a.astype(BF16).astype(F32)).astype(BF16))
    s_hi, s_lo = split(_silu(c_ref[...]))
    w_hi, w_lo = split(w_ref[0])
    o_ref[0] = _dot(s_hi, w_hi) + (_dot(s_lo, w_hi) + _dot(s_hi, w_lo)) + b_ref[0]


def _adaln(c, c_ctx, ada_w, ada_b):
    bsz, depth, n6 = c.shape[0], ada_w.shape[0], ada_w.shape[2]
    rows = 8
    assert bsz + 1 <= rows
    cc = jnp.concatenate([c, c_ctx[None], jnp.zeros((rows - bsz - 1, D_MODEL), F32)], axis=0)
    tn = 1536
    out = _call(
        _mod_kernel, name="adaln", grid=(depth, n6 // tn),
        in_specs=[pl.BlockSpec((rows, D_MODEL), lambda l, j: (0, 0)),
                  pl.BlockSpec((1, D_MODEL, tn), lambda l, j: (l, 0, j)),
                  pl.BlockSpec((1, 1, tn), lambda l, j: (l, 0, j))],
        out_specs=pl.BlockSpec((1, rows, tn), lambda l, j: (l, 0, j)),
        out_shape=jax.ShapeDtypeStruct((depth, rows, n6), F32))(cc, ada_w, ada_b.reshape(depth, 1, n6))
    mod = out.reshape(depth, rows, 6, D_MODEL)
    return mod[:, :bsz], jnp.broadcast_to(mod[:, bsz:bsz + 1], (depth, bsz, 6, D_MODEL))


def _inproj_kernel(x_ref, mod_ref, g_ref, w_ref, s5_ref, hy_ref, ret_ref, mla_ref):
    h = _rms(x_ref[0], g_ref[...]) * (1.0 + mod_ref[0, 1:2, :]) + mod_ref[0, 0:1, :]
    hb = h.astype(BF16)
    s5_ref[0] = _dot(hb, w_ref[:, 0:256]).astype(BF16)
    hy_ref[0] = _dot(hb, w_ref[:, 256:1024]).astype(BF16)
    ret_ref[0] = _dot(hb, w_ref[:, 1024:2048]).astype(BF16)
    mla_ref[0] = _dot(hb, w_ref[:, 2048:2560]).astype(BF16)


def _inproj(x, mod, g, w):
    bsz, n, _ = x.shape
    tm = min(1024, n)
    row = lambda width: pl.BlockSpec((1, tm, width), lambda b, i: (b, i, 0))
    shp = lambda width: jax.ShapeDtypeStruct((bsz, n, width), BF16)
    return _call(
        _inproj_kernel, name="inproj", grid=(bsz, n // tm),
        in_specs=[row(D_MODEL), pl.BlockSpec((1, 6, D_MODEL), lambda b, i: (b, 0, 0)),
                  pl.BlockSpec((1, D_MODEL), lambda b, i: (0, 0)),
                  pl.BlockSpec((D_MODEL, N_IN), lambda b, i: (0, 0))],
        out_specs=[row(256), row(768), row(1024), row(512)],
        out_shape=[shp(256), shp(768), shp(1024), shp(512)])(x, mod, g, w)


def _perm_w_in(w):
    z = lambda k: jnp.zeros((D_MODEL, k), w.dtype)
    return jnp.concatenate([w[:, :2048], w[:, 2048:2240], z(64), w[:, 2240:2368], w[:, 2368:2400], z(96)],
                           axis=1).astype(BF16)


def _gelu_tanh(x):
    return 0.5 * x * (1.0 + jnp.tanh(math.sqrt(2.0 / math.pi) * (x + 0.044715 * (x * x * x))))


def _s5_kernel(*refs, rev, tm, final):
    if final:
        (u_ref, bblk_ref, cblk_ref, tab_ref, tri_ref, h0_ref, yprev_ref, d_ref, gw_ref, gb_ref,
         y_ref, hfin_ref, hs_scr, carry_scr) = refs
    else:
        u_ref, bblk_ref, cblk_ref, tab_ref, tri_ref, h0_ref, y_ref, hfin_ref, hs_scr, carry_scr = refs
    nc = S5_COLS

    @pl.when(pl.program_id(1) == 0)
    def _():
        carry_scr[...] = h0_ref[0]

    car_r, car_i = tab_ref[4], tab_ref[5]
    nsub = S5_GRP // S5_SUB

    def cmul(t, xr, xi):
        tr, ti = tab_ref[t][None], tab_ref[t + 1][None]
        xr3, xi3 = xr.reshape(nsub, S5_SUB, nc), xi.reshape(nsub, S5_SUB, nc)
        return ((tr * xr3 - ti * xi3).reshape(S5_GRP, nc), (tr * xi3 + ti * xr3).reshape(S5_GRP, nc))

    order = list(range(tm // S5_GRP))
    subs = list(range(S5_GRP // S5_SUB))
    if rev:
        order, subs = order[::-1], subs[::-1]
    rows = lambda g: slice(g * S5_GRP, (g + 1) * S5_GRP)

    def project(g):
        return _dot(u_ref[0, rows(g), :], bblk_ref[...])

    def cumsum(bu):
        br, bi = bu[:, 0:nc], bu[:, nc:]
        gr, gi = cmul(0, br, bi)
        return _dot(tri_ref[...], jnp.concatenate([gr, gi], axis=1).astype(BF16))

    def scan(g, cs, hr_prev, hi_prev):
        csr, csi = cs[:, 0:nc], cs[:, nc:]
        edge = 0 if rev else S5_SUB - 1
        first = slice(0, 1)
        step = first if rev else slice(S5_SUB - 1, S5_SUB)
        rmul = lambda sl, xr, xi: (car_r[sl] * xr - car_i[sl] * xi, car_r[sl] * xi + car_i[sl] * xr)
        c_r, c_i = rmul(step if rev else first, hr_prev, hi_prev)
        post_r, post_i = tab_ref[2], tab_ref[3]
        r0 = g * S5_GRP
        hr = hi = None
        for s in subs:
            a0 = s * S5_SUB
            tr = csr[a0:a0 + S5_SUB] + c_r
            ti = csi[a0:a0 + S5_SUB] + c_i
            hr = post_r * tr - post_i * ti
            hi = post_r * ti + post_i * tr
            hs_scr[r0 + a0:r0 + a0 + S5_SUB, 0:nc] = hr.astype(BF16)
            hs_scr[r0 + a0:r0 + a0 + S5_SUB, nc:] = hi.astype(BF16)
            c_r, c_i = rmul(step, tr[edge:edge + 1], ti[edge:edge + 1])
        return hr[edge:edge + 1], hi[edge:edge + 1]

    def readout(g):
        y = _dot(hs_scr[rows(g), :], cblk_ref[...])
        if final:
            yt = _gelu_tanh(yprev_ref[0, rows(g), :] + y + d_ref[...] * u_ref[0, rows(g), :].astype(F32))
            z = _dot(yt.astype(BF16), gw_ref[...]) + gb_ref[...]
            y_ref[0, rows(g), :] = (yt * jax.nn.sigmoid(z)).astype(y_ref.dtype)
        else:
            y_ref[0, rows(g), :] = y

    cr, ci = carry_scr[:, 0:nc], carry_scr[:, nc:]
    ng = len(order)
    bus = {0: project(order[0])}
    if ng > 1:
        bus[1] = project(order[1])
    css = {0: cumsum(bus.pop(0))}
    for i in range(ng):
        if i + 2 < ng:
            bus[i + 2] = project(order[i + 2])
        if i + 1 < ng:
            css[i + 1] = cumsum(bus.pop(i + 1))
        cr, ci = scan(order[i], css.pop(i), cr, ci)
        readout(order[i])
    carry = jnp.concatenate([cr, ci], axis=1)
    carry_scr[...] = carry
    hfin_ref[0] = carry


def _s5_dir(u, bblk, cblk, tab, tri, h0, rev, final=None):
    bsz, n, _ = u.shape
    tm = min(1024, n)
    nb = n // tm
    blk = (lambda i: nb - 1 - i) if rev else (lambda i: i)
    row = pl.BlockSpec((1, tm, W_GROUP), lambda b, i: (b, blk(i), 0))
    full = lambda a: pl.BlockSpec(a.shape, lambda b, i: (0,) * a.ndim)
    state = pl.BlockSpec((1, 1, 2 * S5_COLS), lambda b, i: (b, 0, 0))
    ins = [u, bblk, cblk, tab, tri, h0]
    in_specs = [row, full(bblk), full(cblk), full(tab), full(tri), state]
    if final is not None:
        ins += list(final)
        in_specs += [row] + [full(a) for a in final[1:]]
    return _call(
        functools.partial(_s5_kernel, rev=rev, tm=tm, final=final is not None),
        name="s5_bwd" if rev else "s5_fwd", grid=(bsz, nb), in_specs=in_specs,
        out_specs=[row, state],
        out_shape=[jax.ShapeDtypeStruct((bsz, n, W_GROUP), BF16 if final is not None else F32),
                   jax.ShapeDtypeStruct((bsz, 1, 2 * S5_COLS), F32)],
        scratch_shapes=[pltpu.VMEM((tm, 2 * S5_COLS), BF16), pltpu.VMEM((1, 2 * S5_COLS), F32)])(*ins)


def _s5_weights(lam_re, lam_im, log_dt, b_re, b_im, c_re, c_im):
    g, p, ch = S5_GROUPS, S5_STATE, S5_CH
    dt = jnp.exp(log_dt)[..., None]
    re_dt = (lam_re * dt).reshape(2, 1, g * p)
    im_dt = (lam_im * dt).reshape(2, 1, g * p)
    lam = lax.complex(lam_re, lam_im)
    abar = jnp.exp(lax.complex(lam_re * dt, lam_im * dt))
    bbar = ((abar - 1.0) / lam)[..., None] * lax.complex(b_re, b_im)
    eye = jnp.eye(g, dtype=F32)
    place_b = lambda a: jnp.einsum('dgpc,gh->dgchp', a, eye).reshape(2, g * ch, g * p)
    bblk = jnp.concatenate([place_b(jnp.real(bbar)), place_b(jnp.imag(bbar))], axis=2).astype(BF16)
    place_c = lambda a: jnp.einsum('dgcp,gh->dgphc', a, eye).reshape(2, g * p, g * ch)
    cblk = jnp.concatenate([place_c(c_re), place_c(-c_im)], axis=1).astype(BF16)

    def powers(k, d):
        kk = k[:, None]
        mag = jnp.exp(kk * re_dt[d])
        return [mag * jnp.cos(kk * im_dt[d]), mag * jnp.sin(kk * im_dt[d])]

    i_sub = jnp.arange(S5_SUB, dtype=F32)
    tab_f = jnp.stack(powers(-i_sub, 0) + powers(i_sub, 0) + powers(i_sub + 1.0, 0))
    tab_b = jnp.stack(powers(i_sub, 1) + powers(-i_sub, 1) + powers(S5_SUB - i_sub, 1))
    ii = np.arange(S5_GRP)
    same = (ii[:, None] // S5_SUB) == (ii[None, :] // S5_SUB)
    tri_f = jnp.asarray(same & (ii[None, :] <= ii[:, None]), F32).astype(BF16)
    tri_b = jnp.asarray(same & (ii[None, :] >= ii[:, None]), F32).astype(BF16)
    return (bblk[0], cblk[0], tab_f, tri_f), (bblk[1], cblk[1], tab_b, tri_b)


def _s5_mixer(u_lat, u_ctx, prm, need_ctx):
    (lam_re, lam_im, log_dt, b_re, b_im, c_re, c_im, d_skip, glu_w, glu_b) = prm
    wf, wb = _s5_weights(lam_re, lam_im, log_dt, b_re, b_im, c_re, c_im)
    bsz = u_lat.shape[0]
    zero = jnp.zeros((bsz, 1, 2 * S5_COLS), F32)
    fin = lambda yprev: (yprev, d_skip.reshape(1, W_GROUP), glu_w.astype(BF16), glu_b.reshape(1, W_GROUP))
    yc_f, hc_f = _s5_dir(u_ctx, *wf, zero, rev=False)
    yc, hc_b = _s5_dir(u_ctx, *wb, zero, rev=True, final=fin(yc_f))
    yl_f, _ = _s5_dir(u_lat, *wf, hc_f, rev=False)
    yl, _ = _s5_dir(u_lat, *wb, hc_b, rev=True, final=fin(yl_f))
    return yl, (yc if need_ctx else None)


def _rot_half(x, half, period):
    lane = lax.broadcasted_iota(jnp.int32, x.shape, 1)
    width = x.shape[1]
    return jnp.where((lane % period) < half, pltpu.roll(x, width - half, axis=1), pltpu.roll(x, half, axis=1))


def _ret_kernel(*refs, rev, tm, nchunks, rotate, final):
    refs = list(refs)
    q_ref, k_ref, v_ref = refs[:3]
    del refs[:3]
    if rotate:
        cos_ref, sin_ref = refs[:2]
        del refs[:2]
    lgt_ref, lgc_ref, s0_ref = refs[:3]
    del refs[:3]
    if final:
        of_ref, g_ref, gn_ref, avg_ref = refs[:4]
        del refs[:4]
    o_ref, sfin_ref, d_scr, xz_scr, s_scr = refs
    hd = RET_HEADS * RET_DK

    @pl.when((pl.program_id(0) == 0) & (pl.program_id(1) == 0))
    def _():
        ri = lax.broadcasted_iota(jnp.int32, (tm, tm), 0)
        ci = lax.broadcasted_iota(jnp.int32, (tm, tm), 1)
        diff = (ci - ri) if rev else (ri - ci)
        dpos = jnp.maximum(diff, 0).astype(F32)
        for h in range(RET_HEADS):
            d_scr[h] = jnp.where(diff >= 0, jnp.exp(dpos * lgt_ref[h]), 0.0)
        pos = lax.broadcasted_iota(jnp.int32, (tm, hd), 0).astype(F32)
        lgc = lgc_ref[...]
        xz_scr[0] = jnp.exp(((tm - pos) if rev else (pos + 1.0)) * lgc)
        xz_scr[1] = jnp.exp((pos if rev else (tm - 1.0 - pos)) * lgc)

    @pl.when(pl.program_id(1) == 0)
    def _():
        s_scr[...] = s0_ref[0]

    head = lax.broadcasted_iota(jnp.int32, (tm, hd), 1) // RET_DK
    rh = lax.broadcasted_iota(jnp.int32, (hd, hd), 0) // RET_DK
    ch = lax.broadcasted_iota(jnp.int32, (hd, hd), 1) // RET_DK
    chunks = list(range(nchunks))
    state = s_scr[...]
    for c in (chunks[::-1] if rev else chunks):
        rows = slice(c * tm, (c + 1) * tm)
        q = q_ref[0, rows, :].astype(F32)
        k = k_ref[0, rows, :].astype(F32) * (RET_DK ** -0.5)
        if rotate:
            reps = hd // cos_ref.shape[1]
            cos = jnp.concatenate([cos_ref[rows, :]] * reps, axis=1)
            sin = jnp.concatenate([sin_ref[rows, :]] * reps, axis=1)
            q = q * cos + _rot_half(q, RET_DK // 2, RET_DK) * sin
            k = k * cos + _rot_half(k, RET_DK // 2, RET_DK) * sin
        qb, kb, v = q.astype(BF16), k.astype(BF16), v_ref[0, rows, :]
        o = _dot((q * xz_scr[0]).astype(BF16), state.astype(BF16))
        for h in range(RET_HEADS):
            m = head == h
            s = _dot_nt(jnp.where(m, qb, jnp.zeros_like(qb)), kb)
            oh = _dot((s * d_scr[h]).astype(BF16), v)
            o = o + jnp.where(m, oh, 0.0)
        ds = _dot_tn(kb, (v.astype(F32) * xz_scr[1]).astype(BF16))
        state = state * jnp.exp(tm * lgc_ref[...]) + jnp.where(rh == ch, ds, 0.0)
        if final:
            o = o + of_ref[0, rows, :]
            o2 = o * o
            hi = o2.astype(BF16)
            lo = (o2 - hi.astype(F32)).astype(BF16)
            ms = _dot(hi, avg_ref[...]) + _dot(lo, avg_ref[...])
            g = g_ref[0, rows, :].astype(F32)
            o_ref[0, rows, :] = (_silu(g) * (o * lax.rsqrt(ms + EPS) * gn_ref[...])).astype(o_ref.dtype)
        else:
            o_ref[0, rows, :] = o
    s_scr[...] = state
    sfin_ref[0] = state


def _ret_dir(p, lg, s0, rev, rope=None, final=None):
    bsz, n, _ = p.shape
    tm = min(256, n)
    nchunks = max(d for d in (1, 2, 4) if n % (d * tm) == 0)
    tb = tm * nchunks
    nb = n // tb
    hd = RET_HEADS * RET_DK
    blk = (lambda i: nb - 1 - i) if rev else (lambda i: i)
    col = lambda j: pl.BlockSpec((1, tb, hd), lambda b, i: (b, blk(i), j))
    full = lambda a: pl.BlockSpec(a.shape, lambda b, i: (0,) * a.ndim)
    state = pl.BlockSpec((1, hd, hd), lambda b, i: (b, 0, 0))
    lgt = jnp.broadcast_to(lg[:, None, None], (RET_HEADS, 1, tm))
    lgc = jnp.repeat(lg, RET_DK).reshape(1, hd)
    ins, in_specs = [p, p, p], [col(0), col(1), col(2)]
    if rope is not None:
        ins += list(rope)
        in_specs += [pl.BlockSpec((tb, rope[0].shape[1]), lambda b, i: (blk(i), 0))] * 2
    ins += [lgt, lgc, s0]
    in_specs += [full(lgt), full(lgc), state]
    if final is not None:
        of, gn = final
        avg = jnp.asarray(np.kron(np.eye(RET_HEADS), np.full((RET_DK, RET_DK), 1.0 / RET_DK)), F32).astype(BF16)
        ins += [of, p, gn.reshape(1, hd), avg]
        in_specs += [pl.BlockSpec((1, tb, hd), lambda b, i: (b, blk(i), 0)), col(3), full(gn.reshape(1, hd)), full(avg)]
    return _call(
        functools.partial(_ret_kernel, rev=rev, tm=tm, nchunks=nchunks, rotate=rope is not None,
                          final=final is not None),
        name="ret_bwd" if rev else "ret_fwd", grid=(bsz, nb), in_specs=in_specs,
        out_specs=[pl.BlockSpec((1, tb, hd), lambda b, i: (b, blk(i), 0)), state],
        out_shape=[jax.ShapeDtypeStruct((bsz, n, hd), BF16 if final is not None else F32),
                   jax.ShapeDtypeStruct((bsz, hd, hd), F32)],
        scratch_shapes=[pltpu.VMEM((RET_HEADS, tm, tm), F32), pltpu.VMEM((2, tm, hd), F32),
                        pltpu.VMEM((hd, hd), F32)])(*ins)


def _ret_rope_tables(n):
    theta = RET_ROPE_BASE ** (-jnp.linspace(0.0, 1.0, RET_DK // 2, dtype=F32))
    ang = jnp.arange(n, dtype=F32)[:, None] * theta
    cos = jnp.tile(jnp.cos(ang), (1, 4))
    sin = jnp.tile(jnp.concatenate([-jnp.sin(ang), jnp.sin(ang)], axis=1), (1, 2))
    return cos, sin


def _ret_mixer(p_lat, p_ctx, prm, need_ctx, rope):
    decay_exp, gn_g = prm
    lg = jnp.log1p(-jnp.exp2(-decay_exp))
    bsz, n, _ = p_lat.shape
    hd = RET_HEADS * RET_DK
    cos, sin = rope
    zero = jnp.zeros((bsz, hd, hd), F32)
    oc_f, sc_f = _ret_dir(p_ctx, lg[0], zero, rev=False)
    yc, sc_b = _ret_dir(p_ctx, lg[1], zero, rev=True, final=(oc_f, gn_g))
    ol_f, _ = _ret_dir(p_lat, lg[0], sc_f, rev=False, rope=(cos, sin))
    yl, _ = _ret_dir(p_lat, lg[1], sc_b, rev=True, rope=(cos, sin), final=(ol_f, gn_g))
    return yl, (yc if need_ctx else None)


def _mla_prep_kernel(*refs, rotate):
    if rotate:
        (p_ref, gq_ref, gkv_ref, wq_ref, wk_ref, wvt_ref, e_ref, wqr_ref, er_ref, cos_ref, sin_ref,
         q_ref, k_ref, vt_ref) = refs
    else:
        p_ref, gq_ref, gkv_ref, wq_ref, wk_ref, wvt_ref, e_ref, q_ref, k_ref, vt_ref = refs
    p = p_ref[0]
    cq = p[:, 0:256].astype(F32)
    cqn = cq * lax.rsqrt(jnp.sum(cq * cq, axis=-1, keepdims=True) * (1.0 / MLA_Q_RANK) + EPS) * gq_ref[...]
    cqb = cqn.astype(BF16)
    q = _dot(cqb, wq_ref[...])
    ckv = p[:, 256:384].astype(F32)
    ckvb = _rms(ckv, gkv_ref[...]).astype(BF16)
    k = _dot(ckvb, wk_ref[...]) + _dot(p[:, 384:512], e_ref[...])
    if rotate:
        cos = jnp.concatenate([cos_ref[...]] * MLA_HEADS, axis=1)
        sin = jnp.concatenate([sin_ref[...]] * MLA_HEADS, axis=1)
        q = q * cos + _dot(cqb, wqr_ref[...]) * sin
        k = k * cos + _dot(p[:, 384:512], er_ref[...]) * sin
    q_ref[0] = (q * (MLA_QK ** -0.5 * math.log2(math.e))).astype(BF16)
    k_ref[0] = k.astype(BF16)
    vt_ref[0] = _dot_nt(wvt_ref[...], ckvb).astype(BF16)


def _mla_prep(p, wts, rope=None):
    bsz, n, _ = p.shape
    tm = min(1024, n)
    width = MLA_HEADS * MLA_HEAD_PAD
    full = lambda a: pl.BlockSpec(a.shape, lambda b, i: (0,) * a.ndim)
    row = lambda w: pl.BlockSpec((1, tm, w), lambda b, i: (b, i, 0))
    ins = [p] + list(wts)
    in_specs = [row(512)] + [full(a) for a in wts]
    if rope is not None:
        ins += list(rope)
        in_specs += [full(a) for a in rope[:2]] + [pl.BlockSpec((tm, MLA_HEAD_PAD), lambda b, i: (i, 0))] * 2
    return _call(
        functools.partial(_mla_prep_kernel, rotate=rope is not None), name="mla_prep", grid=(bsz, n // tm),
        in_specs=in_specs,
        out_specs=[row(width), row(width), pl.BlockSpec((1, MLA_HEADS * MLA_VROWS, tm), lambda b, i: (b, 0, i))],
        out_shape=[jax.ShapeDtypeStruct((bsz, n, width), BF16), jax.ShapeDtypeStruct((bsz, n, width), BF16),
                   jax.ShapeDtypeStruct((bsz, MLA_HEADS * MLA_VROWS, n), BF16)])(*ins)


def _kv_blocks(nk, target=1152):
    assert nk % 128 == 0
    n128 = nk // 128
    nblk = max(1, -(-nk // target))
    sizes = [(n128 // nblk + (1 if j < n128 % nblk else 0)) * 128 for j in range(nblk)]
    starts = [sum(sizes[:j]) for j in range(nblk)]
    return list(zip(starts, sizes))


def _attn_kernel(q_ref, *refs, blocks):
    o_ref = refs[-1]
    k_refs, vt_refs = refs[0:-1:2], refs[1:-1:2]
    q = q_ref[0]
    tq = q.shape[0]
    units = [(j, h) for j in range(len(blocks)) for h in range(MLA_HEADS)]

    def scores(j, h):
        src, k0, kn = blocks[j]
        c0 = h * MLA_HEAD_PAD
        return _dot_nt(k_refs[src][0, k0:k0 + kn, c0:c0 + MLA_HEAD_PAD], q[:, c0:c0 + MLA_HEAD_PAD])

    m = [jnp.full((1, tq), -jnp.inf, F32)] * MLA_HEADS
    l = [jnp.zeros((1, tq), F32)] * MLA_HEADS
    acc = [jnp.zeros((MLA_V, tq), F32)] * MLA_HEADS
    ahead = 2
    pending = [scores(*u) for u in units[:ahead]]
    for idx, (j, h) in enumerate(units):
        s = pending.pop(0)
        if idx + ahead < len(units):
            pending.append(scores(*units[idx + ahead]))
        src, k0, kn = blocks[j]
        m_new = jnp.maximum(m[h], jnp.max(s, axis=0, keepdims=True))
        e = jnp.exp2(s - m_new)
        alpha = jnp.exp2(m[h] - m_new)
        l[h] = alpha * l[h] + jnp.sum(e, axis=0, keepdims=True)
        m[h] = m_new
        pv = _dot(vt_refs[src][0, h * MLA_V:(h + 1) * MLA_V, k0:k0 + kn], e.astype(BF16))
        acc[h] = alpha * acc[h] + pv
    out_t = jnp.concatenate([acc[h] * (1.0 / l[h]) for h in range(MLA_HEADS)], axis=0)
    o_ref[0] = out_t.T.astype(o_ref.dtype)


def _attention(q, sources):
    bsz, n, width = q.shape
    tq = min(ATTN_TQ, n)
    blocks, ins = [], [q]
    in_specs = [pl.BlockSpec((1, tq, width), lambda b, i: (b, i, 0))]
    for src, (k, vt) in enumerate(sources):
        nk = k.shape[1]
        blocks += [(src, k0, kn) for k0, kn in _kv_blocks(nk)]
        ins += [k, vt]
        in_specs += [pl.BlockSpec((1, nk, width), lambda b, i: (b, 0, 0)),
                     pl.BlockSpec((1, MLA_HEADS * MLA_VROWS, nk), lambda b, i: (b, 0, 0))]
    return _call(
        functools.partial(_attn_kernel, blocks=blocks), name="mla_attn", grid=(bsz, n // tq),
        in_specs=in_specs, out_specs=pl.BlockSpec((1, tq, MLA_HEADS * MLA_V), lambda b, i: (b, i, 0)),
        out_shape=jax.ShapeDtypeStruct((bsz, n, MLA_HEADS * MLA_V), BF16))(*ins)


def _mla_rope_tables(n):
    pos = jnp.arange(n)
    row, colp = (pos // GRID_W).astype(F32), (pos % GRID_W).astype(F32)
    n_freq = MLA_ROPE // 4
    inv = ROPE_BASE ** (-jnp.arange(n_freq, dtype=F32) / n_freq)
    ang = jnp.concatenate([row[:, None] * inv, colp[:, None] * inv], axis=-1)
    ones, zeros = jnp.ones((n, MLA_NOPE), F32), jnp.zeros((n, MLA_NOPE), F32)
    tail = MLA_HEAD_PAD - MLA_QK
    cos = jnp.concatenate([ones, jnp.cos(ang), jnp.cos(ang), jnp.ones((n, tail), F32)], axis=1)
    sin = jnp.concatenate([zeros, jnp.sin(ang), jnp.sin(ang), jnp.zeros((n, tail), F32)], axis=1)
    return cos, sin


def _mla_mixer(p_lat, p_ctx, prm, need_ctx, rope_tabs):
    q_norm_g, kv_norm_g, w_uq, w_ukv = prm
    n = p_lat.shape[1]
    width = MLA_HEADS * MLA_HEAD_PAD
    wq = jnp.pad(w_uq.reshape(MLA_Q_RANK, MLA_HEADS, MLA_QK),
                 ((0, 256 - MLA_Q_RANK), (0, 0), (0, MLA_HEAD_PAD - MLA_QK))).reshape(256, width).astype(BF16)
    wkv = w_ukv.reshape(MLA_KV_RANK, MLA_HEADS, MLA_NOPE + MLA_V)
    wk = jnp.pad(wkv[:, :, :MLA_NOPE], ((0, 0), (0, 0), (0, MLA_HEAD_PAD - MLA_NOPE))).reshape(MLA_KV_RANK, width)
    wv = wkv[:, :, MLA_NOPE:].reshape(MLA_KV_RANK, MLA_HEADS * MLA_V)
    place = np.zeros((128, width), np.float32)
    for h in range(MLA_HEADS):
        for j in range(MLA_ROPE):
            place[j, h * MLA_HEAD_PAD + MLA_NOPE + j] = 1.0
    place = jnp.asarray(place).astype(BF16)
    gq = jnp.pad(q_norm_g, (0, 256 - MLA_Q_RANK)).reshape(1, 256)
    wts = (gq, kv_norm_g.reshape(1, MLA_KV_RANK), wq, wk.astype(BF16), wv.T.astype(BF16), place)
    cos, sin = rope_tabs

    def rot_cols(w):
        w3 = w.reshape(w.shape[0], MLA_HEADS, MLA_HEAD_PAD)
        x1 = w3[..., MLA_NOPE:MLA_NOPE + MLA_ROPE // 2]
        x2 = w3[..., MLA_NOPE + MLA_ROPE // 2:MLA_QK]
        z = jnp.zeros_like
        return jnp.concatenate([z(w3[..., :MLA_NOPE]), -x2, x1, z(w3[..., MLA_QK:])], axis=-1).reshape(w.shape)

    rope = (rot_cols(wq), rot_cols(place), cos, sin)
    q_c, k_c, v_c = _mla_prep(p_ctx, wts)
    q_l, k_l, v_l = _mla_prep(p_lat, wts, rope)
    y_lat = _attention(q_l, [(jnp.concatenate([k_c, k_l], axis=1), jnp.concatenate([v_c, v_l], axis=2))])
    y_ctx = _attention(q_c, [(k_c, v_c)]) if need_ctx else None
    return y_lat, y_ctx


def _hy_filter_kernel(z_ref, zt_ref, w1t_ref, b1_ref, w2t_ref, b2_ref, w3_ref, fr_ref, dl_ref, k_ref, hid_scr):
    dotp = lambda a, b: jnp.dot(a, b, preferred_element_type=F32, precision=HIGHEST)

    @pl.when(pl.program_id(0) == 0)
    def _():
        fr = fr_ref[...]
        hid = jnp.sin(fr * (dotp(w1t_ref[...], zt_ref[...]) + b1_ref[...]))
        hid = jnp.sin(fr * (dotp(w2t_ref[...], hid) + b2_ref[...]))
        hid_scr[...] = hid.T

    k = dotp(hid_scr[...], w3_ref[...]) * jnp.exp(-z_ref[:, 0:1] * jnp.abs(dl_ref[...]))
    ss = jnp.sum(k * k, axis=0, keepdims=True)
    r = lax.rsqrt(ss[:, 0:W_GROUP] + ss[:, W_GROUP:] + EPS)
    k_ref[...] = (k * jnp.concatenate([r, r], axis=1)).astype(k_ref.dtype)


def _hy_filter(n, prm):
    w1, b1, w2, b2, w3, freq, deltas = prm
    pos = jnp.arange(n, dtype=F32)
    t01 = pos / (n - 1)
    bands = jnp.linspace(1e-4, HY_BANDS - 1, HY_BANDS, dtype=F32)
    ang = (2.0 * math.pi / n) * pos[:, None] * bands[None, :]
    z = jnp.concatenate([t01[:, None], jnp.cos(ang), -jnp.sin(ang)], axis=-1)
    emb = z.shape[1]
    z = jnp.pad(z, ((0, 0), (0, 128 - emb)))
    w1p = jnp.pad(w1, ((0, 128 - emb), (0, 0)))
    col = lambda a: a.reshape(-1, 1)
    ins = [z, z.T, w1p.T, col(b1), w2.T, col(b2), w3, col(freq), deltas.reshape(1, -1)]
    full = lambda a: pl.BlockSpec(a.shape, lambda o: (0,) * a.ndim)
    ncol = w3.shape[1]
    cw = 2 * W_GROUP
    in_specs = [full(a) for a in ins]
    in_specs[6] = pl.BlockSpec((HY_HIDDEN, cw), lambda o: (0, o))
    in_specs[8] = pl.BlockSpec((1, cw), lambda o: (0, o))
    return _call(_hy_filter_kernel, name="hy_filter", grid=(ncol // cw,), in_specs=in_specs,
                 out_specs=pl.BlockSpec((n, cw), lambda o: (0, o)),
                 out_shape=jax.ShapeDtypeStruct((n, ncol), BF16),
                 scratch_shapes=[pltpu.VMEM((n, HY_HIDDEN), F32)])(*ins)


def _dwconv_rows(x, w_ref, b_ref):
    n = x.shape[0]
    r = lax.broadcasted_iota(jnp.int32, x.shape, 0)
    prev = jnp.where(r == 0, 0.0, pltpu.roll(x, 1, axis=0))
    nxt = jnp.where(r == n - 1, 0.0, pltpu.roll(x, n - 1, axis=0))
    return prev * w_ref[0:1, :] + x * w_ref[1:2, :] + nxt * w_ref[2:3, :] + b_ref[...]


def _fft_consts(n_tok):
    n = 2 * n_tok
    n2n = FFT_N2
    n1n = n // n2n
    k1h = -(-(n1n // 2 + 1) // 8) * 8
    k1 = np.arange(k1h)[None, :, None]
    n1 = np.arange(n1n // 2)[None, None, :]
    n2 = np.arange(n2n)[:, None, None]
    keep = (k1 <= n1n // 2).astype(np.float64)
    th = 2.0 * np.pi * ((k1 * (n2n * n1 + n2)) % n) / n
    fa = np.concatenate([np.cos(th) * keep, -np.sin(th) * keep], axis=1)
    pair = np.where((k1 == 0) | (k1 == n1n // 2), 1.0, 2.0)
    fat = np.transpose(fa * np.concatenate([pair, pair], axis=1), (0, 2, 1)) / n
    a = np.arange(n2n)
    t2 = 2.0 * np.pi * ((a[:, None] * a[None, :]) % n2n) / n2n
    fr, fi = np.cos(t2), -np.sin(t2)
    f2 = np.block([[fr, -fi], [fi, fr]])
    f2i = np.block([[fr, fi], [-fi, fr]])
    as_bf16 = lambda m: jnp.asarray(m, F32).astype(BF16)
    return as_bf16(fa), as_bf16(fat), as_bf16(f2), as_bf16(f2i)


def _fft_a_kernel(x_ref, fa_ref, o_ref, *, nb, cw):
    for j in range(nb):
        o_ref[:, j * cw:(j + 1) * cw] = _dot(fa_ref[j], x_ref[:, j * cw:(j + 1) * cw]).astype(o_ref.dtype)


def _fft_a(x, fa, ctot):
    h1 = x.shape[0]
    r2 = fa.shape[1]
    nb = 4
    return _call(
        functools.partial(_fft_a_kernel, nb=nb, cw=ctot), name="hy_fft_a", grid=(FFT_N2 // nb,),
        in_specs=[pl.BlockSpec((h1, nb * ctot), lambda j: (0, j)), pl.BlockSpec((nb, r2, h1), lambda j: (j, 0, 0))],
        out_specs=pl.BlockSpec((r2, nb * ctot), lambda j: (0, j)),
        out_shape=jax.ShapeDtypeStruct((r2, FFT_N2 * ctot), BF16))(x, fa)


def _fft_spec_kernel(a_ref, k0_ref, f2_ref, h_ref, *, kb):
    c = W_GROUP
    for j in range(kb):
        for o in range(HY_ORDER):
            af = jnp.concatenate([a_ref[0, 0, j, :, 2 * o * c:(2 * o + 1) * c],
                                  a_ref[0, 1, j, :, 2 * o * c:(2 * o + 1) * c]], axis=0)
            ab = jnp.concatenate([a_ref[0, 0, j, :, (2 * o + 1) * c:(2 * o + 2) * c],
                                  a_ref[0, 1, j, :, (2 * o + 1) * c:(2 * o + 2) * c]], axis=0)
            xf, xb = _dot(f2_ref[...], af), _dot(f2_ref[...], ab)
            k0 = k0_ref[:, (2 * o + 1) * c:(2 * o + 2) * c].astype(F32)
            h_ref[o, j, 0:FFT_N2, :] = (xf[0:FFT_N2] + xb[0:FFT_N2] - k0).astype(h_ref.dtype)
            h_ref[o, j, FFT_N2:, :] = (xf[FFT_N2:] - xb[FFT_N2:]).astype(h_ref.dtype)


def _fft_spec(a5, k0, f2):
    n1n = a5.shape[2]
    kb = 8
    ctot = a5.shape[4]
    return _call(
        functools.partial(_fft_spec_kernel, kb=kb), name="hy_fft_spec", grid=(n1n // kb,),
        in_specs=[pl.BlockSpec((1, 2, kb, FFT_N2, ctot), lambda i: (0, 0, i, 0, 0)),
                  pl.BlockSpec((1, ctot), lambda i: (0, 0)), pl.BlockSpec(f2.shape, lambda i: (0, 0))],
        out_specs=pl.BlockSpec((HY_ORDER, kb, 2 * FFT_N2, W_GROUP), lambda i: (0, i, 0, 0)),
        out_shape=jax.ShapeDtypeStruct((HY_ORDER, n1n, 2 * FFT_N2, W_GROUP), BF16))(a5, k0, f2)


HY_SLAB = 128
PITCH_PAD = 8
HY_UNROLL = 16


def _hy_fused_kernel(x1_ref, x2_ref, v_ref, w1_ref, w2_ref, wv_ref, b1_ref, b2_ref, bv_ref, spec_ref,
                     fa_ref, fat_ref, f2_ref, f2i_ref, bias_ref, o_ref, x_scr, a_scr, z_scr, y_scr, g_scr, *, n):
    n2n, h1 = FFT_N2, n // FFT_N2
    n1n = fa_ref.shape[1] // 2
    px, pa, pz = n2n + PITCH_PAD, 2 * n1n + PITCH_PAD, 2 * n2n + PITCH_PAD
    al = lambda i: pl.multiple_of(i, 8)
    unroll_c = max(d for d in range(1, HY_UNROLL + HY_UNROLL // 2 + 1) if n1n % d == 0)
    unroll_g = min(HY_UNROLL, h1)

    g_scr[0] = _dwconv_rows(x1_ref[0].astype(F32), w1_ref, b1_ref).astype(BF16)
    g_scr[1] = _dwconv_rows(x2_ref[0].astype(F32), w2_ref, b2_ref).astype(BF16)
    u = _dwconv_rows(v_ref[0].astype(F32), wv_ref, bv_ref)
    for i in range(h1):
        x_scr[i * px:i * px + n2n, :] = u[i * n2n:(i + 1) * n2n]

    for o in range(HY_ORDER):
        def stage_a(t, carry):
            for jj in range(HY_UNROLL):
                n2 = t * HY_UNROLL + jj
                xs = x_scr[pl.ds(n2, h1, stride=px), :].astype(BF16)
                a_scr[pl.ds(al(n2 * pa), 2 * n1n), :] = _dot(fa_ref[n2], xs)
            return carry

        def stage_c(t, carry):
            for jj in range(unroll_c):
                k1 = t * unroll_c + jj
                a = jnp.concatenate([a_scr[pl.ds(k1, n2n, stride=pa), :],
                                     a_scr[pl.ds(n1n + k1, n2n, stride=pa), :]], axis=0).astype(BF16)
                x = _dot(f2_ref[...], a)
                xr, xi = x[0:n2n], x[n2n:]
                hr = spec_ref[o, k1, 0:n2n, :].astype(F32)
                hi = spec_ref[o, k1, n2n:, :].astype(F32)
                y = jnp.concatenate([xr * hr - xi * hi, xr * hi + xi * hr], axis=0).astype(BF16)
                z_scr[pl.ds(al(k1 * pz), 2 * n2n), :] = _dot(f2i_ref[...], y)
            return carry

        def stage_a_inv(t, carry):
            for jj in range(HY_UNROLL):
                n2 = t * HY_UNROLL + jj
                z = jnp.concatenate([z_scr[pl.ds(n2, n1n, stride=pz), :],
                                     z_scr[pl.ds(n2n + n2, n1n, stride=pz), :]], axis=0).astype(BF16)
                y_scr[pl.ds(al(n2 * px), h1), :] = _dot(fat_ref[n2], z)
            return carry

        def gate(t, carry):
            for jj in range(unroll_g):
                n1 = t * unroll_g + jj
                conv = y_scr[pl.ds(n1, n2n, stride=px), :]
                r0 = al(n1 * px)
                g = g_scr[o, pl.ds(pl.multiple_of(n1 * n2n, n2n), n2n), :].astype(F32)
                zn = g * (conv + x_scr[pl.ds(r0, n2n), :] * bias_ref[o:o + 1, :])
                if o + 1 < HY_ORDER:
                    x_scr[pl.ds(r0, n2n), :] = zn
                else:
                    o_ref[0, pl.ds(pl.multiple_of(n1 * n2n, n2n), n2n), :] = zn.astype(o_ref.dtype)
            return carry

        lax.fori_loop(0, n2n // HY_UNROLL, stage_a, 0)
        lax.fori_loop(0, n1n // unroll_c, stage_c, 0)
        lax.fori_loop(0, n2n // HY_UNROLL, stage_a_inv, 0)
        lax.fori_loop(0, h1 // unroll_g, gate, 0)


def _hyena_lat(p, prm):
    conv_w, conv_b, w1, b1, w2, b2, w3, freq, deltas, bias = prm
    bsz, n, ch = p.shape
    c = W_GROUP
    h1 = n // FFT_N2
    fa, fat, f2, f2i = _fft_consts(n)
    n1n = fa.shape[1] // 2
    kf = _hy_filter(n, (w1, b1, w2, b2, w3, freq, deltas))
    ak = _fft_a(kf.reshape(h1, FFT_N2 * 4 * c), fa, 4 * c)
    spec = _fft_spec(ak.reshape(1, 2, n1n, FFT_N2, 4 * c), kf[0:1], f2)
    ns = c // HY_SLAB
    px, pa, pz = FFT_N2 + PITCH_PAD, 2 * n1n + PITCH_PAD, 2 * FFT_N2 + PITCH_PAD
    once = pl.Buffered(1)
    part = lambda a, k, rows: pl.BlockSpec((rows, HY_SLAB), lambda s, b: (0, k * ns + s))
    data = lambda k: pl.BlockSpec((1, n, HY_SLAB), lambda s, b: (b, 0, k * ns + s))
    full = lambda a: pl.BlockSpec(a.shape, lambda s, b: (0,) * a.ndim, pipeline_mode=once)
    cb = conv_b.reshape(1, ch)
    return _call(
        functools.partial(_hy_fused_kernel, n=n), name="hy_fused", grid=(ns, bsz),
        in_specs=[data(0), data(1), data(2), part(conv_w, 0, 3), part(conv_w, 1, 3), part(conv_w, 2, 3),
                  part(cb, 0, 1), part(cb, 1, 1), part(cb, 2, 1),
                  pl.BlockSpec((HY_ORDER, n1n, 2 * FFT_N2, HY_SLAB), lambda s, b: (0, 0, 0, s), pipeline_mode=once),
                  full(fa), full(fat), full(f2), full(f2i),
                  pl.BlockSpec((HY_ORDER, HY_SLAB), lambda s, b: (0, s))],
        out_specs=pl.BlockSpec((1, n, HY_SLAB), lambda s, b: (b, 0, s)),
        out_shape=jax.ShapeDtypeStruct((bsz, n, c), BF16),
        scratch_shapes=[pltpu.VMEM((h1 * px, HY_SLAB), F32), pltpu.VMEM((FFT_N2 * pa, HY_SLAB), F32),
                        pltpu.VMEM((n1n * pz, HY_SLAB), F32), pltpu.VMEM((FFT_N2 * px, HY_SLAB), F32),
                        pltpu.VMEM((HY_ORDER, n, HY_SLAB), BF16)])(
        p, p, p, conv_w, conv_w, conv_w, cb, cb, cb, spec, fa, fat, f2, f2i, bias)


def _hy_ctx_kernel(p_ref, cw_ref, cb_ref, k_ref, fd_ref, fdi_ref, bias_ref, o_ref):
    c = W_GROUP
    u = _dwconv_rows(p_ref[0].astype(F32), cw_ref, cb_ref)
    nf = fd_ref.shape[0] // 2
    xk = _dot(fd_ref[...], k_ref[...])
    z = u[:, 2 * c:]
    for o, gate in enumerate((u[:, 0:c], u[:, c:2 * c])):
        kf, kb = xk[:, 2 * o * c:(2 * o + 1) * c], xk[:, (2 * o + 1) * c:(2 * o + 2) * c]
        k0 = k_ref[0:1, (2 * o + 1) * c:(2 * o + 2) * c].astype(F32)
        hr = kf[0:nf] + kb[0:nf] - k0
        hi = kf[nf:] - kb[nf:]
        x = _dot(fd_ref[...], z.astype(BF16))
        xr, xi = x[0:nf], x[nf:]
        y = jnp.concatenate([xr * hr - xi * hi, xr * hi + xi * hr], axis=0).astype(BF16)
        z = gate * (_dot(fdi_ref[...], y) + z * bias_ref[o:o + 1, :])
    o_ref[0] = z.astype(o_ref.dtype)


def _hyena_ctx(p, prm):
    conv_w, conv_b, w1, b1, w2, b2, w3, freq, deltas, bias = prm
    bsz, n, ch = p.shape
    kf = _hy_filter(n, (w1, b1, w2, b2, w3, freq, deltas))
    nn = 2 * n
    th = 2.0 * np.pi * ((np.arange(nn)[:, None] * np.arange(n)[None, :]) % nn) / nn
    fd = np.concatenate([np.cos(th), -np.sin(th)], axis=0)
    fdj = jnp.asarray(fd, F32).astype(BF16)
    fdi = jnp.asarray(fd.T / nn, F32).astype(BF16)
    ins = [p, conv_w, conv_b.reshape(1, ch), kf, fdj, fdi, bias]
    full = lambda a: pl.BlockSpec(a.shape, lambda b: (0,) * a.ndim)
    return _call(
        _hy_ctx_kernel, name="hy_ctx", grid=(bsz,),
        in_specs=[pl.BlockSpec((1, n, ch), lambda b: (b, 0, 0))] + [full(a) for a in ins[1:]],
        out_specs=pl.BlockSpec((1, n, W_GROUP), lambda b: (b, 0, 0)),
        out_shape=jax.ShapeDtypeStruct((bsz, n, W_GROUP), BF16))(*ins)


def _outproj_kernel(y0_ref, y1_ref, y2_ref, y3_ref, w_ref, x_ref, mod_ref, g_ref, o_ref):
    c = W_GROUP
    y = (_dot(y0_ref[0], w_ref[0:c, :]) + _dot(y1_ref[0], w_ref[c:2 * c, :])
         + _dot(y2_ref[0], w_ref[2 * c:3 * c, :]) + _dot(y3_ref[0], w_ref[3 * c:, :]))
    o_ref[0] = x_ref[0] + mod_ref[0, 2:3, :] * _rms(y, g_ref[...])


def _outproj(ys, w, x, mod, g):
    bsz, n, _ = x.shape
    tm = min(1024, n)
    row = lambda width: pl.BlockSpec((1, tm, width), lambda b, i: (b, i, 0))
    return _call(
        _outproj_kernel, name="outproj", grid=(bsz, n // tm),
        in_specs=[row(W_GROUP)] * 4 + [pl.BlockSpec(w.shape, lambda b, i: (0, 0)), row(D_MODEL),
                                       pl.BlockSpec((1, 6, D_MODEL), lambda b, i: (b, 0, 0)),
                                       pl.BlockSpec((1, D_MODEL), lambda b, i: (0, 0))],
        out_specs=row(D_MODEL), out_shape=jax.ShapeDtypeStruct(x.shape, F32))(*ys, w, x, mod, g)


def _ffn_kernel(xp_ref, x_ref, xn_ref, mod_ref, g2_ref, g3_ref, wup_ref, cw_ref, cb_ref, wdn_ref, perm_ref, permt_ref,
                o_ref, h_scr, u_scr, acc_scr, *, tm, nchunk):
    i, nb = pl.program_id(1), pl.num_programs(1)
    shift, scale = mod_ref[0, 3:4, :], mod_ref[0, 4:5, :]
    pre = lambda x: _rms(x, g2_ref[...]) * (1.0 + scale) + shift
    h_scr[0:tm, :] = _dot(perm_ref[...], pre(x_ref[0]).astype(BF16)).astype(BF16)
    before = pre(xp_ref[0])[7:8] * (i > 0).astype(F32)
    after = pre(xn_ref[0])[0:1] * (i < nb - 1).astype(F32)
    hrow = lax.broadcasted_iota(jnp.int32, (16, D_MODEL), 0)
    h_scr[tm:, :] = jnp.where(hrow == 0, before, jnp.where(hrow == 1, after, 0.0)).astype(BF16)
    sub = lax.broadcasted_iota(jnp.int32, (8, 2 * FF_CHUNK), 0)

    def cols(c):
        return (slice(c * FF_CHUNK, (c + 1) * FF_CHUNK), slice(D_FF + c * FF_CHUNK, D_FF + (c + 1) * FF_CHUNK))

    def up(c):
        ga, va = cols(c)
        hb = h_scr[...]
        r = jnp.concatenate([_dot(hb, wup_ref[:, ga]), _dot(hb, wup_ref[:, va])], axis=1)
        ub = u_scr.at[c % nbuf]
        ub[8:8 + tm, :] = r[0:tm]
        ub[0:8, :] = jnp.where(sub == 0, r[tm:tm + 1], pltpu.roll(r[tm - 8:tm], 1, axis=0))
        ub[8 + tm:, :] = jnp.where(sub == 7, r[tm + 1:tm + 2], pltpu.roll(r[0:8], 7, axis=0))

    nbuf = FF_AHEAD + 1
    for c in range(min(FF_AHEAD, nchunk)):
        up(c)
    for c in range(nchunk):
        ub = u_scr.at[c % nbuf]
        if c + FF_AHEAD < nchunk:
            up(c + FF_AHEAD)
        ga, va = cols(c)
        cw = jnp.concatenate([cw_ref[:, ga], cw_ref[:, va]], axis=1)
        cb = jnp.concatenate([cb_ref[:, ga], cb_ref[:, va]], axis=1)
        u = ub[0:tm, :] * cw[0:1, :] + ub[8:8 + tm, :] * cw[1:2, :] + ub[16:16 + tm, :] * cw[2:3, :] + cb
        act = (_silu(u[:, 0:FF_CHUNK]) * u[:, FF_CHUNK:]).astype(BF16)
        if c == 0:
            acc_scr[...] = _dot(act, wdn_ref[c])
        else:
            acc_scr[...] += _dot(act, wdn_ref[c])
    branch = (mod_ref[0, 5:6, :] * _rms(acc_scr[...], g3_ref[...])).astype(BF16)
    o_ref[0] = x_ref[0] + _dot(permt_ref[...], branch)


def _ffn(x, mod, g2, g3, w_up, conv_w, conv_b, w_down):
    bsz, n, _ = x.shape
    tm = min(256, n)
    nb = n // tm
    nchunk = D_FF // FF_CHUNK
    t8 = tm // 8
    pm = np.zeros((tm, tm), np.float32)
    for j in range(t8):
        for s in range(8):
            pm[8 * j + s, s * t8 + j] = 1.0
    perm, permt = jnp.asarray(pm).astype(BF16), jnp.asarray(pm.T).astype(BF16)
    wup, cw, cb = w_up.astype(BF16), conv_w, conv_b.reshape(1, 2 * D_FF)
    wdn = w_down.reshape(nchunk, FF_CHUNK, D_MODEL).astype(BF16)
    r8 = tm // 8
    last8 = n // 8 - 1
    full = lambda a: pl.BlockSpec(a.shape, lambda b, i: (0,) * a.ndim)
    vec = pl.BlockSpec((1, D_MODEL), lambda b, i: (0, 0))
    return _call(
        functools.partial(_ffn_kernel, tm=tm, nchunk=nchunk), name="convffn", grid=(bsz, nb),
        in_specs=[pl.BlockSpec((1, 8, D_MODEL), lambda b, i: (b, jnp.maximum(i * r8 - 1, 0), 0)),
                  pl.BlockSpec((1, tm, D_MODEL), lambda b, i: (b, i, 0)),
                  pl.BlockSpec((1, 8, D_MODEL), lambda b, i: (b, jnp.minimum((i + 1) * r8, last8), 0)),
                  pl.BlockSpec((1, 6, D_MODEL), lambda b, i: (b, 0, 0)), vec, vec,
                  full(wup), full(cw), full(cb), full(wdn), full(perm), full(permt)],
        out_specs=pl.BlockSpec((1, tm, D_MODEL), lambda b, i: (b, i, 0)),
        out_shape=jax.ShapeDtypeStruct(x.shape, F32),
        scratch_shapes=[pltpu.VMEM((tm + 16, D_MODEL), BF16), pltpu.VMEM((FF_AHEAD + 1, tm + 16, 2 * FF_CHUNK), F32),
                        pltpu.VMEM((tm, D_MODEL), F32)])(x, x, x, mod, g2, g3, wup, cw, cb, wdn, perm, permt)


def kernel(x, c, ctx, c_ctx, ada_w, ada_b, norm_g, w_in, w_out, s5_lam_re, s5_lam_im, s5_log_dt, s5_b_re, s5_b_im, s5_c_re, s5_c_im, s5_d, s5_glu_w, s5_glu_b, hy_conv_w, hy_conv_b, hy_w1, hy_b1, hy_w2, hy_b2, hy_w3, hy_freq, hy_deltas, hy_bias, ret_decay_exp, ret_gn_g, mla_q_norm_g, mla_kv_norm_g, mla_w_uq, mla_w_ukv, ffn_w_up, ffn_conv_w, ffn_conv_b, ffn_w_down):
    depth = ada_w.shape[0]
    mod_lat, mod_ctx = _adaln(c, c_ctx, ada_w, ada_b)
    ret_rope, mla_rope = _ret_rope_tables(x.shape[1]), _mla_rope_tables(x.shape[1])
    for l in range(depth):
        need_ctx = l < depth - 1
        ml, mc = mod_lat[l], mod_ctx[l]
        g = lambda j: norm_g[l, j].reshape(1, D_MODEL)
        w_in_l = _perm_w_in(w_in[l])
        pl_s5, pl_hy, pl_ret, pl_mla = _inproj(x, ml, g(0), w_in_l)
        pc_s5, pc_hy, pc_ret, pc_mla = _inproj(ctx, mc, g(0), w_in_l)
        s5_p = (s5_lam_re[l], s5_lam_im[l], s5_log_dt[l], s5_b_re[l], s5_b_im[l], s5_c_re[l], s5_c_im[l],
                s5_d[l], s5_glu_w[l], s5_glu_b[l])
        hy_p = (hy_conv_w[l], hy_conv_b[l], hy_w1[l], hy_b1[l], hy_w2[l], hy_b2[l], hy_w3[l], hy_freq[l],
                hy_deltas[l], hy_bias[l])
        y_s5, yc_s5 = _s5_mixer(pl_s5, pc_s5, s5_p, need_ctx)
        y_hy = _hyena_lat(pl_hy, hy_p)
        y_ret, yc_ret = _ret_mixer(pl_ret, pc_ret, (ret_decay_exp[l], ret_gn_g[l]), need_ctx, ret_rope)
        y_mla, yc_mla = _mla_mixer(pl_mla, pc_mla, (mla_q_norm_g[l], mla_kv_norm_g[l], mla_w_uq[l], mla_w_ukv[l]),
                                   need_ctx, mla_rope)
        w_out_l = w_out[l].astype(BF16)
        ffn_p = (ffn_w_up[l], ffn_conv_w[l], ffn_conv_b[l], ffn_w_down[l])
        x = _outproj((y_s5, y_hy, y_ret, y_mla), w_out_l, x, ml, g(1))
        x = _ffn(x, ml, g(2), g(3), *ffn_p)
        if need_ctx:
            yc_hy = _hyena_ctx(pc_hy, hy_p)
            ctx = _outproj((yc_s5, yc_hy, yc_ret, yc_mla), w_out_l, ctx, mc, g(1))
            ctx = _ffn(ctx, mc, g(2), g(3), *ffn_p)
    return x
```
